```python
import jax, jax.numpy as jnp
from jax import lax
import numpy as np

D_MODEL = 1024
BATCH = 8
SEQ = 2048
DEPTH = 1
DEC_BATCH = 32
DEC_SEQ = 8
PAST_LEN = 8192
PAGE_SIZE = 128

R_HEAD_DIM = 64
R_HEADS = D_MODEL // (2 * R_HEAD_DIM)
R_WIDTH = R_HEADS * R_HEAD_DIM
DECAY_LORA = 64
AAA_LORA = 64
GATE_LORA = 128
LNX_EPS = 64e-5
F_HEAD_DIM = 64
F_HEADS = D_MODEL // (2 * F_HEAD_DIM)
F_WIDTH = F_HEADS * F_HEAD_DIM
Q_BLOCK = 128
N_EXPERTS = 64
TOP_K = 6
N_GROUPS = 8
TOPK_GROUPS = 4
EXPERT_FF = 256
SHARED_FF = 256
ROUTED_SCALE = 2.5
NORM_EPS = 1e-6

OFF_R = 0
OFF_K = OFF_R + R_WIDTH
OFF_V = OFF_K + R_WIDTH
OFF_WLO = OFF_V + R_WIDTH
OFF_ALO = OFF_WLO + DECAY_LORA
OFF_GLO = OFF_ALO + AAA_LORA
R_COLS = OFF_GLO + GATE_LORA
OFF_FQ = R_COLS
OFF_FK = OFF_FQ + F_WIDTH
OFF_FV = OFF_FK + F_WIDTH
OFF_FF = OFF_FV + F_WIDTH
OFF_GR = OFF_FF + F_HEADS
OFF_GF = OFF_GR + D_MODEL
IN_COLS = OFF_GF + D_MODEL

kernel_name = 'rwkv7_fox_moe_adaln_step'

F32 = jnp.float32


def rmsnorm(x, g):
    xf = x.astype(F32)
    y = xf * lax.rsqrt(jnp.mean(xf * xf, axis=-1, keepdims=True) + NORM_EPS) * g.astype(F32)
    return y.astype(x.dtype)


def modulate(x, g, shift, scale):
    return rmsnorm(x, g) * (1 + scale[:, None]) + shift[:, None]


def wkv_scan(S0, r, w, k, v, kk, a):
    def step(S, inp):
        r_t, w_t, k_t, v_t, kk_t, a_t = inp
        sa = jnp.einsum('bhij,bhj->bhi', S, -kk_t)
        S = S * w_t[:, :, None, :] + sa[..., None] * (kk_t * a_t)[:, :, None, :] + v_t[..., None] * k_t[:, :, None, :]
        return S, jnp.einsum('bhij,bhj->bhi', S, r_t)
    xs = tuple(jnp.moveaxis(t.astype(F32), 1, 0) for t in (r, w, k, v, kk, a))
    S, ys = lax.scan(step, S0.astype(F32), xs)
    return S, jnp.moveaxis(ys, 0, 1)


def rwkv7_branch(Pm, S0, p):
    B, T, _ = Pm.shape
    Pm = Pm.astype(F32)
    hs = (B, T, R_HEADS, R_HEAD_DIM)
    r = Pm[..., OFF_R:OFF_K]
    k = Pm[..., OFF_K:OFF_V]
    v = Pm[..., OFF_V:OFF_WLO]
    wlo = Pm[..., OFF_WLO:OFF_ALO]
    alo = Pm[..., OFF_ALO:OFF_GLO]
    glo = Pm[..., OFF_GLO:R_COLS]
    w_log = -jax.nn.softplus(-(p['w0'] + jnp.tanh(wlo) @ p['w_w2'])) - 0.5
    decay = jnp.exp(-jnp.exp(w_log))
    a = jax.nn.sigmoid(p['a0'] + alo @ p['w_a2'])
    g = jax.nn.sigmoid(glo) @ p['w_g2']
    kk = (k * p['k_k']).reshape(hs)
    kk = kk / jnp.maximum(jnp.linalg.norm(kk, axis=-1, keepdims=True), 1e-12)
    k = k * (1 + (a - 1) * p['k_a'])
    r, k, v, a, decay = (t.reshape(hs) for t in (r, k, v, a, decay))
    S, y = wkv_scan(S0, r, decay, k, v, kk, a)
    mu = jnp.mean(y, axis=-1, keepdims=True)
    var = jnp.mean(jnp.square(y - mu), axis=-1, keepdims=True)
    y = ((y - mu) * lax.rsqrt(var + LNX_EPS)).reshape(B, T, R_WIDTH) * p['lnx_w'] + p['lnx_b']
    bonus = jnp.sum(r * k * p['r_k'], axis=-1, keepdims=True) * v
    y = (y + bonus.reshape(B, T, R_WIDTH)) * g
    return y, S


def fox_prompt(q, k, v, logf):
    B, S, H, Dh = q.shape
    nb = S // Q_BLOCK
    scale = Dh ** -0.5
    cum = jnp.cumsum(logf, axis=1)
    cum_k = jnp.transpose(cum, (0, 2, 1))
    kpos = jnp.arange(S)
    qb = q.reshape(B, nb, Q_BLOCK, H, Dh).transpose(1, 0, 2, 3, 4)
    cb = cum.reshape(B, nb, Q_BLOCK, H).transpose(1, 0, 3, 2)
    starts = jnp.arange(nb) * Q_BLOCK

    def block(args):
        qi, ci, st = args
        s = jnp.einsum('bqhd,bkhd->bhqk', qi, k).astype(F32) * scale + ci[..., None] - cum_k[:, :, None, :]
        qpos = st + jnp.arange(Q_BLOCK)
        s = jnp.where(kpos[None, :] <= qpos[:, None], s, -jnp.inf)
        pr = jax.nn.softmax(s, axis=-1).astype(v.dtype)
        return jnp.einsum('bhqk,bkhd->bqhd', pr, v)

    o = lax.map(block, (qb, cb, starts))
    return o.transpose(1, 0, 2, 3, 4).reshape(B, S, H, Dh)


def fox_sample(q, k, v, logf, past_k, past_v, past_logf):
    B, T, H, Dh = q.shape
    P = past_k.shape[1]
    scale = Dh ** -0.5
    past_logf = past_logf.astype(F32)
    rev = jnp.sum(past_logf, axis=1, keepdims=True) - jnp.cumsum(past_logf, axis=1)
    cnew = jnp.transpose(jnp.cumsum(logf, axis=1), (0, 2, 1))
    s_past = (jnp.einsum('bqhd,bkhd->bhqk', q, past_k).astype(F32) * scale
              + cnew[..., None] + jnp.transpose(rev, (0, 2, 1))[:, :, None, :])
    s_new = jnp.einsum('bqhd,bkhd->bhqk', q, k).astype(F32) * scale + cnew[..., None] - cnew[:, :, None, :]
    tri = jnp.arange(T)[None, :] <= jnp.arange(T)[:, None]
    s_new = jnp.where(tri, s_new, -jnp.inf)
    pr = jax.nn.softmax(jnp.concatenate([s_past, s_new], axis=-1), axis=-1)
    return (jnp.einsum('bhqk,bkhd->bqhd', pr[..., :P].astype(past_v.dtype), past_v)
            + jnp.einsum('bhqk,bkhd->bqhd', pr[..., P:].astype(v.dtype), v))


def moe_ffn(h, p):
    T = h.shape[0]
    scores = jax.nn.sigmoid((h @ p['w_router']).astype(F32))
    biased = scores + p['e_bias'].astype(F32)
    grp = biased.reshape(T, N_GROUPS, N_EXPERTS // N_GROUPS)
    gscore = jnp.sum(lax.top_k(grp, 2)[0], axis=-1)
    _, gidx = lax.top_k(gscore, TOPK_GROUPS)
    gmask = jnp.sum(jax.nn.one_hot(gidx, N_GROUPS, dtype=F32), axis=1) > 0
    emask = jnp.repeat(gmask, N_EXPERTS // N_GROUPS, axis=-1)
    _, eidx = lax.top_k(jnp.where(emask, biased, -jnp.inf), TOP_K)
    wsel = jnp.take_along_axis(scores, eidx, axis=-1)
    wsel = wsel / jnp.sum(wsel, axis=-1, keepdims=True) * ROUTED_SCALE
    gate = jnp.sum(jax.nn.one_hot(eidx, N_EXPERTS, dtype=F32) * wsel[..., None], axis=1)
    hg = jnp.einsum('td,edf->tef', h, p['w_exp_gate'])
    hu = jnp.einsum('td,edf->tef', h, p['w_exp_up'])
    act = jax.nn.silu(hg) * hu * gate[..., None].astype(hg.dtype)
    routed = jnp.einsum('tef,efd->td', act, p['w_exp_down'])
    shared = (jax.nn.silu(h @ p['w_sh_gate']) * (h @ p['w_sh_up'])) @ p['w_sh_down']
    return routed + shared


def decoder_layer(x, c, shift_prev, wkv0, attend, p):
    B, T, _ = x.shape
    mod = jax.nn.silu(c) @ p['w_ada'] + p['b_ada']
    sh1, sc1, g1, sh2, sc2, g2 = jnp.split(mod, 6, axis=-1)
    h = modulate(x, p['norm1_g'], sh1, sc1)
    P = h @ p['w_in']
    Pr = P[..., :R_COLS]
    prev_proj = shift_prev @ p['w_in'][:, :R_COLS]
    Pprev = jnp.concatenate([prev_proj[:, None].astype(Pr.dtype), Pr[:, :-1]], axis=1)
    Pm = Pr + (Pprev - Pr) * p['mu_shift']
    y_r, wkv_new = rwkv7_branch(Pm, wkv0, p)
    fs = (B, T, F_HEADS, F_HEAD_DIM)
    q = P[..., OFF_FQ:OFF_FK].reshape(fs)
    k = P[..., OFF_FK:OFF_FV].reshape(fs)
    v = P[..., OFF_FV:OFF_FF].reshape(fs)
    logf = jax.nn.log_sigmoid((P[..., OFF_FF:OFF_GR] + p['b_f']).astype(F32))
    y_f = attend(q, k, v, logf).reshape(B, T, F_WIDTH)
    merged = (jax.nn.sigmoid(P[..., OFF_GR:OFF_GF]) * (y_r.astype(x.dtype) @ p['w_br_r'])
              + jax.nn.sigmoid(P[..., OFF_GF:IN_COLS]) * (y_f @ p['w_br_f']))
    x = x + g1[:, None] * (merged @ p['w_out'])
    h2 = modulate(x, p['norm2_g'], sh2, sc2)
    x = x + g2[:, None] * lax.map(lambda t: moe_ffn(t, p), h2)
    y = rmsnorm(x, p['normf_g'])
    return y, k, v, logf, wkv_new, h[:, -1]


def setup_inputs(seed: int = 0) -> dict:
    key = jax.random.key(seed)
    ks = iter(jax.random.split(key, 64))
    nrm = lambda shape, s: jax.random.normal(next(ks), shape, F32) * s
    n_pages = PAST_LEN // PAGE_SIZE
    n_used = DEC_BATCH * n_pages
    n_pool = n_used + max(1, n_used // 4)
    page_table = jax.random.permutation(next(ks), n_pool)[:n_used].reshape(DEC_BATCH, n_pages).astype(jnp.int32)
    D = D_MODEL
    return {
        'x_prompt': nrm((BATCH, SEQ, D), 1.0),
        'x_sample': nrm((DEC_BATCH, DEC_SEQ, D), 1.0),
        'c_prompt': nrm((BATCH, D), 1.0),
        'c_sample': nrm((DEC_BATCH, D), 1.0),
        'cache_k': nrm((n_pool, PAGE_SIZE, F_HEADS, F_HEAD_DIM), 1.0),
        'cache_v': nrm((n_pool, PAGE_SIZE, F_HEADS, F_HEAD_DIM), 1.0),
        'cache_logf': jax.nn.log_sigmoid(2.0 + nrm((n_pool, PAGE_SIZE, F_HEADS), 0.5)),
        'page_table': page_table,
        'state_wkv': nrm((DEC_BATCH, R_HEADS, R_HEAD_DIM, R_HEAD_DIM), 0.3),
        'state_shift': nrm((DEC_BATCH, D), 1.0),
        'w_ada': nrm((D, 6 * D), 0.5 * D ** -0.5),
        'b_ada': nrm((6 * D,), 0.02),
        'norm1_g': 1.0 + nrm((D,), 0.05),
        'w_in': nrm((D, IN_COLS), D ** -0.5),
        'mu_shift': jax.random.uniform(next(ks), (R_COLS,), F32),
        'w0': -2.0 + nrm((R_WIDTH,), 1.0),
        'w_w2': nrm((DECAY_LORA, R_WIDTH), 0.1 * DECAY_LORA ** -0.5),
        'a0': nrm((R_WIDTH,), 0.5),
        'w_a2': nrm((AAA_LORA, R_WIDTH), 0.1 * AAA_LORA ** -0.5),
        'w_g2': nrm((GATE_LORA, R_WIDTH), GATE_LORA ** -0.5),
        'k_k': 0.85 + nrm((R_WIDTH,), 0.05),
        'k_a': 1.0 + nrm((R_WIDTH,), 0.05),
        'r_k': nrm((R_HEADS, R_HEAD_DIM), 0.1),
        'lnx_w': 1.0 + nrm((R_WIDTH,), 0.05),
        'lnx_b': nrm((R_WIDTH,), 0.02),
        'b_f': 2.0 + nrm((F_HEADS,), 0.5),
        'w_br_r': nrm((R_WIDTH, D), R_WIDTH ** -0.5),
        'w_br_f': nrm((F_WIDTH, D), F_WIDTH ** -0.5),
        'w_out': nrm((D, D), D ** -0.5),
        'norm2_g': 1.0 + nrm((D,), 0.05),
        'w_router': nrm((D, N_EXPERTS), D ** -0.5),
        'e_bias': nrm((N_EXPERTS,), 0.01),
        'w_exp_gate': nrm((N_EXPERTS, D, EXPERT_FF), D ** -0.5),
        'w_exp_up': nrm((N_EXPERTS, D, EXPERT_FF), D ** -0.5),
        'w_exp_down': nrm((N_EXPERTS, EXPERT_FF, D), EXPERT_FF ** -0.5),
        'w_sh_gate': nrm((D, SHARED_FF), D ** -0.5),
        'w_sh_up': nrm((D, SHARED_FF), D ** -0.5),
        'w_sh_down': nrm((SHARED_FF, D), SHARED_FF ** -0.5),
        'normf_g': 1.0 + nrm((D,), 0.05),
    }


def reference(x_prompt, x_sample, c_prompt, c_sample, cache_k, cache_v, cache_logf, page_table,
              state_wkv, state_shift, w_ada, b_ada, norm1_g, w_in, mu_shift, w0, w_w2, a0, w_a2, w_g2,
              k_k, k_a, r_k, lnx_w, lnx_b, b_f, w_br_r, w_br_f, w_out, norm2_g, w_router, e_bias,
              w_exp_gate, w_exp_up, w_exp_down, w_sh_gate, w_sh_up, w_sh_down, normf_g):
    p = dict(w_ada=w_ada, b_ada=b_ada, norm1_g=norm1_g, w_in=w_in, mu_shift=mu_shift, w0=w0, w_w2=w_w2,
             a0=a0, w_a2=w_a2, w_g2=w_g2, k_k=k_k, k_a=k_a, r_k=r_k, lnx_w=lnx_w, lnx_b=lnx_b, b_f=b_f,
             w_br_r=w_br_r, w_br_f=w_br_f, w_out=w_out, norm2_g=norm2_g, w_router=w_router, e_bias=e_bias,
             w_exp_gate=w_exp_gate, w_exp_up=w_exp_up, w_exp_down=w_exp_down, w_sh_gate=w_sh_gate,
             w_sh_up=w_sh_up, w_sh_down=w_sh_down, normf_g=normf_g)
    B = x_prompt.shape[0]
    for _ in range(DEPTH):
        yp, kp, vp, lfp, wkvp, shp = decoder_layer(
            x_prompt, c_prompt, jnp.zeros((B, D_MODEL), x_prompt.dtype),
            jnp.zeros((B, R_HEADS, R_HEAD_DIM, R_HEAD_DIM), F32), fox_prompt, p)
    nb, npg = page_table.shape
    past_k = cache_k[page_table].reshape(nb, npg * PAGE_SIZE, F_HEADS, F_HEAD_DIM)
    past_v = cache_v[page_table].reshape(nb, npg * PAGE_SIZE, F_HEADS, F_HEAD_DIM)
    past_lf = cache_logf[page_table].reshape(nb, npg * PAGE_SIZE, F_HEADS)
    attend_s = lambda q, k, v, lf: fox_sample(q, k, v, lf, past_k, past_v, past_lf)
    for _ in range(DEPTH):
        ys, ks_, vs, lfs, wkvs, shs = decoder_layer(x_sample, c_sample, state_shift, state_wkv, attend_s, p)
    return (yp, ys, kp, vp, lfp, wkvp, shp, ks_, vs, lfs, wkvs, shs)
```

```python
import functools

import jax
import jax.numpy as jnp
from jax import lax
from jax.experimental import pallas as pl
from jax.experimental.pallas import tpu as pltpu

F32 = jnp.float32
BF16 = jnp.bfloat16

HEAD_DIM = 64
N_HEADS = 8
WIDTH = N_HEADS * HEAD_DIM
DECAY_LORA = 64
AAA_LORA = 64
GATE_LORA = 128
R_COLS = 3 * WIDTH + DECAY_LORA + AAA_LORA + GATE_LORA
LNX_EPS = 64e-5
NORM_EPS = 1e-6
N_EXPERTS = 64
N_GROUPS = 8
GROUP_SIZE = N_EXPERTS // N_GROUPS
TOPK_GROUPS = 4
TOP_K = 6
ROUTED_SCALE = 2.5
LANES = 128
VMEM_LIMIT = 56 * 1024 * 1024


def _cparams(sem):
    return pltpu.CompilerParams(dimension_semantics=sem, vmem_limit_bytes=VMEM_LIMIT)


def _dot(a, b):
    return jnp.dot(a, b, preferred_element_type=F32)


def _dot_nt(a, b):
    return lax.dot_general(a, b, (((1,), (1,)), ((), ())), preferred_element_type=F32)


def _split2(x):
    hi = x.astype(BF16)
    lo = (x - hi.astype(F32)).astype(BF16)
    return hi, lo


def _split3(x):
    hi = x.astype(BF16)
    r = x - hi.astype(F32)
    mid = r.astype(BF16)
    lo = (r - mid.astype(F32)).astype(BF16)
    return hi, mid, lo


def _dot_x3(x, m):
    hi, mid, lo = _split3(x)
    return _dot(hi, m) + _dot(mid, m) + _dot(lo, m)


def _dot_3x(m, x):
    hi, mid, lo = _split3(x)
    return _dot(m, hi) + _dot(m, mid) + _dot(m, lo)


def _dot_hp(x, w):
    xh, xm, xl = _split3(x)
    wh, wl = _split2(w)
    return _dot(xh, wh) + (_dot(xh, wl) + _dot(xm, wh)) + (_dot(xm, wl) + _dot(xl, wh))


def _sigmoid(x):
    return 1.0 / (1.0 + jnp.exp(-x))


def _softplus(x):
    return jnp.maximum(x, 0.0) + jnp.log1p(jnp.exp(-jnp.abs(x)))


def _silu(x):
    return x * _sigmoid(x)


def _rmsnorm(x, g):
    return x * lax.rsqrt(jnp.mean(x * x, axis=-1, keepdims=True) + NORM_EPS) * g


def _dense_kernel(x_ref, w_ref, b_ref, o_ref, *, act):
    x = x_ref[...]
    if act:
        x = _silu(x)
    o_ref[...] = _dot_hp(x, w_ref[...]) + b_ref[...]


def _dense(x, w, b, act, tn=512):
    m, k = x.shape
    n = w.shape[1]
    assert n % tn == 0
    return pl.pallas_call(
        functools.partial(_dense_kernel, act=act),
        grid=(n // tn,),
        in_specs=[pl.BlockSpec((m, k), lambda j: (0, 0)),
                  pl.BlockSpec((k, tn), lambda j: (0, j)),
                  pl.BlockSpec((1, tn), lambda j: (0, j))],
        out_specs=pl.BlockSpec((m, tn), lambda j: (0, j)),
        out_shape=jax.ShapeDtypeStruct((m, n), F32),
        compiler_params=_cparams(("parallel",)),
        name="dense",
    )(x, w, b.reshape(1, n))


def _in_proj_kernel(x_ref, sh_ref, sc_ref, g_ref, w_ref, b_ref, o_ref, h_ref, h_scr, *, epilogue, h_rows):
    @pl.when(pl.program_id(2) == 0)
    def _():
        h = _rmsnorm(x_ref[0], g_ref[...]) * (1.0 + sc_ref[0]) + sh_ref[0]
        h_scr[...] = h.astype(BF16)
        h_ref[0] = h[h.shape[0] - h_rows:, :]

    acc = _dot(h_scr[...], w_ref[...])
    if epilogue == "sigmoid":
        acc = _sigmoid(acc)
    elif epilogue == "log_sigmoid":
        acc = -_softplus(-(acc + b_ref[...]))
    o_ref[0] = acc.astype(o_ref.dtype)


def _in_proj(x, sh, sc, g, w, bias, *, tm, tn, epilogue, out_dtype, h_rows):
    b, t, d = x.shape
    n = w.shape[1]
    assert t % tm == 0 and n % tn == 0
    per_tok = sh.shape[1] != 1
    mrows = tm if per_tok else 1
    mod_map = (lambda bi, i, j: (bi, i, 0)) if per_tok else (lambda bi, i, j: (bi, 0, 0))
    if bias is None:
        bias = jnp.zeros((1, n), F32)
    out, h = pl.pallas_call(
        functools.partial(_in_proj_kernel, epilogue=epilogue, h_rows=h_rows),
        grid=(b, t // tm, n // tn),
        in_specs=[pl.BlockSpec((1, tm, d), lambda bi, i, j: (bi, i, 0)),
                  pl.BlockSpec((1, mrows, d), mod_map),
                  pl.BlockSpec((1, mrows, d), mod_map),
                  pl.BlockSpec((1, d), lambda bi, i, j: (0, 0)),
                  pl.BlockSpec((d, tn), lambda bi, i, j: (0, j)),
                  pl.BlockSpec((1, tn), lambda bi, i, j: (0, j))],
        out_specs=[pl.BlockSpec((1, tm, tn), lambda bi, i, j: (bi, i, j)),
                   pl.BlockSpec((1, h_rows, d), lambda bi, i, j: (bi, 0, 0))],
        out_shape=[jax.ShapeDtypeStruct((b, t, n), out_dtype),
                   jax.ShapeDtypeStruct((b, h_rows, d), F32)],
        scratch_shapes=[pltpu.VMEM((tm, d), BF16)],
        compiler_params=_cparams(("parallel", "arbitrary", "arbitrary")),
        name="in_proj_" + epilogue,
    )(x, sh, sc, g.reshape(1, d), w, bias)
    return out, h


def _prep_kernel(pr_ref, prev_ref, mu_ref, w0_ref, ww2_ref, a0_ref, wa2_ref, wg2_ref, kk_ref, ka_ref, rk_ref,
                 seg_ref, r_o, w_o, k_o, v_o, kk_o, kka_o, g_o, bonus_o, carry):
    @pl.when(pl.program_id(1) == 0)
    def _():
        carry[...] = prev_ref[0]

    pr = pr_ref[0]
    tt = pr.shape[0]
    first = lax.broadcasted_iota(jnp.int32, pr.shape, 0) == 0
    pprev = jnp.where(first, carry[...], pltpu.roll(pr, 1, axis=0))
    carry[...] = pr[tt - 1:tt, :]
    pm = pr + (pprev - pr) * mu_ref[...]
    r = pm[:, 0:WIDTH]
    k = pm[:, WIDTH:2 * WIDTH]
    v = pm[:, 2 * WIDTH:3 * WIDTH]
    lo = pm[:, 3 * WIDTH:3 * WIDTH + LANES]
    glo = pm[:, 3 * WIDTH + LANES:3 * WIDTH + 2 * LANES]
    seg = seg_ref[...]
    w_log = -_softplus(-(w0_ref[...] + _dot(jnp.tanh(lo).astype(BF16), ww2_ref[...]))) - 0.5
    decay = jnp.exp(-jnp.exp(w_log))
    a = _sigmoid(a0_ref[...] + _dot(lo.astype(BF16), wa2_ref[...]))
    g = _dot(_sigmoid(glo).astype(BF16), wg2_ref[...])
    kk = k * kk_ref[...]
    kk = kk / jnp.maximum(jnp.sqrt(_dot_x3(kk * kk, seg)), 1e-12)
    k2 = k * (1.0 + (a - 1.0) * ka_ref[...])
    r_o[0] = r
    w_o[0] = decay
    k_o[0] = k2
    v_o[0] = v
    kk_o[0] = kk
    kka_o[0] = kk * a
    g_o[0] = g
    bonus_o[0] = _dot_x3(r * k2 * rk_ref[...], seg) * v


def _seg_ones(width, seg):
    i = jnp.arange(width) // seg
    return (i[:, None] == i[None, :]).astype(BF16)


def _rwkv_prep(pr, prev, p, tt):
    b, t, _ = pr.shape
    zpad = jnp.zeros((LANES - DECAY_LORA, WIDTH), F32)
    ww2 = jnp.concatenate([p["w_w2"], zpad], axis=0).astype(BF16)
    wa2 = jnp.concatenate([zpad, p["w_a2"]], axis=0).astype(BF16)
    row = lambda a: a.reshape(1, -1)
    consts = [row(p["mu_shift"]), row(p["w0"]), ww2, row(p["a0"]), wa2, p["w_g2"].astype(BF16),
              row(p["k_k"]), row(p["k_a"]), row(p["r_k"]), _seg_ones(WIDTH, HEAD_DIM)]
    const_specs = [pl.BlockSpec(c.shape, lambda bi, i: (0, 0)) for c in consts]
    o_spec = pl.BlockSpec((1, tt, WIDTH), lambda bi, i: (bi, i, 0))
    return pl.pallas_call(
        _prep_kernel,
        grid=(b, t // tt),
        in_specs=[pl.BlockSpec((1, tt, R_COLS), lambda bi, i: (bi, i, 0)),
                  pl.BlockSpec((1, 1, R_COLS), lambda bi, i: (bi, 0, 0))] + const_specs,
        out_specs=[o_spec] * 8,
        out_shape=[jax.ShapeDtypeStruct((b, t, WIDTH), F32)] * 8,
        scratch_shapes=[pltpu.VMEM((1, R_COLS), F32)],
        compiler_params=_cparams(("parallel", "arbitrary")),
        name="rwkv_prep",
    )(pr, prev.reshape(b, 1, R_COLS), *consts)


def _scan_kernel(kk_ref, w_ref, k_ref, kka_ref, r_ref, v_ref, s0_ref, y_ref, s_ref, *, steps, n_slabs):
    @pl.when(pl.program_id(0) == 0)
    def _():
        s_ref[...] = s0_ref[...]

    def step(t, carry):
        kk = kk_ref[t]
        w = w_ref[t]
        k = k_ref[t]
        kka = kka_ref[t]
        r = r_ref[t]
        rw = r * w
        c1 = jnp.sum(r * kka, axis=0, keepdims=True)
        c2 = jnp.sum(r * k, axis=0, keepdims=True)
        for n in range(n_slabs):
            s = s_ref[n]
            sa = -jnp.sum(s * kk, axis=0, keepdims=True)
            yp = jnp.sum(s * rw, axis=0, keepdims=True)
            vn = v_ref[t, pl.ds(n, 1), :]
            s_ref[n] = s * w + sa * kka + vn * k
            y_ref[t, pl.ds(n, 1), :] = yp + c1 * sa + c2 * vn
        return carry

    lax.fori_loop(0, steps, step, 0)


def _wkv_scan(r, w, k, v, kk, kka, s0, tc):
    b, t, _ = r.shape
    bh = b * N_HEADS
    dup = max(1, LANES // bh)
    lanes = dup * bh
    ni = HEAD_DIM // dup

    def key_layout(x):
        x = x.reshape(b, t, N_HEADS, HEAD_DIM).transpose(1, 3, 0, 2).reshape(t, HEAD_DIM, bh)
        return jnp.concatenate([x] * dup, axis=-1)

    v_l = v.reshape(b, t, N_HEADS, ni, dup).transpose(1, 3, 4, 0, 2).reshape(t, ni, lanes)
    s_l = s0.reshape(b, N_HEADS, ni, dup, HEAD_DIM).transpose(2, 4, 3, 0, 1).reshape(ni, HEAD_DIM, lanes)
    key_spec = pl.BlockSpec((tc, HEAD_DIM, lanes), lambda c: (c, 0, 0))
    val_spec = pl.BlockSpec((tc, ni, lanes), lambda c: (c, 0, 0))
    st_spec = pl.BlockSpec((ni, HEAD_DIM, lanes), lambda c: (0, 0, 0))
    y_l, s_out = pl.pallas_call(
        functools.partial(_scan_kernel, steps=tc, n_slabs=ni),
        grid=(t // tc,),
        in_specs=[key_spec] * 5 + [val_spec, st_spec],
        out_specs=[val_spec, st_spec],
        out_shape=[jax.ShapeDtypeStruct((t, ni, lanes), F32),
                   jax.ShapeDtypeStruct((ni, HEAD_DIM, lanes), F32)],
        compiler_params=_cparams(("arbitrary",)),
        name="wkv_scan",
    )(key_layout(kk), key_layout(w), key_layout(k), key_layout(kka), key_layout(r), v_l, s_l)
    y = y_l.reshape(t, ni, dup, b, N_HEADS).transpose(3, 0, 4, 1, 2).reshape(b, t, WIDTH)
    s_new = s_out.reshape(ni, HEAD_DIM, dup, b, N_HEADS).transpose(3, 4, 0, 2, 1).reshape(
        b, N_HEADS, HEAD_DIM, HEAD_DIM)
    return y, s_new


def _cumsum_kernel(x_ref, o_ref, carry):
    @pl.when(pl.program_id(1) == 0)
    def _():
        carry[...] = jnp.zeros_like(carry)

    x = x_ref[0]
    tb = x.shape[0]
    tril = (lax.broadcasted_iota(jnp.int32, (tb, tb), 1) <= lax.broadcasted_iota(jnp.int32, (tb, tb), 0)).astype(BF16)
    c = _dot_3x(tril, x) + carry[...]
    o_ref[0] = c
    carry[...] = c[tb - 1:tb, :]


def _cumsum_time(x, tb):
    b, t, n = x.shape
    spec = pl.BlockSpec((1, tb, n), lambda bi, i: (bi, i, 0))
    return pl.pallas_call(
        _cumsum_kernel, grid=(b, t // tb), in_specs=[spec], out_specs=spec,
        out_shape=jax.ShapeDtypeStruct((b, t, n), F32),
        scratch_shapes=[pltpu.VMEM((1, n), F32)],
        compiler_params=_cparams(("parallel", "arbitrary")),
        name="cumsum_time",
    )(x)


def _fox_prompt_kernel(q_ref, k_ref, v_ref, cq_ref, ck_ref, o_ref, *, tq, tk):
    i = pl.program_id(2)
    q = q_ref[0] * (HEAD_DIM ** -0.5)
    lane = lax.broadcasted_iota(jnp.int32, (tq, LANES), 1)
    qpos = i * tq + lax.broadcasted_iota(jnp.int32, (tq, tk), 0)
    kcol = lax.broadcasted_iota(jnp.int32, (tq, tk), 1)
    nblk = (i * tq + tq) // tk
    outs = []
    for hh in range(2):
        qm = jnp.where((lane >= hh * HEAD_DIM) & (lane < (hh + 1) * HEAD_DIM), q, 0.0).astype(BF16)
        cq = cq_ref[0, hh]

        def body(j, carry, qm=qm, cq=cq, hh=hh):
            m, l, acc = carry
            start = pl.multiple_of(j * tk, tk)
            kb = k_ref[0, pl.ds(start, tk), :].astype(BF16)
            vb = v_ref[0, pl.ds(start, tk), :].astype(BF16)
            s = _dot_nt(qm, kb) + cq - ck_ref[0, hh, pl.ds(j, 1), :]
            s = jnp.where(kcol + j * tk <= qpos, s, -jnp.inf)
            m_new = jnp.maximum(m, jnp.max(s, axis=-1, keepdims=True))
            p = jnp.exp(s - m_new)
            alpha = jnp.exp(m - m_new)
            l = alpha * l + jnp.sum(p, axis=-1, keepdims=True)
            acc = alpha * acc + _dot(p.astype(BF16), vb)
            return m_new, l, acc

        init = (jnp.full((tq, 1), -jnp.inf, F32), jnp.zeros((tq, 1), F32), jnp.zeros((tq, LANES), F32))
        _, l, acc = lax.fori_loop(0, nblk, body, init)
        outs.append(acc / l)
    o_ref[0] = jnp.where(lane < HEAD_DIM, outs[0], outs[1]).astype(o_ref.dtype)


def _fox_prompt(qkv, cum, tq, tk):
    b, t, _ = qkv.shape
    npair = WIDTH // LANES
    cq = cum.transpose(0, 2, 1).reshape(b, N_HEADS, t, 1)
    ck = cum.transpose(0, 2, 1).reshape(b, N_HEADS, t // tk, tk)
    return pl.pallas_call(
        functools.partial(_fox_prompt_kernel, tq=tq, tk=tk),
        grid=(b, npair, t // tq),
        in_specs=[pl.BlockSpec((1, tq, LANES), lambda bi, p, i: (bi, i, p)),
                  pl.BlockSpec((1, t, LANES), lambda bi, p, i: (bi, 0, npair + p)),
                  pl.BlockSpec((1, t, LANES), lambda bi, p, i: (bi, 0, 2 * npair + p)),
                  pl.BlockSpec((1, 2, tq, 1), lambda bi, p, i: (bi, p, i, 0)),
                  pl.BlockSpec((1, 2, t // tk, tk), lambda bi, p, i: (bi, p, 0, 0))],
        out_specs=pl.BlockSpec((1, tq, LANES), lambda bi, p, i: (bi, i, p)),
        out_shape=jax.ShapeDtypeStruct((b, t, WIDTH), BF16),
        compiler_params=_cparams(("parallel", "parallel", "arbitrary")),
        name="fox_prompt",
    )(qkv, qkv, qkv, cq, ck)


def _fox_sample_kernel(pt_ref, q_ref, kn_ref, vn_ref, lfn_ref, *rest, pages_per_step, page):
    npp = pages_per_step
    k_refs = rest[:npp]
    v_refs = rest[npp:2 * npp]
    lf_refs = rest[2 * npp:3 * npp]
    o_ref = rest[3 * npp]
    qrep, m_s, l_s, acc_s, suf_s, cn_s = rest[3 * npp + 1:]
    step = pl.program_id(1)
    rows = N_HEADS * q_ref.shape[1]
    nq = q_ref.shape[1]
    row_head = lax.broadcasted_iota(jnp.int32, (rows, WIDTH), 0) // nq
    lane_head = lax.broadcasted_iota(jnp.int32, (rows, WIDTH), 1) // HEAD_DIM
    key_i = lax.broadcasted_iota(jnp.int32, (page, page), 0)
    key_j = lax.broadcasted_iota(jnp.int32, (page, page), 1)

    def rep_heads(x):
        return jnp.concatenate([jnp.broadcast_to(x[h:h + 1, :], (nq, x.shape[1])) for h in range(N_HEADS)], axis=0)

    def update(s, vb):
        m_old = m_s[...]
        m_new = jnp.maximum(m_old, jnp.max(s, axis=-1, keepdims=True))
        p = jnp.exp(s - m_new)
        alpha = jnp.exp(m_old - m_new)
        l_s[...] = alpha * l_s[...] + jnp.sum(p, axis=-1, keepdims=True)
        acc_s[...] = alpha * acc_s[...] + _dot(p.astype(BF16), vb)
        m_s[...] = m_new

    @pl.when(step == 0)
    def _():
        q = q_ref[0] * (HEAD_DIM ** -0.5)
        qr = jnp.concatenate([q] * N_HEADS, axis=0)
        qrep[...] = jnp.where(row_head == lane_head, qr, 0.0).astype(BF16)
        lfn = rep_heads(lfn_ref[0])
        cn_row = _dot_x3(lfn, (key_i <= key_j).astype(BF16))
        rq = lax.broadcasted_iota(jnp.int32, (rows, page), 0) % nq
        kc = lax.broadcasted_iota(jnp.int32, (rows, page), 1)
        cn_col = jnp.sum(jnp.where(kc == rq, cn_row, 0.0), axis=-1, keepdims=True)
        cn_s[...] = cn_col
        suf_s[...] = jnp.zeros_like(suf_s)
        m_s[...] = jnp.full_like(m_s, -jnp.inf)
        l_s[...] = jnp.zeros_like(l_s)
        acc_s[...] = jnp.zeros_like(acc_s)
        s = _dot_nt(qrep[...], kn_ref[0].astype(BF16)) + cn_col - cn_row
        update(jnp.where(kc <= rq, s, -jnp.inf), vn_ref[0].astype(BF16))

    strict = (key_i > key_j).astype(BF16)
    for u in range(npp):
        lf = rep_heads(lf_refs[u][0])
        rev = _dot_x3(lf, strict) + suf_s[...]
        s = _dot_nt(qrep[...], k_refs[u][0].astype(BF16)) + cn_s[...] + rev
        update(s, v_refs[u][0].astype(BF16))
        suf_s[...] = suf_s[...] + jnp.sum(lf, axis=-1, keepdims=True)

    @pl.when(step == pl.num_programs(1) - 1)
    def _():
        o_full = acc_s[...] / l_s[...]
        o_sel = jnp.where(row_head == lane_head, o_full, 0.0)
        out = o_sel[0:nq]
        for h in range(1, N_HEADS):
            out = out + o_sel[h * nq:(h + 1) * nq]
        o_ref[0] = out.astype(o_ref.dtype)


def _fox_sample(q, k_new, v_new, logf_new, cache_k, cache_v, cache_logf, page_table, pages_per_step):
    b, tn, _ = q.shape
    n_pool, page = cache_k.shape[:2]
    n_pages = page_table.shape[1]
    npp = pages_per_step
    assert n_pages % npp == 0 and tn <= page
    ck = cache_k.reshape(n_pool, page, WIDTH)
    cv = cache_v.reshape(n_pool, page, WIDTH)
    clf = cache_logf.transpose(0, 2, 1)
    pad_rows = lambda x: jnp.pad(x, ((0, 0), (0, page - tn), (0, 0)))
    lfn = jnp.pad(logf_new.transpose(0, 2, 1), ((0, 0), (0, 0), (0, page - tn)))
    rows = N_HEADS * tn

    def page_map(u):
        return lambda bi, s, pt: (pt[bi, n_pages - 1 - (s * npp + u)], 0, 0)

    tok_spec = lambda r: pl.BlockSpec((1, r, WIDTH), lambda bi, s, pt: (bi, 0, 0))
    in_specs = ([tok_spec(tn), tok_spec(page), tok_spec(page),
                 pl.BlockSpec((1, N_HEADS, page), lambda bi, s, pt: (bi, 0, 0))]
                + [pl.BlockSpec((1, page, WIDTH), page_map(u)) for u in range(npp)]
                + [pl.BlockSpec((1, page, WIDTH), page_map(u)) for u in range(npp)]
                + [pl.BlockSpec((1, N_HEADS, page), page_map(u)) for u in range(npp)])
    grid_spec = pltpu.PrefetchScalarGridSpec(
        num_scalar_prefetch=1,
        grid=(b, n_pages // npp),
        in_specs=in_specs,
        out_specs=pl.BlockSpec((1, tn, WIDTH), lambda bi, s, pt: (bi, 0, 0)),
        scratch_shapes=[pltpu.VMEM((rows, WIDTH), BF16), pltpu.VMEM((rows, 1), F32), pltpu.VMEM((rows, 1), F32),
                        pltpu.VMEM((rows, WIDTH), F32), pltpu.VMEM((rows, 1), F32), pltpu.VMEM((rows, 1), F32)])
    return pl.pallas_call(
        functools.partial(_fox_sample_kernel, pages_per_step=npp, page=page),
        grid_spec=grid_spec,
        out_shape=jax.ShapeDtypeStruct((b, tn, WIDTH), BF16),
        compiler_params=_cparams(("parallel", "arbitrary")),
        name="fox_sample",
    )(page_table, q, pad_rows(k_new), pad_rows(v_new), lfn, *([ck] * npp), *([cv] * npp), *([clf] * npp))


def _merge_kernel(y_ref, bonus_ref, g_ref, yf_ref, gates_ref, x_ref, g1_ref, sh2_ref, sc2_ref,
                  lnw_ref, lnb_ref, seg_ref, wr_ref, wf_ref, wo_ref, n2_ref, wrt_ref,
                  x1_ref, h2_ref, lg_ref):
    seg = seg_ref[...]
    y = y_ref[0]
    mu = _dot_x3(y, seg) * (1.0 / HEAD_DIM)
    d = y - mu
    var = _dot_x3(d * d, seg) * (1.0 / HEAD_DIM)
    yn = d * lax.rsqrt(var + LNX_EPS) * lnw_ref[...] + lnb_ref[...]
    yr = ((yn + bonus_ref[0]) * g_ref[0]).astype(BF16)
    gates = gates_ref[0].astype(F32)
    d_model = x_ref.shape[2]
    merged = gates[:, :d_model] * _dot(yr, wr_ref[...]) + gates[:, d_model:] * _dot(yf_ref[0], wf_ref[...])
    x1 = x_ref[0] + g1_ref[0] * _dot(merged.astype(BF16), wo_ref[...])
    x1_ref[0] = x1
    h2 = _rmsnorm(x1, n2_ref[...]) * (1.0 + sc2_ref[0]) + sh2_ref[0]
    h2_ref[0] = h2.astype(BF16)
    hh, hl = _split2(h2)
    wrt = wrt_ref[...]
    lg_ref[0] = _dot(hh, wrt[0]) + (_dot(hh, wrt[1]) + _dot(hl, wrt[0]))


def _merge(y, bonus, g, yf, gates, x, g1, sh2, sc2, p, w_router_t, tm):
    b, t, d = x.shape
    per_tok = g1.shape[1] != 1
    mrows = tm if per_tok else 1
    mod_map = (lambda bi, i: (bi, i, 0)) if per_tok else (lambda bi, i: (bi, 0, 0))
    tok = lambda n: pl.BlockSpec((1, tm, n), lambda bi, i: (bi, i, 0))
    mod = pl.BlockSpec((1, mrows, d), mod_map)
    row = lambda a: a.reshape(1, -1)
    consts = [row(p["lnx_w"]), row(p["lnx_b"]), _seg_ones(WIDTH, HEAD_DIM), p["w_br_r"].astype(BF16),
              p["w_br_f"].astype(BF16), p["w_out"].astype(BF16), row(p["norm2_g"]), w_router_t]
    const_specs = [pl.BlockSpec(c.shape, (lambda bi, i: (0, 0)) if c.ndim == 2 else (lambda bi, i: (0, 0, 0)))
                   for c in consts]
    return pl.pallas_call(
        _merge_kernel,
        grid=(b, t // tm),
        in_specs=[tok(WIDTH), tok(WIDTH), tok(WIDTH), tok(WIDTH), tok(2 * d), tok(d), mod, mod, mod] + const_specs,
        out_specs=[tok(d), tok(d), tok(LANES)],
        out_shape=[jax.ShapeDtypeStruct((b, t, d), F32), jax.ShapeDtypeStruct((b, t, d), BF16),
                   jax.ShapeDtypeStruct((b, t, LANES), F32)],
        compiler_params=_cparams(("parallel", "parallel")),
        name="merge",
    )(y, bonus, g, yf, gates, x, g1, sh2, sc2, *consts)


def _topk_kernel(lg_ref, bias_ref, gate_ref):
    lt = lg_ref[...].T
    tm = lt.shape[1]
    score = _sigmoid(lt[:N_EXPERTS])
    biased = score + bias_ref[...][:N_EXPERTS]
    slab = [biased[k * N_GROUPS:(k + 1) * N_GROUPS] for k in range(GROUP_SIZE)]
    neg = jnp.full((N_GROUPS, tm), -jnp.inf, F32)
    m1 = functools.reduce(jnp.maximum, slab)
    taken = jnp.zeros((N_GROUPS, tm), jnp.bool_)
    m2 = neg
    for k in range(GROUP_SIZE):
        is_first = (slab[k] == m1) & jnp.logical_not(taken)
        taken = taken | is_first
        m2 = jnp.maximum(m2, jnp.where(is_first, neg, slab[k]))
    gs = m1 + m2
    g_iota = lax.broadcasted_iota(jnp.int32, (N_GROUPS, tm), 0)
    cnt = jnp.zeros((N_GROUPS, tm), jnp.int32)
    for g2 in range(N_GROUPS):
        other = gs[g2:g2 + 1, :]
        beats = (other > gs) | ((g_iota > g2) & (other == gs))
        cnt = cnt + beats.astype(jnp.int32)
    g_sel = cnt < TOPK_GROUPS
    cand = [jnp.where(g_sel, slab[k], neg) for k in range(GROUP_SIZE)]
    rank = [jnp.zeros((N_GROUPS, tm), jnp.int32) for _ in range(GROUP_SIZE)]
    for k2 in range(GROUP_SIZE):
        for g2 in range(N_GROUPS):
            other = cand[k2][g2:g2 + 1, :]
            for k in range(GROUP_SIZE):
                first = (g_iota >= g2) if k2 < k else (g_iota > g2)
                beats = (other > cand[k]) | (first & (other == cand[k]))
                rank[k] = rank[k] + beats.astype(jnp.int32)
    sel = [rank[k] < TOP_K for k in range(GROUP_SIZE)]
    sc = [score[k * N_GROUPS:(k + 1) * N_GROUPS] for k in range(GROUP_SIZE)]
    picked = [jnp.where(sel[k], sc[k], 0.0) for k in range(GROUP_SIZE)]
    total = jnp.sum(functools.reduce(jnp.add, picked), axis=0, keepdims=True)
    gate = jnp.concatenate([pk / total * ROUTED_SCALE for pk in picked], axis=0)
    pad_rows = lt.shape[0] - N_EXPERTS
    shared = (lax.broadcasted_iota(jnp.int32, (pad_rows, tm), 0) == 0).astype(F32)
    gate_ref[...] = jnp.concatenate([gate, shared], axis=0).T


def _topk(logits, e_bias_perm, tm):
    m = logits.shape[0]
    bias = jnp.pad(e_bias_perm, (0, LANES - N_EXPERTS)).reshape(LANES, 1)
    return pl.pallas_call(
        _topk_kernel,
        grid=(m // tm,),
        in_specs=[pl.BlockSpec((tm, LANES), lambda i: (i, 0)), pl.BlockSpec((LANES, 1), lambda i: (0, 0))],
        out_specs=pl.BlockSpec((tm, LANES), lambda i: (i, 0)),
        out_shape=jax.ShapeDtypeStruct((m, LANES), F32),
        compiler_params=_cparams(("parallel",)),
        name="topk",
    )(logits, bias)


def _moe_kernel(h_ref, gate_ref, wg_ref, wu_ref, wd_ref, x1_ref, g2_ref, nf_ref, y_ref, acc, *, ff):
    e = pl.program_id(2)

    @pl.when(e == 0)
    def _():
        acc[...] = jnp.zeros_like(acc)

    h = h_ref[0]
    hg = _dot(h, wg_ref[0])
    hu = _dot(h, wu_ref[0])
    pos = jnp.where(e < N_EXPERTS, (e % GROUP_SIZE) * N_GROUPS + e // GROUP_SIZE, N_EXPERTS)
    sel = (lax.broadcasted_iota(jnp.int32, (LANES, ff), 0) == pos).astype(BF16)
    gh, gl = _split2(gate_ref[0])
    gb = _dot(gh, sel) + _dot(gl, sel)
    act = _silu(hg) * hu * gb
    acc[...] += _dot(act.astype(BF16), wd_ref[0])

    @pl.when(e == pl.num_programs(2) - 1)
    def _():
        x2 = x1_ref[0] + g2_ref[0] * acc[...]
        y_ref[0] = _rmsnorm(x2, nf_ref[...])


def _moe(h2, gate, wg, wu, wd, x1, g2, normf_g, tm):
    b, t, d = x1.shape
    ne, _, ff = wg.shape
    per_tok = g2.shape[1] != 1
    mrows = tm if per_tok else 1
    mod_map = (lambda bi, i, e: (bi, i, 0)) if per_tok else (lambda bi, i, e: (bi, 0, 0))
    tok = lambda n: pl.BlockSpec((1, tm, n), lambda bi, i, e: (bi, i, 0))
    return pl.pallas_call(
        functools.partial(_moe_kernel, ff=ff),
        grid=(b, t // tm, ne),
        in_specs=[tok(d), tok(LANES),
                  pl.BlockSpec((1, d, ff), lambda bi, i, e: (e, 0, 0)),
                  pl.BlockSpec((1, d, ff), lambda bi, i, e: (e, 0, 0)),
                  pl.BlockSpec((1, ff, d), lambda bi, i, e: (e, 0, 0)),
                  tok(d), pl.BlockSpec((1, mrows, d), mod_map),
                  pl.BlockSpec((1, d), lambda bi, i, e: (0, 0))],
        out_specs=tok(d),
        out_shape=jax.ShapeDtypeStruct((b, t, d), F32),
        scratch_shapes=[pltpu.VMEM((tm, d), F32)],
        compiler_params=_cparams(("parallel", "parallel", "arbitrary")),
        name="moe",
    )(h2, gate.reshape(b, t, LANES), wg, wu, wd, x1, g2, normf_g.reshape(1, d))


def _layer(x, mod, shift_prev, wkv0, attend, p, w, cfg):
    b, t, d = x.shape
    bx, tx = cfg["rows"]
    tm = cfg["tm"]
    xr = x.reshape(bx, tx, d)
    if bx == b:
        part = lambda i: mod[:, i:i + 1, :]
    else:
        part = lambda i: jnp.repeat(mod[:, i, :], t, axis=0).reshape(bx, tx, d)
    sh1, sc1, g1, sh2, sc2, g2 = (part(i) for i in range(6))
    h_rows = 1 if bx == b else tx
    proj = functools.partial(_in_proj, xr, sh1, sc1, p["norm1_g"], tm=tm, h_rows=h_rows)
    pr, h_keep = proj(w["in_r"], None, tn=cfg["tn_r"], epilogue="none", out_dtype=F32)
    qkv, _ = proj(w["in_qkv"], None, tn=512, epilogue="none", out_dtype=F32)
    logf_pad, _ = proj(w["in_f"], w["b_f_pad"], tn=LANES, epilogue="log_sigmoid", out_dtype=F32)
    gates, _ = proj(w["in_g"], None, tn=512, epilogue="sigmoid", out_dtype=BF16)
    shift_new = h_keep[:, 0, :] if bx == b else h_keep.reshape(b, t, d)[:, -1, :]
    logf = logf_pad.reshape(b, t, LANES)[:, :, :N_HEADS]
    qkv = qkv.reshape(b, t, 3 * WIDTH)

    prev = _dense(shift_prev, w["in_r_f32"], jnp.zeros((R_COLS,), F32), act=False, tn=R_COLS // 2)
    r, dec, k2, v, kk, kka, g, bonus = _rwkv_prep(pr.reshape(b, t, R_COLS), prev, p, cfg["tt"])
    y_scan, wkv_new = _wkv_scan(r, dec, k2, v, kk, kka, wkv0, cfg["tc"])

    y_f = attend(qkv, logf)

    rs = lambda a: a.reshape(bx, tx, a.shape[-1])
    x1, h2, logits = _merge(rs(y_scan), rs(bonus), rs(g), rs(y_f), gates, xr, g1, sh2, sc2, p, w["router_t"], tm)
    gate = _topk(logits.reshape(bx * tx, LANES), w["e_bias_perm"], cfg["tm_topk"])
    y = _moe(h2, gate, w["exp_gate"], w["exp_up"], w["exp_down"], x1, g2, p["normf_g"], cfg["tm_moe"])
    k_out = qkv[:, :, WIDTH:2 * WIDTH].reshape(b, t, N_HEADS, HEAD_DIM)
    v_out = qkv[:, :, 2 * WIDTH:].reshape(b, t, N_HEADS, HEAD_DIM)
    return y.reshape(b, t, d), k_out, v_out, logf, wkv_new, shift_new


def kernel(x_prompt, x_sample, c_prompt, c_sample, cache_k, cache_v, cache_logf, page_table, state_wkv, state_shift, w_ada, b_ada, norm1_g, w_in, mu_shift, w0, w_w2, a0, w_a2, w_g2, k_k, k_a, r_k, lnx_w, lnx_b, b_f, w_br_r, w_br_f, w_out, norm2_g, w_router, e_bias, w_exp_gate, w_exp_up, w_exp_down, w_sh_gate, w_sh_up, w_sh_down, normf_g):
    p = dict(norm1_g=norm1_g, mu_shift=mu_shift, w0=w0, w_w2=w_w2, a0=a0, w_a2=w_a2, w_g2=w_g2, k_k=k_k, k_a=k_a,
             r_k=r_k, lnx_w=lnx_w, lnx_b=lnx_b, w_br_r=w_br_r, w_br_f=w_br_f, w_out=w_out, norm2_g=norm2_g,
             normf_g=normf_g)
    bp, tp, d = x_prompt.shape
    bs, ts, _ = x_sample.shape
    off_f = R_COLS + 3 * WIDTH
    off_g = off_f + N_HEADS
    perm = lambda a: a.reshape(a.shape[:-1] + (N_GROUPS, GROUP_SIZE)).swapaxes(-1, -2).reshape(a.shape)
    router = jnp.pad(perm(w_router), ((0, 0), (0, LANES - N_EXPERTS)))
    r_hi = router.astype(BF16)
    w = dict(
        in_r=w_in[:, :R_COLS].astype(BF16), in_r_f32=w_in[:, :R_COLS],
        in_qkv=w_in[:, R_COLS:off_f].astype(BF16),
        in_f=jnp.pad(w_in[:, off_f:off_g], ((0, 0), (0, LANES - N_HEADS))).astype(BF16),
        in_g=w_in[:, off_g:].astype(BF16),
        b_f_pad=jnp.pad(b_f, (0, LANES - N_HEADS)).reshape(1, LANES),
        router_t=jnp.stack([r_hi, (router - r_hi.astype(F32)).astype(BF16)]),
        e_bias_perm=perm(e_bias),
        exp_gate=jnp.concatenate([w_exp_gate, w_sh_gate[None]], axis=0).astype(BF16),
        exp_up=jnp.concatenate([w_exp_up, w_sh_up[None]], axis=0).astype(BF16),
        exp_down=jnp.concatenate([w_exp_down, w_sh_down[None]], axis=0).astype(BF16),
    )
    mod = _dense(jnp.concatenate([c_prompt, c_sample], axis=0), w_ada, b_ada, act=True).reshape(bp + bs, 6, d)

    def attend_prompt(qkv, logf):
        lf = jnp.pad(logf, ((0, 0), (0, 0), (0, LANES - N_HEADS)))
        cum = _cumsum_time(lf, 256)[:, :, :N_HEADS]
        return _fox_prompt(qkv, cum, 256, 256)

    def attend_sample(qkv, logf):
        return _fox_sample(qkv[:, :, :WIDTH], qkv[:, :, WIDTH:2 * WIDTH], qkv[:, :, 2 * WIDTH:], logf,
                           cache_k, cache_v, cache_logf, page_table, 4)

    cfg_p = dict(rows=(bp, tp), tm=min(512, tp), tn_r=R_COLS // 2, tt=min(256, tp), tc=16,
                 tm_topk=min(1024, tp), tm_moe=min(1024, tp))
    cfg_s = dict(rows=(1, bs * ts), tm=bs * ts, tn_r=R_COLS // 2, tt=ts, tc=ts, tm_topk=bs * ts, tm_moe=bs * ts)
    yp, kp, vp, lfp, wkvp, shp = _layer(x_prompt, mod[:bp], jnp.zeros((bp, d), F32),
                                        jnp.zeros((bp, N_HEADS, HEAD_DIM, HEAD_DIM), F32), attend_prompt, p, w, cfg_p)
    ys, ks, vs, lfs, wkvs, shs = _layer(x_sample, mod[bp:], state_shift, state_wkv, attend_sample, p, w, cfg_s)
    return (yp, ys, kp, vp, lfp, wkvp, shp, ks, vs, lfs, wkvs, shs)
```

```python
import functools

import jax
import jax.numpy as jnp
from jax import lax
from jax.experimental import pallas as pl
from jax.experimental.pallas import tpu as pltpu

F32 = jnp.float32
BF16 = jnp.bfloat16

HEAD_DIM = 64
N_HEADS = 8
WIDTH = N_HEADS * HEAD_DIM
DECAY_LORA = 64
AAA_LORA = 64
GATE_LORA = 128
R_COLS = 3 * WIDTH + DECAY_LORA + AAA_LORA + GATE_LORA
LNX_EPS = 64e-5
NORM_EPS = 1e-6
N_EXPERTS = 64
N_GROUPS = 8
GROUP_SIZE = N_EXPERTS // N_GROUPS
TOPK_GROUPS = 4
TOP_K = 6
ROUTED_SCALE = 2.5
LANES = 128
VMEM_LIMIT = 56 * 1024 * 1024


def _cparams(sem):
    return pltpu.CompilerParams(dimension_semantics=sem, vmem_limit_bytes=VMEM_LIMIT)


def _dot(a, b):
    return jnp.dot(a, b, preferred_element_type=F32)


def _dot_nt(a, b):
    return lax.dot_general(a, b, (((1,), (1,)), ((), ())), preferred_element_type=F32)


def _split2(x):
    hi = x.astype(BF16)
    lo = (x - hi.astype(F32)).astype(BF16)
    return hi, lo


def _split3(x):
    hi = x.astype(BF16)
    r = x - hi.astype(F32)
    mid = r.astype(BF16)
    lo = (r - mid.astype(F32)).astype(BF16)
    return hi, mid, lo


def _dot_x3(x, m):
    hi, mid, lo = _split3(x)
    return _dot(hi, m) + _dot(mid, m) + _dot(lo, m)


def _dot_3x(m, x):
    hi, mid, lo = _split3(x)
    return _dot(m, hi) + _dot(m, mid) + _dot(m, lo)


def _dot_hp(x, w):
    xh, xm, xl = _split3(x)
    wh, wl = _split2(w)
    return _dot(xh, wh) + (_dot(xh, wl) + _dot(xm, wh)) + (_dot(xm, wl) + _dot(xl, wh))


def _sigmoid(x):
    return 1.0 / (1.0 + jnp.exp(-x))


def _softplus(x):
    return jnp.maximum(x, 0.0) + jnp.log1p(jnp.exp(-jnp.abs(x)))


def _silu(x):
    return x * _sigmoid(x)


def _rmsnorm(x, g):
    return x * lax.rsqrt(jnp.mean(x * x, axis=-1, keepdims=True) + NORM_EPS) * g


def _dense_kernel(x_ref, w_ref, b_ref, o_ref, *, act):
    x = x_ref[...]
    if act:
        x = _silu(x)
    o_ref[...] = _dot_hp(x, w_ref[...]) + b_ref[...]


def _dense(x, w, b, act, tn=512):
    m, k = x.shape
    n = w.shape[1]
    assert n % tn == 0
    return pl.pallas_call(
        functools.partial(_dense_kernel, act=act),
        grid=(n // tn,),
        in_specs=[pl.BlockSpec((m, k), lambda j: (0, 0)),
                  pl.BlockSpec((k, tn), lambda j: (0, j)),
                  pl.BlockSpec((1, tn), lambda j: (0, j))],
        out_specs=pl.BlockSpec((m, tn), lambda j: (0, j)),
        out_shape=jax.ShapeDtypeStruct((m, n), F32),
        compiler_params=_cparams(("parallel",)),
        name="dense",
    )(x, w, b.reshape(1, n))


def _in_proj_kernel(x_ref, sh_ref, sc_ref, g_ref, w_ref, b_ref, o_ref, h_ref, h_scr, *, epilogue, h_rows):
    @pl.when(pl.program_id(2) == 0)
    def _():
        h = _rmsnorm(x_ref[0], g_ref[...]) * (1.0 + sc_ref[0]) + sh_ref[0]
        h_scr[...] = h.astype(BF16)
        h_ref[0] = h[h.shape[0] - h_rows:, :]

    acc = _dot(h_scr[...], w_ref[...])
    if epilogue == "sigmoid":
        acc = _sigmoid(acc)
    elif epilogue == "log_sigmoid":
        acc = -_softplus(-(acc + b_ref[...]))
    o_ref[0] = acc.astype(o_ref.dtype)


def _in_proj(x, sh, sc, g, w, bias, *, tm, tn, epilogue, out_dtype, h_rows):
    b, t, d = x.shape
    n = w.shape[1]
    assert t % tm == 0 and n % tn == 0
    per_tok = sh.shape[1] != 1
    mrows = tm if per_tok else 1
    mod_map = (lambda bi, i, j: (bi, i, 0)) if per_tok else (lambda bi, i, j: (bi, 0, 0))
    if bias is None:
        bias = jnp.zeros((1, n), F32)
    out, h = pl.pallas_call(
        functools.partial(_in_proj_kernel, epilogue=epilogue, h_rows=h_rows),
        grid=(b, t // tm, n // tn),
        in_specs=[pl.BlockSpec((1, tm, d), lambda bi, i, j: (bi, i, 0)),
                  pl.BlockSpec((1, mrows, d), mod_map),
                  pl.BlockSpec((1, mrows, d), mod_map),
                  pl.BlockSpec((1, d), lambda bi, i, j: (0, 0)),
                  pl.BlockSpec((d, tn), lambda bi, i, j: (0, j)),
                  pl.BlockSpec((1, tn), lambda bi, i, j: (0, j))],
        out_specs=[pl.BlockSpec((1, tm, tn), lambda bi, i, j: (bi, i, j)),
                   pl.BlockSpec((1, h_rows, d), lambda bi, i, j: (bi, 0, 0))],
        out_shape=[jax.ShapeDtypeStruct((b, t, n), out_dtype),
                   jax.ShapeDtypeStruct((b, h_rows, d), F32)],
        scratch_shapes=[pltpu.VMEM((tm, d), BF16)],
        compiler_params=_cparams(("parallel", "arbitrary", "arbitrary")),
        name="in_proj_" + epilogue,
    )(x, sh, sc, g.reshape(1, d), w, bias)
    return out, h


def _prep_kernel(pr_ref, prev_ref, mu_ref, w0_ref, ww2_ref, a0_ref, wa2_ref, wg2_ref, kk_ref, ka_ref, rk_ref,
                 seg_ref, r_o, w_o, k_o, v_o, kk_o, kka_o, g_o, bonus_o, carry):
    @pl.when(pl.program_id(1) == 0)
    def _():
        carry[...] = prev_ref[0]

    pr = pr_ref[0]
    tt = pr.shape[0]
    first = lax.broadcasted_iota(jnp.int32, pr.shape, 0) == 0
    pprev = jnp.where(first, carry[...], pltpu.roll(pr, 1, axis=0))
    carry[...] = pr[tt - 1:tt, :]
    pm = pr + (pprev - pr) * mu_ref[...]
    r = pm[:, 0:WIDTH]
    k = pm[:, WIDTH:2 * WIDTH]
    v = pm[:, 2 * WIDTH:3 * WIDTH]
    lo = pm[:, 3 * WIDTH:3 * WIDTH + LANES]
    glo = pm[:, 3 * WIDTH + LANES:3 * WIDTH + 2 * LANES]
    seg = seg_ref[...]
    w_log = -_softplus(-(w0_ref[...] + _dot(jnp.tanh(lo).astype(BF16), ww2_ref[...]))) - 0.5
    decay = jnp.exp(-jnp.exp(w_log))
    a = _sigmoid(a0_ref[...] + _dot(lo.astype(BF16), wa2_ref[...]))
    g = _dot(_sigmoid(glo).astype(BF16), wg2_ref[...])
    kk = k * kk_ref[...]
    kk = kk / jnp.maximum(jnp.sqrt(_dot_x3(kk * kk, seg)), 1e-12)
    k2 = k * (1.0 + (a - 1.0) * ka_ref[...])
    r_o[0] = r
    w_o[0] = decay
    k_o[0] = k2
    v_o[0] = v
    kk_o[0] = kk
    kka_o[0] = kk * a
    g_o[0] = g
    bonus_o[0] = _dot_x3(r * k2 * rk_ref[...], seg) * v


def _seg_ones(width, seg):
    i = jnp.arange(width) // seg
    return (i[:, None] == i[None, :]).astype(BF16)


def _rwkv_prep(pr, prev, p, tt):
    b, t, _ = pr.shape
    zpad = jnp.zeros((LANES - DECAY_LORA, WIDTH), F32)
    ww2 = jnp.concatenate([p["w_w2"], zpad], axis=0).astype(BF16)
    wa2 = jnp.concatenate([zpad, p["w_a2"]], axis=0).astype(BF16)
    row = lambda a: a.reshape(1, -1)
    consts = [row(p["mu_shift"]), row(p["w0"]), ww2, row(p["a0"]), wa2, p["w_g2"].astype(BF16),
              row(p["k_k"]), row(p["k_a"]), row(p["r_k"]), _seg_ones(WIDTH, HEAD_DIM)]
    const_specs = [pl.BlockSpec(c.shape, lambda bi, i: (0, 0)) for c in consts]
    o_spec = pl.BlockSpec((1, tt, WIDTH), lambda bi, i: (bi, i, 0))
    return pl.pallas_call(
        _prep_kernel,
        grid=(b, t // tt),
        in_specs=[pl.BlockSpec((1, tt, R_COLS), lambda bi, i: (bi, i, 0)),
                  pl.BlockSpec((1, 1, R_COLS), lambda bi, i: (bi, 0, 0))] + const_specs,
        out_specs=[o_spec] * 8,
        out_shape=[jax.ShapeDtypeStruct((b, t, WIDTH), F32)] * 8,
        scratch_shapes=[pltpu.VMEM((1, R_COLS), F32)],
        compiler_params=_cparams(("parallel", "arbitrary")),
        name="rwkv_prep",
    )(pr, prev.reshape(b, 1, R_COLS), *consts)


def _scan_kernel(kk_ref, w_ref, k_ref, kka_ref, r_ref, v_ref, s0_ref, y_ref, s_ref, *, steps, n_slabs):
    @pl.when(pl.program_id(0) == 0)
    def _():
        s_ref[...] = s0_ref[...]

    def step(t, carry):
        kk = kk_ref[t]
        w = w_ref[t]
        k = k_ref[t]
        kka = kka_ref[t]
        r = r_ref[t]
        rw = r * w
        c1 = jnp.sum(r * kka, axis=0, keepdims=True)
        c2 = jnp.sum(r * k, axis=0, keepdims=True)
        for n in range(n_slabs):
            s = s_ref[n]
            sa = -jnp.sum(s * kk, axis=0, keepdims=True)
            yp = jnp.sum(s * rw, axis=0, keepdims=True)
            vn = v_ref[t, pl.ds(n, 1), :]
            s_ref[n] = s * w + sa * kka + vn * k
            y_ref[t, pl.ds(n, 1), :] = yp + c1 * sa + c2 * vn
        return carry

    lax.fori_loop(0, steps, step, 0)


def _wkv_scan(r, w, k, v, kk, kka, s0, tc):
    b, t, _ = r.shape
    bh = b * N_HEADS
    dup = max(1, LANES // bh)
    lanes = dup * bh
    ni = HEAD_DIM // dup

    def key_layout(x):
        x = x.reshape(b, t, N_HEADS, HEAD_DIM).transpose(1, 3, 0, 2).reshape(t, HEAD_DIM, bh)
        return jnp.concatenate([x] * dup, axis=-1)

    v_l = v.reshape(b, t, N_HEADS, ni, dup).transpose(1, 3, 4, 0, 2).reshape(t, ni, lanes)
    s_l = s0.reshape(b, N_HEADS, ni, dup, HEAD_DIM).transpose(2, 4, 3, 0, 1).reshape(ni, HEAD_DIM, lanes)
    key_spec = pl.BlockSpec((tc, HEAD_DIM, lanes), lambda c: (c, 0, 0))
    val_spec = pl.BlockSpec((tc, ni, lanes), lambda c: (c, 0, 0))
    st_spec = pl.BlockSpec((ni, HEAD_DIM, lanes), lambda c: (0, 0, 0))
    y_l, s_out = pl.pallas_call(
        functools.partial(_scan_kernel, steps=tc, n_slabs=ni),
        grid=(t // tc,),
        in_specs=[key_spec] * 5 + [val_spec, st_spec],
        out_specs=[val_spec, st_spec],
        out_shape=[jax.ShapeDtypeStruct((t, ni, lanes), F32),
                   jax.ShapeDtypeStruct((ni, HEAD_DIM, lanes), F32)],
        compiler_params=_cparams(("arbitrary",)),
        name="wkv_scan",
    )(key_layout(kk), key_layout(w), key_layout(k), key_layout(kka), key_layout(r), v_l, s_l)
    y = y_l.reshape(t, ni, dup, b, N_HEADS).transpose(3, 0, 4, 1, 2).reshape(b, t, WIDTH)
    s_new = s_out.reshape(ni, HEAD_DIM, dup, b, N_HEADS).transpose(3, 4, 0, 2, 1).reshape(
        b, N_HEADS, HEAD_DIM, HEAD_DIM)
    return y, s_new


def _cumsum_kernel(x_ref, o_ref, carry):
    @pl.when(pl.program_id(1) == 0)
    def _():
        carry[...] = jnp.zeros_like(carry)

    x = x_ref[0]
    tb = x.shape[0]
    tril = (lax.broadcasted_iota(jnp.int32, (tb, tb), 1) <= lax.broadcasted_iota(jnp.int32, (tb, tb), 0)).astype(BF16)
    c = _dot_3x(tril, x) + carry[...]
    o_ref[0] = c
    carry[...] = c[tb - 1:tb, :]


def _cumsum_time(x, tb):
    b, t, n = x.shape
    spec = pl.BlockSpec((1, tb, n), lambda bi, i: (bi, i, 0))
    return pl.pallas_call(
        _cumsum_kernel, grid=(b, t // tb), in_specs=[spec], out_specs=spec,
        out_shape=jax.ShapeDtypeStruct((b, t, n), F32),
        scratch_shapes=[pltpu.VMEM((1, n), F32)],
        compiler_params=_cparams(("parallel", "arbitrary")),
        name="cumsum_time",
    )(x)


def _bias_lanes(x, col, lane, base, own, key_side):
    hi, mid, lo = _split3(col)
    first, second = (base + 3, base) if key_side else (base, base + 3)
    out = jnp.where(own, x, 0.0)
    out = jnp.where((lane >= second) & (lane < second + 3), 1.0, out)
    out = jnp.where(lane == first, hi.astype(F32), out)
    out = jnp.where(lane == first + 1, mid.astype(F32), out)
    return jnp.where(lane == first + 2, lo.astype(F32), out)


def _fox_prompt_kernel(q_ref, k_ref, v_ref, cq_ref, ck_ref, o_ref, kaug, vb, *, tq):
    i = pl.program_id(2)
    t = k_ref.shape[1]

    @pl.when(i == 0)
    def _():
        lane_k = lax.broadcasted_iota(jnp.int32, (t, LANES), 1)
        vb[...] = v_ref[0].astype(BF16)
        k = k_ref[0]
        for hh in range(2):
            own = (lane_k >= hh * HEAD_DIM) & (lane_k < (hh + 1) * HEAD_DIM)
            kaug[hh] = _bias_lanes(k, -ck_ref[0, hh], lane_k, (1 - hh) * HEAD_DIM, own, True).astype(BF16)

    lane = lax.broadcasted_iota(jnp.int32, (tq, LANES), 1)
    q = q_ref[0] * (HEAD_DIM ** -0.5)
    qa = []
    for hh in range(2):
        own = (lane >= hh * HEAD_DIM) & (lane < (hh + 1) * HEAD_DIM)
        qa.append(_bias_lanes(q, cq_ref[0, hh], lane, (1 - hh) * HEAD_DIM, own, False).astype(BF16))
    causal = lax.broadcasted_iota(jnp.int32, (tq, tq), 1) <= lax.broadcasted_iota(jnp.int32, (tq, tq), 0)

    def block(j, carry, masked):
        start = pl.multiple_of(j * tq, tq)
        v_blk = vb[pl.ds(start, tq), :]
        out = []
        for hh in range(2):
            m, l, acc = carry[hh]
            s = _dot_nt(qa[hh], kaug[hh, pl.ds(start, tq), :])
            if masked:
                s = jnp.where(causal, s, -jnp.inf)
            m_new = jnp.maximum(m, jnp.max(s, axis=-1, keepdims=True))
            p = jnp.exp(s - m_new)
            alpha = jnp.exp(m - m_new)
            out.append((m_new, alpha * l + jnp.sum(p, axis=-1, keepdims=True),
                        alpha * acc + _dot(p.astype(BF16), v_blk)))
        return tuple(out)

    one = (jnp.full((tq, 1), -jnp.inf, F32), jnp.zeros((tq, 1), F32), jnp.zeros((tq, LANES), F32))
    carry = lax.fori_loop(0, i, lambda j, c: block(j, c, False), (one, one))
    (_, l0, a0), (_, l1, a1) = block(i, carry, True)
    o_ref[0] = jnp.where(lane < HEAD_DIM, a0 / l0, a1 / l1).astype(o_ref.dtype)


def _fox_prompt(qkv, cum, tq):
    b, t, _ = qkv.shape
    npair = WIDTH // LANES
    cum_col = cum.transpose(0, 2, 1).reshape(b, N_HEADS, t, 1)
    return pl.pallas_call(
        functools.partial(_fox_prompt_kernel, tq=tq),
        grid=(b, npair, t // tq),
        in_specs=[pl.BlockSpec((1, tq, LANES), lambda bi, p, i: (bi, i, p)),
                  pl.BlockSpec((1, t, LANES), lambda bi, p, i: (bi, 0, npair + p)),
                  pl.BlockSpec((1, t, LANES), lambda bi, p, i: (bi, 0, 2 * npair + p)),
                  pl.BlockSpec((1, 2, tq, 1), lambda bi, p, i: (bi, p, i, 0)),
                  pl.BlockSpec((1, 2, t, 1), lambda bi, p, i: (bi, p, 0, 0))],
        out_specs=pl.BlockSpec((1, tq, LANES), lambda bi, p, i: (bi, i, p)),
        out_shape=jax.ShapeDtypeStruct((b, t, WIDTH), BF16),
        scratch_shapes=[pltpu.VMEM((2, t, LANES), BF16), pltpu.VMEM((t, LANES), BF16)],
        compiler_params=_cparams(("parallel", "parallel", "arbitrary")),
        name="fox_prompt",
    )(qkv, qkv, qkv, cum_col, cum_col)


def _page_sums_kernel(lf_ref, rev_ref, tot_ref):
    n = lf_ref.shape[1]
    a = lax.broadcasted_iota(jnp.int32, (n, n), 0)
    c = lax.broadcasted_iota(jnp.int32, (n, n), 1)
    same = (a % N_HEADS) == (c % N_HEADS)
    hi, mid, lo = _split3(lf_ref[...])
    later = (same & (a > c)).astype(BF16)
    rev_ref[...] = _dot(hi, later) + _dot(mid, later) + _dot(lo, later)
    every = same.astype(BF16)
    tot_ref[...] = _dot(hi, every) + _dot(mid, every) + _dot(lo, every)


def _page_sums(lf_flat, rows):
    n_pool, n = lf_flat.shape
    spec = pl.BlockSpec((rows, n), lambda i: (i, 0))
    return pl.pallas_call(
        _page_sums_kernel, grid=(n_pool // rows,), in_specs=[spec], out_specs=[spec, spec],
        out_shape=[jax.ShapeDtypeStruct((n_pool, n), F32)] * 2,
        compiler_params=_cparams(("parallel",)),
        name="page_sums",
    )(lf_flat)


def _fox_sample_kernel(pt_ref, q_ref, kn_ref, vn_ref, lfn_ref, *rest, pages_per_step):
    npp = pages_per_step
    k_refs = rest[:npp]
    v_refs = rest[npp:2 * npp]
    rev_refs = rest[2 * npp:3 * npp]
    tot_refs = rest[3 * npp:4 * npp]
    o_ref = rest[4 * npp]
    m_s, l_s, acc_s, suf_s, cn_s = rest[4 * npp + 1:]
    step = pl.program_id(1)
    rows = q_ref.shape[1]
    nq = rows // N_HEADS
    qb = (q_ref[0] * (HEAD_DIM ** -0.5)).astype(BF16)

    def head_match(n):
        r = lax.broadcasted_iota(jnp.int32, (rows, n), 0)
        c = lax.broadcasted_iota(jnp.int32, (rows, n), 1)
        return r, c, (r // nq) == (c % N_HEADS)

    def update(s, vb):
        m_old = m_s[...]
        m_new = jnp.maximum(m_old, jnp.max(s, axis=-1, keepdims=True))
        p = jnp.exp(s - m_new)
        alpha = jnp.exp(m_old - m_new)
        l_s[...] = alpha * l_s[...] + jnp.sum(p, axis=-1, keepdims=True)
        acc_s[...] = alpha * acc_s[...] + _dot(p.astype(BF16), vb)
        m_s[...] = m_new

    @pl.when(step == 0)
    def _():
        n = kn_ref.shape[1]
        r, c, match = head_match(n)
        a2 = lax.broadcasted_iota(jnp.int32, (n, n), 0)
        c2 = lax.broadcasted_iota(jnp.int32, (n, n), 1)
        upto = (((a2 % N_HEADS) == (c2 % N_HEADS)) & (a2 <= c2)).astype(BF16)
        lfn = jnp.broadcast_to(lfn_ref[0], (8, n))
        cn_lane = _dot_x3(lfn, upto)[0:1, :]
        pick = c == (r % nq) * N_HEADS + r // nq
        cn_col = jnp.sum(jnp.where(pick, cn_lane, 0.0), axis=-1, keepdims=True)
        cn_s[...] = cn_col
        suf_s[...] = jnp.zeros_like(suf_s)
        m_s[...] = jnp.full_like(m_s, -jnp.inf)
        l_s[...] = jnp.zeros_like(l_s)
        acc_s[...] = jnp.zeros_like(acc_s)
        s = _dot_nt(qb, kn_ref[0].astype(BF16)) + cn_col - cn_lane
        update(jnp.where(match & (c // N_HEADS <= r % nq), s, -jnp.inf), vn_ref[0].astype(BF16))

    n = rev_refs[0].shape[2]
    _, _, match = head_match(n)
    for u in range(npp):
        kh = k_refs[u][0].reshape(n, HEAD_DIM).astype(BF16)
        vh = v_refs[u][0].reshape(n, HEAD_DIM).astype(BF16)
        s = _dot_nt(qb, kh) + cn_s[...] + (rev_refs[u][0] + suf_s[...])
        update(jnp.where(match, s, -jnp.inf), vh)
        suf_s[...] = suf_s[...] + tot_refs[u][0]

    @pl.when(step == pl.num_programs(1) - 1)
    def _():
        o_ref[0] = (acc_s[...] / l_s[...]).astype(o_ref.dtype)


def _fox_sample(q, k_new, v_new, logf_new, cache_k, cache_v, cache_logf, page_table, pages_per_step):
    b, tn, _ = q.shape
    n_pool, page = cache_k.shape[:2]
    n_pages = page_table.shape[1]
    npp = pages_per_step
    assert n_pages % npp == 0
    rows = N_HEADS * tn
    lanes = page * N_HEADS
    rev, tot = _page_sums(cache_logf.reshape(n_pool, lanes), 256 if n_pool % 256 == 0 else n_pool)
    rev = rev.reshape(n_pool, 1, lanes)
    tot = tot.reshape(n_pool, 1, lanes)
    q_hq = q.reshape(b, tn, N_HEADS, HEAD_DIM).transpose(0, 2, 1, 3).reshape(b, rows, HEAD_DIM)
    kn = k_new.reshape(b, rows, HEAD_DIM)
    vn = v_new.reshape(b, rows, HEAD_DIM)
    lfn = logf_new.reshape(b, 1, rows)

    def page_map(u, nd):
        return lambda bi, s, pt: (pt[bi, n_pages - 1 - (s * npp + u)],) + (0,) * nd

    tok = pl.BlockSpec((1, rows, HEAD_DIM), lambda bi, s, pt: (bi, 0, 0))
    in_specs = ([tok, tok, tok, pl.BlockSpec((1, 1, rows), lambda bi, s, pt: (bi, 0, 0))]
                + [pl.BlockSpec((1, page, N_HEADS, HEAD_DIM), page_map(u, 3)) for u in range(npp)] * 2
                + [pl.BlockSpec((1, 1, lanes), page_map(u, 2)) for u in range(npp)] * 2)
    grid_spec = pltpu.PrefetchScalarGridSpec(
        num_scalar_prefetch=1,
        grid=(b, n_pages // npp),
        in_specs=in_specs,
        out_specs=tok,
        scratch_shapes=[pltpu.VMEM((rows, 1), F32), pltpu.VMEM((rows, 1), F32), pltpu.VMEM((rows, HEAD_DIM), F32),
                        pltpu.VMEM((1, lanes), F32), pltpu.VMEM((rows, 1), F32)])
    o = pl.pallas_call(
        functools.partial(_fox_sample_kernel, pages_per_step=npp),
        grid_spec=grid_spec,
        out_shape=jax.ShapeDtypeStruct((b, rows, HEAD_DIM), BF16),
        compiler_params=_cparams(("parallel", "arbitrary")),
        name="fox_sample",
    )(page_table, q_hq, kn, vn, lfn, *([cache_k] * npp), *([cache_v] * npp), *([rev] * npp), *([tot] * npp))
    return o.reshape(b, N_HEADS, tn, HEAD_DIM).transpose(0, 2, 1, 3).reshape(b, tn, WIDTH)


def _merge_kernel(y_ref, bonus_ref, g_ref, yf_ref, gates_ref, x_ref, g1_ref, sh2_ref, sc2_ref,
                  lnw_ref, lnb_ref, seg_ref, wr_ref, wf_ref, wo_ref, n2_ref, wrt_ref,
                  x1_ref, h2_ref, lg_ref):
    seg = seg_ref[...]
    y = y_ref[0]
    mu = _dot_x3(y, seg) * (1.0 / HEAD_DIM)
    d = y - mu
    var = _dot_x3(d * d, seg) * (1.0 / HEAD_DIM)
    yn = d * lax.rsqrt(var + LNX_EPS) * lnw_ref[...] + lnb_ref[...]
    yr = ((yn + bonus_ref[0]) * g_ref[0]).astype(BF16)
    gates = gates_ref[0].astype(F32)
    d_model = x_ref.shape[2]
    merged = gates[:, :d_model] * _dot(yr, wr_ref[...]) + gates[:, d_model:] * _dot(yf_ref[0], wf_ref[...])
    x1 = x_ref[0] + g1_ref[0] * _dot(merged.astype(BF16), wo_ref[...])
    x1_ref[0] = x1
    h2 = _rmsnorm(x1, n2_ref[...]) * (1.0 + sc2_ref[0]) + sh2_ref[0]
    h2_ref[0] = h2
    hh, hl = _split2(h2)
    wrt = wrt_ref[...]
    lg_ref[0] = _dot(hh, wrt[0]) + (_dot(hh, wrt[1]) + _dot(hl, wrt[0]))


def _merge(y, bonus, g, yf, gates, x, g1, sh2, sc2, p, w_router_t, tm):
    b, t, d = x.shape
    per_tok = g1.shape[1] != 1
    mrows = tm if per_tok else 1
    mod_map = (lambda bi, i: (bi, i, 0)) if per_tok else (lambda bi, i: (bi, 0, 0))
    tok = lambda n: pl.BlockSpec((1, tm, n), lambda bi, i: (bi, i, 0))
    mod = pl.BlockSpec((1, mrows, d), mod_map)
    row = lambda a: a.reshape(1, -1)
    consts = [row(p["lnx_w"]), row(p["lnx_b"]), _seg_ones(WIDTH, HEAD_DIM), p["w_br_r"].astype(BF16),
              p["w_br_f"].astype(BF16), p["w_out"].astype(BF16), row(p["norm2_g"]), w_router_t]
    const_specs = [pl.BlockSpec(c.shape, (lambda bi, i: (0, 0)) if c.ndim == 2 else (lambda bi, i: (0, 0, 0)))
                   for c in consts]
    return pl.pallas_call(
        _merge_kernel,
        grid=(b, t // tm),
        in_specs=[tok(WIDTH), tok(WIDTH), tok(WIDTH), tok(WIDTH), tok(2 * d), tok(d), mod, mod, mod] + const_specs,
        out_specs=[tok(d), tok(d), tok(LANES)],
        out_shape=[jax.ShapeDtypeStruct((b, t, d), F32), jax.ShapeDtypeStruct((b, t, d), F32),
                   jax.ShapeDtypeStruct((b, t, LANES), F32)],
        compiler_params=_cparams(("parallel", "parallel")),
        name="merge",
    )(y, bonus, g, yf, gates, x, g1, sh2, sc2, *consts)


def _route_kernel(lg_ref, bias_ref, eidx_ref, rank_ref, wt_ref, cnt_ref, carry):
    @pl.when(pl.program_id(0) == 0)
    def _():
        carry[...] = jnp.zeros_like(carry)

    lt = lg_ref[...].T
    tm = lt.shape[1]
    score = _sigmoid(lt[:N_EXPERTS])
    biased = score + bias_ref[...][:N_EXPERTS]
    slab = [biased[k * N_GROUPS:(k + 1) * N_GROUPS] for k in range(GROUP_SIZE)]
    neg = jnp.full((N_GROUPS, tm), -jnp.inf, F32)
    m1 = functools.reduce(jnp.maximum, slab)
    taken = jnp.zeros((N_GROUPS, tm), jnp.bool_)
    m2 = neg
    for k in range(GROUP_SIZE):
        is_first = (slab[k] == m1) & jnp.logical_not(taken)
        taken = taken | is_first
        m2 = jnp.maximum(m2, jnp.where(is_first, neg, slab[k]))
    gs = m1 + m2
    g_iota = lax.broadcasted_iota(jnp.int32, (N_GROUPS, tm), 0)
    cnt = jnp.zeros((N_GROUPS, tm), jnp.int32)
    for g2 in range(N_GROUPS):
        other = gs[g2:g2 + 1, :]
        beats = (other > gs) | ((g_iota > g2) & (other == gs))
        cnt = cnt + beats.astype(jnp.int32)
    g_sel = cnt < TOPK_GROUPS
    cand = [jnp.where(g_sel, slab[k], neg) for k in range(GROUP_SIZE)]
    rank = [jnp.zeros((N_GROUPS, tm), jnp.int32) for _ in range(GROUP_SIZE)]
    for k2 in range(GROUP_SIZE):
        for g2 in range(N_GROUPS):
            other = cand[k2][g2:g2 + 1, :]
            for k in range(GROUP_SIZE):
                first = (g_iota >= g2) if k2 < k else (g_iota > g2)
                beats = (other > cand[k]) | (first & (other == cand[k]))
                rank[k] = rank[k] + beats.astype(jnp.int32)
    sel = [rank[k] < TOP_K for k in range(GROUP_SIZE)]
    sc = [score[k * N_GROUPS:(k + 1) * N_GROUPS] for k in range(GROUP_SIZE)]
    picked = [jnp.where(sel[k], sc[k], 0.0) for k in range(GROUP_SIZE)]
    total = jnp.sum(functools.reduce(jnp.add, picked), axis=0, keepdims=True)
    gate = jnp.concatenate([pk / total * ROUTED_SCALE for pk in picked], axis=0)
    chosen = jnp.concatenate([s.astype(F32) for s in sel], axis=0)
    chosen_b = chosen.astype(BF16)
    ri = lax.broadcasted_iota(jnp.int32, (N_EXPERTS, N_EXPERTS), 0)
    ci = lax.broadcasted_iota(jnp.int32, (N_EXPERTS, N_EXPERTS), 1)
    ordinal = _dot((ci < ri).astype(BF16), chosen_b)
    ta = lax.broadcasted_iota(jnp.int32, (tm, tm), 0)
    tc = lax.broadcasted_iota(jnp.int32, (tm, tm), 1)
    rank_tok = _dot(chosen_b, (ta < tc).astype(BF16)) + carry[...]
    carry[...] = carry[...] + jnp.sum(chosen, axis=1, keepdims=True)
    cnt_ref[...] = jnp.broadcast_to(carry[...], cnt_ref.shape)
    row = lax.broadcasted_iota(jnp.int32, (N_EXPERTS, tm), 0)
    expert_id = ((row % N_GROUPS) * GROUP_SIZE + row // N_GROUPS).astype(F32)
    e_rows, r_rows, w_rows = [], [], []
    for n in range(TOP_K):
        hit = (chosen > 0.0) & (ordinal == float(n))
        e_rows.append(jnp.sum(jnp.where(hit, expert_id, 0.0), axis=0, keepdims=True))
        r_rows.append(jnp.sum(jnp.where(hit, rank_tok, 0.0), axis=0, keepdims=True))
        w_rows.append(jnp.sum(jnp.where(hit, gate, 0.0), axis=0, keepdims=True))
    zrow = jnp.zeros((1, tm), F32)
    pad8 = lambda rows: jnp.concatenate(rows + [zrow] * (8 - TOP_K), axis=0)
    eidx_ref[...] = pad8(e_rows).astype(jnp.int32)
    rank_ref[...] = pad8(r_rows).astype(jnp.int32)
    wt_ref[...] = jnp.concatenate([pad8(w_rows), jnp.zeros((LANES - 8, tm), F32)], axis=0).T


def _route(logits, e_bias_perm, tm):
    m = logits.shape[0]
    bias = jnp.pad(e_bias_perm, (0, LANES - N_EXPERTS)).reshape(LANES, 1)
    lane_spec = pl.BlockSpec((8, tm), lambda i: (0, i))
    return pl.pallas_call(
        _route_kernel,
        grid=(m // tm,),
        in_specs=[pl.BlockSpec((tm, LANES), lambda i: (i, 0)), pl.BlockSpec((LANES, 1), lambda i: (0, 0))],
        out_specs=[lane_spec, lane_spec, pl.BlockSpec((tm, LANES), lambda i: (i, 0)),
                   pl.BlockSpec((N_EXPERTS, LANES), lambda i: (0, 0))],
        out_shape=[jax.ShapeDtypeStruct((8, m), jnp.int32), jax.ShapeDtypeStruct((8, m), jnp.int32),
                   jax.ShapeDtypeStruct((m, LANES), F32), jax.ShapeDtypeStruct((N_EXPERTS, LANES), F32)],
        scratch_shapes=[pltpu.VMEM((N_EXPERTS, 1), F32)],
        compiler_params=_cparams(("arbitrary",)),
        name="route",
    )(logits, bias)


def _scatter_kernel(off_ref, eidx_ref, rank_ref, h_ref, xs_in_ref, xs_ref, sem, *, tile):
    del xs_in_ref

    def row_copy(t, slot):
        return pltpu.make_async_copy(h_ref.at[pl.ds(t, 1), :], xs_ref.at[pl.ds(slot, 1), :], sem)

    def issue(t, c):
        for n in range(TOP_K):
            row_copy(t, off_ref[eidx_ref[n, t]] + rank_ref[n, t]).start()
        return c

    def drain(t, c):
        for n in range(TOP_K):
            row_copy(0, 0).wait()
        return c

    lax.fori_loop(0, tile, issue, 0)
    lax.fori_loop(0, tile, drain, 0)


def _scatter_rows(h2, eidx, rank, off, n_rows, tile):
    m, d = h2.shape
    smem = pl.BlockSpec((8, tile), lambda i, off: (0, i), memory_space=pltpu.SMEM)
    grid_spec = pltpu.PrefetchScalarGridSpec(
        num_scalar_prefetch=1,
        grid=(m // tile,),
        in_specs=[smem, smem, pl.BlockSpec((tile, d), lambda i, off: (i, 0)), pl.BlockSpec(memory_space=pl.ANY)],
        out_specs=pl.BlockSpec(memory_space=pl.ANY),
        scratch_shapes=[pltpu.SemaphoreType.DMA(())])
    return pl.pallas_call(
        functools.partial(_scatter_kernel, tile=tile),
        grid_spec=grid_spec,
        out_shape=jax.ShapeDtypeStruct((n_rows, d), F32),
        input_output_aliases={4: 0},
        compiler_params=_cparams(("arbitrary",)),
        name="scatter_rows",
    )(off, eidx, rank, h2, jnp.zeros((n_rows, d), F32))


def _expert_kernel(te_ref, nu_ref, x_ref, wg_ref, wu_ref, wd_ref, o_ref, wgb, wub, wdb):
    i = pl.program_id(0)
    changed = (i == 0) | (te_ref[i] != te_ref[jnp.maximum(i - 1, 0)])

    @pl.when(changed)
    def _():
        wgb[...] = wg_ref[0].astype(BF16)
        wub[...] = wu_ref[0].astype(BF16)
        wdb[...] = wd_ref[0].astype(BF16)

    @pl.when(i < nu_ref[0])
    def _():
        x = x_ref[...].astype(BF16)
        act = _silu(_dot(x, wgb[...])) * _dot(x, wub[...])
        o_ref[...] = _dot(act.astype(BF16), wdb[...])

    @pl.when(i >= nu_ref[0])
    def _():
        o_ref[...] = jnp.zeros_like(o_ref)


def _expert_tiles(xs, tile_expert, n_used, wg, wu, wd, te):
    n_rows, d = xs.shape
    ff = wg.shape[2]
    last = lambda i, nu: jnp.minimum(i, nu[0] - 1)
    grid_spec = pltpu.PrefetchScalarGridSpec(
        num_scalar_prefetch=2,
        grid=(n_rows // te,),
        in_specs=[pl.BlockSpec((te, d), lambda i, tx, nu: (last(i, nu), 0)),
                  pl.BlockSpec((1, d, ff), lambda i, tx, nu: (tx[i], 0, 0)),
                  pl.BlockSpec((1, d, ff), lambda i, tx, nu: (tx[i], 0, 0)),
                  pl.BlockSpec((1, ff, d), lambda i, tx, nu: (tx[i], 0, 0))],
        out_specs=pl.BlockSpec((te, d), lambda i, tx, nu: (i, 0)),
        scratch_shapes=[pltpu.VMEM((d, ff), BF16), pltpu.VMEM((d, ff), BF16), pltpu.VMEM((ff, d), BF16)])
    return pl.pallas_call(
        _expert_kernel,
        grid_spec=grid_spec,
        out_shape=jax.ShapeDtypeStruct((n_rows, d), F32),
        compiler_params=_cparams(("arbitrary",)),
        name="expert_tiles",
    )(tile_expert, n_used, xs, wg, wu, wd)


def _combine_kernel(off_ref, eidx_ref, rank_ref, wt_ref, h_ref, x1_ref, g2_ref, nf_ref, wsg_ref, wsu_ref, wsd_ref,
                    os_ref, y_ref, buf, sem, *, tile):
    def row_copy(n, t, slot):
        return pltpu.make_async_copy(os_ref.at[pl.ds(slot, 1), :], buf.at[n, pl.ds(t, 1), :], sem)

    def issue(t, c):
        for n in range(TOP_K):
            row_copy(n, t, off_ref[eidx_ref[n, t]] + rank_ref[n, t]).start()
        return c

    def drain(t, c):
        for n in range(TOP_K):
            row_copy(0, 0, 0).wait()
        return c

    lax.fori_loop(0, tile, issue, 0)
    h = h_ref[0].astype(BF16)
    shared = _dot((_silu(_dot(h, wsg_ref[...])) * _dot(h, wsu_ref[...])).astype(BF16), wsd_ref[...])
    lax.fori_loop(0, tile, drain, 0)
    w = wt_ref[0]
    routed = w[:, 0:1] * buf[0]
    for n in range(1, TOP_K):
        routed = routed + w[:, n:n + 1] * buf[n]
    x2 = x1_ref[0] + g2_ref[0] * (routed + shared)
    y_ref[0] = _rmsnorm(x2, nf_ref[...])


def _combine(os, eidx, rank, off, wt, h2, x1, g2, normf_g, wsg, wsu, wsd, tile):
    b, t, d = x1.shape
    nt = t // tile
    per_tok = g2.shape[1] != 1
    mrows = tile if per_tok else 1
    mod_map = (lambda bi, i, off: (bi, i, 0)) if per_tok else (lambda bi, i, off: (bi, 0, 0))
    tok = lambda n: pl.BlockSpec((1, tile, n), lambda bi, i, off: (bi, i, 0))
    smem = pl.BlockSpec((8, tile), lambda bi, i, off: (0, bi * nt + i), memory_space=pltpu.SMEM)
    const = lambda a: pl.BlockSpec(a.shape, lambda bi, i, off: (0, 0))
    grid_spec = pltpu.PrefetchScalarGridSpec(
        num_scalar_prefetch=1,
        grid=(b, nt),
        in_specs=[smem, smem, tok(LANES), tok(d), tok(d), pl.BlockSpec((1, mrows, d), mod_map),
                  pl.BlockSpec((1, d), lambda bi, i, off: (0, 0)), const(wsg), const(wsu), const(wsd),
                  pl.BlockSpec(memory_space=pl.ANY)],
        out_specs=tok(d),
        scratch_shapes=[pltpu.VMEM((TOP_K, tile, d), F32), pltpu.SemaphoreType.DMA(())])
    return pl.pallas_call(
        functools.partial(_combine_kernel, tile=tile),
        grid_spec=grid_spec,
        out_shape=jax.ShapeDtypeStruct((b, t, d), F32),
        compiler_params=_cparams(("arbitrary", "arbitrary")),
        name="combine",
    )(off, eidx, rank, wt.reshape(b, t, LANES), h2, x1, g2, normf_g.reshape(1, d), wsg, wsu, wsd, os)


def _moe(h2, logits, x1, g2, p, w, cfg):
    bx, tx, d = x1.shape
    m = bx * tx
    te = cfg["te"]
    eidx, rank, wt, cnt = _route(logits.reshape(m, LANES), w["e_bias_perm"], cfg["tm_route"])
    counts = cnt[:, 0].astype(jnp.int32).reshape(GROUP_SIZE, N_GROUPS).T.reshape(N_EXPERTS)
    tiles = (counts + te - 1) // te
    ends = jnp.cumsum(tiles)
    off = ((ends - tiles) * te).astype(jnp.int32)
    n_tiles = (m * TOP_K) // te + N_EXPERTS
    n_used = ends[-1:].astype(jnp.int32)
    tile_ids = jnp.minimum(jnp.arange(n_tiles, dtype=jnp.int32), n_used[0] - 1)
    tile_expert = jnp.minimum(jnp.searchsorted(ends, tile_ids, side="right"), N_EXPERTS - 1).astype(jnp.int32)
    xs = _scatter_rows(h2.reshape(m, d), eidx, rank, off, n_tiles * te, cfg["tile_rows"])
    os = _expert_tiles(xs, tile_expert, n_used, p["w_exp_gate"], p["w_exp_up"], p["w_exp_down"], te)
    return _combine(os, eidx, rank, off, wt, h2, x1, g2, p["normf_g"], w["sh_gate"], w["sh_up"], w["sh_down"],
                    cfg["tile_rows"])


def _layer(x, mod, shift_prev, wkv0, attend, p, w, cfg):
    b, t, d = x.shape
    bx, tx = cfg["rows"]
    tm = cfg["tm"]
    xr = x.reshape(bx, tx, d)
    if bx == b:
        part = lambda i: mod[:, i:i + 1, :]
    else:
        part = lambda i: jnp.repeat(mod[:, i, :], t, axis=0).reshape(bx, tx, d)
    sh1, sc1, g1, sh2, sc2, g2 = (part(i) for i in range(6))
    h_rows = 1 if bx == b else tx
    proj = functools.partial(_in_proj, xr, sh1, sc1, p["norm1_g"], tm=tm, h_rows=h_rows)
    pr, h_keep = proj(w["in_r"], None, tn=cfg["tn_r"], epilogue="none", out_dtype=F32)
    qkv, _ = proj(w["in_qkv"], None, tn=512, epilogue="none", out_dtype=F32)
    logf_pad, _ = proj(w["in_f"], w["b_f_pad"], tn=LANES, epilogue="log_sigmoid", out_dtype=F32)
    gates, _ = proj(w["in_g"], None, tn=512, epilogue="sigmoid", out_dtype=BF16)
    shift_new = h_keep[:, 0, :] if bx == b else h_keep.reshape(b, t, d)[:, -1, :]
    logf = logf_pad.reshape(b, t, LANES)[:, :, :N_HEADS]
    qkv = qkv.reshape(b, t, 3 * WIDTH)

    prev = _dense(shift_prev, w["in_r_f32"], jnp.zeros((R_COLS,), F32), act=False, tn=R_COLS // 2)
    r, dec, k2, v, kk, kka, g, bonus = _rwkv_prep(pr.reshape(b, t, R_COLS), prev, p, cfg["tt"])
    y_scan, wkv_new = _wkv_scan(r, dec, k2, v, kk, kka, wkv0, cfg["tc"])

    y_f = attend(qkv, logf)

    rs = lambda a: a.reshape(bx, tx, a.shape[-1])
    x1, h2, logits = _merge(rs(y_scan), rs(bonus), rs(g), rs(y_f), gates, xr, g1, sh2, sc2, p, w["router_t"], tm)
    y = _moe(h2, logits, x1, g2, p, w, cfg)
    k_out = qkv[:, :, WIDTH:2 * WIDTH].reshape(b, t, N_HEADS, HEAD_DIM)
    v_out = qkv[:, :, 2 * WIDTH:].reshape(b, t, N_HEADS, HEAD_DIM)
    return y.reshape(b, t, d), k_out, v_out, logf, wkv_new, shift_new


def kernel(x_prompt, x_sample, c_prompt, c_sample, cache_k, cache_v, cache_logf, page_table, state_wkv, state_shift, w_ada, b_ada, norm1_g, w_in, mu_shift, w0, w_w2, a0, w_a2, w_g2, k_k, k_a, r_k, lnx_w, lnx_b, b_f, w_br_r, w_br_f, w_out, norm2_g, w_router, e_bias, w_exp_gate, w_exp_up, w_exp_down, w_sh_gate, w_sh_up, w_sh_down, normf_g):
    p = dict(norm1_g=norm1_g, mu_shift=mu_shift, w0=w0, w_w2=w_w2, a0=a0, w_a2=w_a2, w_g2=w_g2, k_k=k_k, k_a=k_a,
             r_k=r_k, lnx_w=lnx_w, lnx_b=lnx_b, w_br_r=w_br_r, w_br_f=w_br_f, w_out=w_out, norm2_g=norm2_g,
             normf_g=normf_g, w_exp_gate=w_exp_gate, w_exp_up=w_exp_up, w_exp_down=w_exp_down)
    bp, tp, d = x_prompt.shape
    bs, ts, _ = x_sample.shape
    off_f = R_COLS + 3 * WIDTH
    off_g = off_f + N_HEADS
    perm = lambda a: a.reshape(a.shape[:-1] + (N_GROUPS, GROUP_SIZE)).swapaxes(-1, -2).reshape(a.shape)
    router = jnp.pad(perm(w_router), ((0, 0), (0, LANES - N_EXPERTS)))
    r_hi = router.astype(BF16)
    w = dict(
        in_r=w_in[:, :R_COLS].astype(BF16), in_r_f32=w_in[:, :R_COLS],
        in_qkv=w_in[:, R_COLS:off_f].astype(BF16),
        in_f=jnp.pad(w_in[:, off_f:off_g], ((0, 0), (0, LANES - N_HEADS))).astype(BF16),
        in_g=w_in[:, off_g:].astype(BF16),
        b_f_pad=jnp.pad(b_f, (0, LANES - N_HEADS)).reshape(1, LANES),
        router_t=jnp.stack([r_hi, (router - r_hi.astype(F32)).astype(BF16)]),
        e_bias_perm=perm(e_bias),
        sh_gate=w_sh_gate.astype(BF16), sh_up=w_sh_up.astype(BF16), sh_down=w_sh_down.astype(BF16),
    )
    mod = _dense(jnp.concatenate([c_prompt, c_sample], axis=0), w_ada, b_ada, act=True).reshape(bp + bs, 6, d)

    def attend_prompt(qkv, logf):
        lf = jnp.pad(logf, ((0, 0), (0, 0), (0, LANES - N_HEADS)))
        cum = _cumsum_time(lf, 256)[:, :, :N_HEADS]
        return _fox_prompt(qkv, cum, min(512, tp))

    def attend_sample(qkv, logf):
        return _fox_sample(qkv[:, :, :WIDTH], qkv[:, :, WIDTH:2 * WIDTH], qkv[:, :, 2 * WIDTH:], logf,
                           cache_k, cache_v, cache_logf, page_table, 4)

    cfg_p = dict(rows=(bp, tp), tm=min(512, tp), tn_r=R_COLS // 2, tt=min(256, tp), tc=16,
                 tm_route=min(1024, tp), te=256, tile_rows=256)
    cfg_s = dict(rows=(1, bs * ts), tm=bs * ts, tn_r=R_COLS // 2, tt=ts, tc=ts, tm_route=bs * ts, te=256,
                 tile_rows=bs * ts)
    yp, kp, vp, lfp, wkvp, shp = _layer(x_prompt, mod[:bp], jnp.zeros((bp, d), F32),
                                        jnp.zeros((bp, N_HEADS, HEAD_DIM, HEAD_DIM), F32), attend_prompt, p, w, cfg_p)
    ys, ks, vs, lfs, wkvs, shs = _layer(x_sample, mod[bp:], state_shift, state_wkv, attend_sample, p, w, cfg_s)
    return (yp, ys, kp, vp, lfp, wkvp, shp, ks, vs, lfs, wkvs, shs)
```

```python
import functools

import jax
import jax.numpy as jnp
from jax import lax
from jax.experimental import pallas as pl
from jax.experimental.pallas import tpu as pltpu

F32 = jnp.float32
BF16 = jnp.bfloat16

HEAD_DIM = 64
N_HEADS = 8
WIDTH = N_HEADS * HEAD_DIM
DECAY_LORA = 64
AAA_LORA = 64
GATE_LORA = 128
R_COLS = 3 * WIDTH + DECAY_LORA + AAA_LORA + GATE_LORA
LNX_EPS = 64e-5
NORM_EPS = 1e-6
N_EXPERTS = 64
N_GROUPS = 8
GROUP_SIZE = N_EXPERTS // N_GROUPS
TOPK_GROUPS = 4
TOP_K = 6
ROUTED_SCALE = 2.5
LANES = 128
VMEM_LIMIT = 56 * 1024 * 1024


def _cparams(sem):
    return pltpu.CompilerParams(dimension_semantics=sem, vmem_limit_bytes=VMEM_LIMIT)


def _dot(a, b):
    return jnp.dot(a, b, preferred_element_type=F32)


def _dot_nt(a, b):
    return lax.dot_general(a, b, (((1,), (1,)), ((), ())), preferred_element_type=F32)


def _split2(x):
    hi = x.astype(BF16)
    lo = (x - hi.astype(F32)).astype(BF16)
    return hi, lo


def _split3(x):
    hi = x.astype(BF16)
    r = x - hi.astype(F32)
    mid = r.astype(BF16)
    lo = (r - mid.astype(F32)).astype(BF16)
    return hi, mid, lo


def _dot_x3(x, m):
    hi, mid, lo = _split3(x)
    return _dot(hi, m) + _dot(mid, m) + _dot(lo, m)


def _dot_3x(m, x):
    hi, mid, lo = _split3(x)
    return _dot(m, hi) + _dot(m, mid) + _dot(m, lo)


def _dot_hp(x, w):
    xh, xm, xl = _split3(x)
    wh, wl = _split2(w)
    return _dot(xh, wh) + (_dot(xh, wl) + _dot(xm, wh)) + (_dot(xm, wl) + _dot(xl, wh))


def _sigmoid(x):
    return 1.0 / (1.0 + jnp.exp(-x))


def _softplus(x):
    return jnp.maximum(x, 0.0) + jnp.log1p(jnp.exp(-jnp.abs(x)))


def _silu(x):
    return x * _sigmoid(x)


def _rmsnorm(x, g):
    return x * lax.rsqrt(jnp.mean(x * x, axis=-1, keepdims=True) + NORM_EPS) * g


def _dense_kernel(x_ref, w_ref, b_ref, o_ref, *, act):
    x = x_ref[...]
    if act:
        x = _silu(x)
    o_ref[...] = _dot_hp(x, w_ref[...]) + b_ref[...]


def _dense(x, w, b, act, tn=512):
    m, k = x.shape
    n = w.shape[1]
    assert n % tn == 0
    return pl.pallas_call(
        functools.partial(_dense_kernel, act=act),
        grid=(n // tn,),
        in_specs=[pl.BlockSpec((m, k), lambda j: (0, 0)),
                  pl.BlockSpec((k, tn), lambda j: (0, j)),
                  pl.BlockSpec((1, tn), lambda j: (0, j))],
        out_specs=pl.BlockSpec((m, tn), lambda j: (0, j)),
        out_shape=jax.ShapeDtypeStruct((m, n), F32),
        compiler_params=_cparams(("parallel",)),
        name="dense",
    )(x, w, b.reshape(1, n))


def _in_proj_kernel(x_ref, sh_ref, sc_ref, g_ref, w_ref, b_ref, o_ref, h_ref, h_scr, *, epilogue, h_rows):
    @pl.when(pl.program_id(2) == 0)
    def _():
        h = _rmsnorm(x_ref[0], g_ref[...]) * (1.0 + sc_ref[0]) + sh_ref[0]
        h_scr[...] = h.astype(BF16)
        h_ref[0] = h[h.shape[0] - h_rows:, :]

    acc = _dot(h_scr[...], w_ref[...])
    if epilogue == "sigmoid":
        acc = _sigmoid(acc)
    elif epilogue == "log_sigmoid":
        acc = -_softplus(-(acc + b_ref[...]))
    o_ref[0] = acc.astype(o_ref.dtype)


def _in_proj(x, sh, sc, g, w, bias, *, tm, tn, epilogue, out_dtype, h_rows):
    b, t, d = x.shape
    n = w.shape[1]
    assert t % tm == 0 and n % tn == 0
    per_tok = sh.shape[1] != 1
    mrows = tm if per_tok else 1
    mod_map = (lambda bi, i, j: (bi, i, 0)) if per_tok else (lambda bi, i, j: (bi, 0, 0))
    if bias is None:
        bias = jnp.zeros((1, n), F32)
    out, h = pl.pallas_call(
        functools.partial(_in_proj_kernel, epilogue=epilogue, h_rows=h_rows),
        grid=(b, t // tm, n // tn),
        in_specs=[pl.BlockSpec((1, tm, d), lambda bi, i, j: (bi, i, 0)),
                  pl.BlockSpec((1, mrows, d), mod_map),
                  pl.BlockSpec((1, mrows, d), mod_map),
                  pl.BlockSpec((1, d), lambda bi, i, j: (0, 0)),
                  pl.BlockSpec((d, tn), lambda bi, i, j: (0, j)),
                  pl.BlockSpec((1, tn), lambda bi, i, j: (0, j))],
        out_specs=[pl.BlockSpec((1, tm, tn), lambda bi, i, j: (bi, i, j)),
                   pl.BlockSpec((1, h_rows, d), lambda bi, i, j: (bi, 0, 0))],
        out_shape=[jax.ShapeDtypeStruct((b, t, n), out_dtype),
                   jax.ShapeDtypeStruct((b, h_rows, d), F32)],
        scratch_shapes=[pltpu.VMEM((tm, d), BF16)],
        compiler_params=_cparams(("parallel", "arbitrary", "arbitrary")),
        name="in_proj_" + epilogue,
    )(x, sh, sc, g.reshape(1, d), w, bias)
    return out, h


def _prep_kernel(pr_ref, prev_ref, mu_ref, w0_ref, ww2_ref, a0_ref, wa2_ref, wg2_ref, kk_ref, ka_ref, rk_ref,
                 seg_ref, r_o, w_o, k_o, v_o, kk_o, kka_o, g_o, bonus_o, carry):
    @pl.when(pl.program_id(1) == 0)
    def _():
        carry[...] = prev_ref[0]

    pr = pr_ref[0]
    tt = pr.shape[0]
    first = lax.broadcasted_iota(jnp.int32, pr.shape, 0) == 0
    pprev = jnp.where(first, carry[...], pltpu.roll(pr, 1, axis=0))
    carry[...] = pr[tt - 1:tt, :]
    pm = pr + (pprev - pr) * mu_ref[...]
    r = pm[:, 0:WIDTH]
    k = pm[:, WIDTH:2 * WIDTH]
    v = pm[:, 2 * WIDTH:3 * WIDTH]
    lo = pm[:, 3 * WIDTH:3 * WIDTH + LANES]
    glo = pm[:, 3 * WIDTH + LANES:3 * WIDTH + 2 * LANES]
    seg = seg_ref[...]
    w_log = -_softplus(-(w0_ref[...] + _dot(jnp.tanh(lo).astype(BF16), ww2_ref[...]))) - 0.5
    decay = jnp.exp(-jnp.exp(w_log))
    a = _sigmoid(a0_ref[...] + _dot(lo.astype(BF16), wa2_ref[...]))
    g = _dot(_sigmoid(glo).astype(BF16), wg2_ref[...])
    kk = k * kk_ref[...]
    kk = kk / jnp.maximum(jnp.sqrt(_dot_x3(kk * kk, seg)), 1e-12)
    k2 = k * (1.0 + (a - 1.0) * ka_ref[...])
    r_o[0] = r
    w_o[0] = decay
    k_o[0] = k2
    v_o[0] = v
    kk_o[0] = kk
    kka_o[0] = kk * a
    g_o[0] = g
    bonus_o[0] = _dot_x3(r * k2 * rk_ref[...], seg) * v


def _seg_ones(width, seg):
    i = jnp.arange(width) // seg
    return (i[:, None] == i[None, :]).astype(BF16)


def _rwkv_prep(pr, prev, p, tt):
    b, t, _ = pr.shape
    zpad = jnp.zeros((LANES - DECAY_LORA, WIDTH), F32)
    ww2 = jnp.concatenate([p["w_w2"], zpad], axis=0).astype(BF16)
    wa2 = jnp.concatenate([zpad, p["w_a2"]], axis=0).astype(BF16)
    row = lambda a: a.reshape(1, -1)
    consts = [row(p["mu_shift"]), row(p["w0"]), ww2, row(p["a0"]), wa2, p["w_g2"].astype(BF16),
              row(p["k_k"]), row(p["k_a"]), row(p["r_k"]), _seg_ones(WIDTH, HEAD_DIM)]
    const_specs = [pl.BlockSpec(c.shape, lambda bi, i: (0, 0)) for c in consts]
    o_spec = pl.BlockSpec((1, tt, WIDTH), lambda bi, i: (bi, i, 0))
    return pl.pallas_call(
        _prep_kernel,
        grid=(b, t // tt),
        in_specs=[pl.BlockSpec((1, tt, R_COLS), lambda bi, i: (bi, i, 0)),
                  pl.BlockSpec((1, 1, R_COLS), lambda bi, i: (bi, 0, 0))] + const_specs,
        out_specs=[o_spec] * 8,
        out_shape=[jax.ShapeDtypeStruct((b, t, WIDTH), F32)] * 8,
        scratch_shapes=[pltpu.VMEM((1, R_COLS), F32)],
        compiler_params=_cparams(("parallel", "arbitrary")),
        name="rwkv_prep",
    )(pr, prev.reshape(b, 1, R_COLS), *consts)


def _scan_kernel(kk_ref, w_ref, k_ref, kka_ref, r_ref, v_ref, s0_ref, y_ref, s_ref, *, steps, n_slabs):
    @pl.when(pl.program_id(0) == 0)
    def _():
        s_ref[...] = s0_ref[...]

    def step(t, carry):
        kk = kk_ref[t]
        w = w_ref[t]
        k = k_ref[t]
        kka = kka_ref[t]
        r = r_ref[t]
        rw = r * w
        c1 = jnp.sum(r * kka, axis=0, keepdims=True)
        c2 = jnp.sum(r * k, axis=0, keepdims=True)
        for n in range(n_slabs):
            s = s_ref[n]
            sa = -jnp.sum(s * kk, axis=0, keepdims=True)
            yp = jnp.sum(s * rw, axis=0, keepdims=True)
            vn = v_ref[t, pl.ds(n, 1), :]
            s_ref[n] = s * w + sa * kka + vn * k
            y_ref[t, pl.ds(n, 1), :] = yp + c1 * sa + c2 * vn
        return carry

    lax.fori_loop(0, steps, step, 0)


def _wkv_scan(r, w, k, v, kk, kka, s0, tc):
    b, t, _ = r.shape
    bh = b * N_HEADS
    dup = max(1, LANES // bh)
    lanes = dup * bh
    ni = HEAD_DIM // dup

    def key_layout(x):
        x = x.reshape(b, t, N_HEADS, HEAD_DIM).transpose(1, 3, 0, 2).reshape(t, HEAD_DIM, 1, bh)
        return jnp.broadcast_to(x, (t, HEAD_DIM, dup, bh)).reshape(t, HEAD_DIM, lanes)

    v_l = v.reshape(b, t, N_HEADS, ni, dup).transpose(1, 3, 4, 0, 2).reshape(t, ni, lanes)
    s_l = s0.reshape(b, N_HEADS, ni, dup, HEAD_DIM).transpose(2, 4, 3, 0, 1).reshape(ni, HEAD_DIM, lanes)
    key_spec = pl.BlockSpec((tc, HEAD_DIM, lanes), lambda c: (c, 0, 0))
    val_spec = pl.BlockSpec((tc, ni, lanes), lambda c: (c, 0, 0))
    st_spec = pl.BlockSpec((ni, HEAD_DIM, lanes), lambda c: (0, 0, 0))
    y_l, s_out = pl.pallas_call(
        functools.partial(_scan_kernel, steps=tc, n_slabs=ni),
        grid=(t // tc,),
        in_specs=[key_spec] * 5 + [val_spec, st_spec],
        out_specs=[val_spec, st_spec],
        out_shape=[jax.ShapeDtypeStruct((t, ni, lanes), F32),
                   jax.ShapeDtypeStruct((ni, HEAD_DIM, lanes), F32)],
        compiler_params=_cparams(("arbitrary",)),
        name="wkv_scan",
    )(key_layout(kk), key_layout(w), key_layout(k), key_layout(kka), key_layout(r), v_l, s_l)
    y = y_l.reshape(t, ni, dup, b, N_HEADS).transpose(3, 0, 4, 1, 2).reshape(b, t, WIDTH)
    s_new = s_out.reshape(ni, HEAD_DIM, dup, b, N_HEADS).transpose(3, 4, 0, 2, 1).reshape(
        b, N_HEADS, HEAD_DIM, HEAD_DIM)
    return y, s_new


def _cumsum_kernel(x_ref, o_ref, carry):
    @pl.when(pl.program_id(1) == 0)
    def _():
        carry[...] = jnp.zeros_like(carry)

    x = x_ref[0]
    tb = x.shape[0]
    tril = (lax.broadcasted_iota(jnp.int32, (tb, tb), 1) <= lax.broadcasted_iota(jnp.int32, (tb, tb), 0)).astype(BF16)
    c = _dot_3x(tril, x) + carry[...]
    o_ref[0] = c
    carry[...] = c[tb - 1:tb, :]


def _cumsum_time(x, tb):
    b, t, n = x.shape
    spec = pl.BlockSpec((1, tb, n), lambda bi, i: (bi, i, 0))
    return pl.pallas_call(
        _cumsum_kernel, grid=(b, t // tb), in_specs=[spec], out_specs=spec,
        out_shape=jax.ShapeDtypeStruct((b, t, n), F32),
        scratch_shapes=[pltpu.VMEM((1, n), F32)],
        compiler_params=_cparams(("parallel", "arbitrary")),
        name="cumsum_time",
    )(x)


def _bias_lanes(x, col, lane, base, own, key_side):
    hi, mid, lo = _split3(col)
    first, second = (base + 3, base) if key_side else (base, base + 3)
    out = jnp.where(own, x, 0.0)
    out = jnp.where((lane >= second) & (lane < second + 3), 1.0, out)
    out = jnp.where(lane == first, hi.astype(F32), out)
    out = jnp.where(lane == first + 1, mid.astype(F32), out)
    return jnp.where(lane == first + 2, lo.astype(F32), out)


def _fox_prompt_kernel(q_ref, k_ref, v_ref, cq_ref, ck_ref, o_ref, kaug, vb, *, tq):
    i = pl.program_id(2)
    t = k_ref.shape[1]

    @pl.when(i == 0)
    def _():
        lane_k = lax.broadcasted_iota(jnp.int32, (t, LANES), 1)
        vb[...] = v_ref[0].astype(BF16)
        k = k_ref[0]
        for hh in range(2):
            own = (lane_k >= hh * HEAD_DIM) & (lane_k < (hh + 1) * HEAD_DIM)
            kaug[hh] = _bias_lanes(k, -ck_ref[0, hh], lane_k, (1 - hh) * HEAD_DIM, own, True).astype(BF16)

    lane = lax.broadcasted_iota(jnp.int32, (tq, LANES), 1)
    q = q_ref[0] * (HEAD_DIM ** -0.5)
    qa = []
    for hh in range(2):
        own = (lane >= hh * HEAD_DIM) & (lane < (hh + 1) * HEAD_DIM)
        qa.append(_bias_lanes(q, cq_ref[0, hh], lane, (1 - hh) * HEAD_DIM, own, False).astype(BF16))
    causal = lax.broadcasted_iota(jnp.int32, (tq, tq), 1) <= lax.broadcasted_iota(jnp.int32, (tq, tq), 0)

    def block(j, carry, masked):
        start = pl.multiple_of(j * tq, tq)
        v_blk = vb[pl.ds(start, tq), :]
        out = []
        for hh in range(2):
            m, l, acc = carry[hh]
            s = _dot_nt(qa[hh], kaug[hh, pl.ds(start, tq), :])
            if masked:
                s = jnp.where(causal, s, -jnp.inf)
            m_new = jnp.maximum(m, jnp.max(s, axis=-1, keepdims=True))
            p = jnp.exp(s - m_new)
            alpha = jnp.exp(m - m_new)
            out.append((m_new, alpha * l + jnp.sum(p, axis=-1, keepdims=True),
                        alpha * acc + _dot(p.astype(BF16), v_blk)))
        return tuple(out)

    one = (jnp.full((tq, 1), -jnp.inf, F32), jnp.zeros((tq, 1), F32), jnp.zeros((tq, LANES), F32))
    carry = lax.fori_loop(0, i, lambda j, c: block(j, c, False), (one, one))
    (_, l0, a0), (_, l1, a1) = block(i, carry, True)
    o_ref[0] = jnp.where(lane < HEAD_DIM, a0 / l0, a1 / l1).astype(o_ref.dtype)


def _fox_prompt(qkv, cum, tq):
    b, t, _ = qkv.shape
    npair = WIDTH // LANES
    cum_col = cum.transpose(0, 2, 1).reshape(b, N_HEADS, t, 1)
    return pl.pallas_call(
        functools.partial(_fox_prompt_kernel, tq=tq),
        grid=(b, npair, t // tq),
        in_specs=[pl.BlockSpec((1, tq, LANES), lambda bi, p, i: (bi, i, p)),
                  pl.BlockSpec((1, t, LANES), lambda bi, p, i: (bi, 0, npair + p)),
                  pl.BlockSpec((1, t, LANES), lambda bi, p, i: (bi, 0, 2 * npair + p)),
                  pl.BlockSpec((1, 2, tq, 1), lambda bi, p, i: (bi, p, i, 0)),
                  pl.BlockSpec((1, 2, t, 1), lambda bi, p, i: (bi, p, 0, 0))],
        out_specs=pl.BlockSpec((1, tq, LANES), lambda bi, p, i: (bi, i, p)),
        out_shape=jax.ShapeDtypeStruct((b, t, WIDTH), BF16),
        scratch_shapes=[pltpu.VMEM((2, t, LANES), BF16), pltpu.VMEM((t, LANES), BF16)],
        compiler_params=_cparams(("parallel", "parallel", "arbitrary")),
        name="fox_prompt",
    )(qkv, qkv, qkv, cum_col, cum_col)


def _page_sums_kernel(lf_ref, rev_ref, tot_ref):
    n = lf_ref.shape[1]
    a = lax.broadcasted_iota(jnp.int32, (n, n), 0)
    c = lax.broadcasted_iota(jnp.int32, (n, n), 1)
    hi, mid, lo = _split3(lf_ref[...])
    later = (a > c).astype(BF16)
    rev_ref[...] = _dot(hi, later) + _dot(mid, later) + _dot(lo, later)
    every = jnp.ones((n, n), BF16)
    tot_ref[...] = _dot(hi, every) + _dot(mid, every) + _dot(lo, every)


def _page_sums(lf_rows, rows):
    n_rows, n = lf_rows.shape
    spec = pl.BlockSpec((rows, n), lambda i: (i, 0))
    return pl.pallas_call(
        _page_sums_kernel, grid=(n_rows // rows,), in_specs=[spec], out_specs=[spec, spec],
        out_shape=[jax.ShapeDtypeStruct((n_rows, n), F32)] * 2,
        compiler_params=_cparams(("parallel",)),
        name="page_sums",
    )(lf_rows)


def _fox_sample_kernel(pt_ref, q_ref, kn_ref, vn_ref, lfn_ref, *rest, pages_per_step):
    npp = pages_per_step
    k_refs = rest[:npp]
    v_refs = rest[npp:2 * npp]
    rev_refs = rest[2 * npp:3 * npp]
    tot_refs = rest[3 * npp:4 * npp]
    o_ref = rest[4 * npp]
    qrep, m_s, l_s, acc_s, suf_s, cn_s = rest[4 * npp + 1:]
    step = pl.program_id(1)
    nq = q_ref.shape[1]
    rows = N_HEADS * nq
    page = kn_ref.shape[1]
    row_head = lax.broadcasted_iota(jnp.int32, (rows, WIDTH), 0) // nq
    lane_head = lax.broadcasted_iota(jnp.int32, (rows, WIDTH), 1) // HEAD_DIM

    def rep_heads(x):
        return jnp.concatenate([jnp.broadcast_to(x[h:h + 1, :], (nq, x.shape[1])) for h in range(N_HEADS)], axis=0)

    def update(s_list, pv):
        m_old = m_s[...]
        m_new = functools.reduce(jnp.maximum, [jnp.max(s, axis=-1, keepdims=True) for s in s_list] + [m_old])
        p_list = [jnp.exp(s - m_new) for s in s_list]
        alpha = jnp.exp(m_old - m_new)
        l_s[...] = alpha * l_s[...] + functools.reduce(jnp.add, [jnp.sum(p, axis=-1, keepdims=True) for p in p_list])
        acc_s[...] = alpha * acc_s[...] + functools.reduce(jnp.add, [pv(u, p.astype(BF16)) for u, p in enumerate(p_list)])
        m_s[...] = m_new

    @pl.when(step == 0)
    def _():
        q = q_ref[0] * (HEAD_DIM ** -0.5)
        qrep[...] = jnp.where(row_head == lane_head, jnp.concatenate([q] * N_HEADS, axis=0), 0.0).astype(BF16)
        key_i = lax.broadcasted_iota(jnp.int32, (page, page), 0)
        key_j = lax.broadcasted_iota(jnp.int32, (page, page), 1)
        cn_row = _dot_x3(rep_heads(lfn_ref[0]), (key_i <= key_j).astype(BF16))
        rq = lax.broadcasted_iota(jnp.int32, (rows, page), 0) % nq
        kc = lax.broadcasted_iota(jnp.int32, (rows, page), 1)
        cn_col = jnp.sum(jnp.where(kc == rq, cn_row, 0.0), axis=-1, keepdims=True)
        cn_s[...] = cn_col
        suf_s[...] = jnp.zeros_like(suf_s)
        m_s[...] = jnp.full_like(m_s, -jnp.inf)
        l_s[...] = jnp.zeros_like(l_s)
        acc_s[...] = jnp.zeros_like(acc_s)
        s = _dot_nt(qrep[...], kn_ref[0].astype(BF16)) + cn_col - cn_row
        vn = vn_ref[0].astype(BF16)
        update([jnp.where(kc <= rq, s, -jnp.inf)], lambda u, p: _dot(p, vn))

    q_all = qrep[...]
    cn = cn_s[...]
    suf = suf_s[...]
    s_list = []
    for u in range(npp):
        kt = k_refs[u][0].reshape(WIDTH, page).astype(BF16)
        s_list.append(_dot(q_all, kt) + ((cn + suf) + rep_heads(rev_refs[u][0])))
        suf = suf + rep_heads(tot_refs[u][0])
    update(s_list, lambda u, p: _dot_nt(p, v_refs[u][0].reshape(WIDTH, page).astype(BF16)))
    suf_s[...] = suf

    @pl.when(step == pl.num_programs(1) - 1)
    def _():
        o_sel = jnp.where(row_head == lane_head, acc_s[...] / l_s[...], 0.0)
        out = o_sel[0:nq]
        for h in range(1, N_HEADS):
            out = out + o_sel[h * nq:(h + 1) * nq]
        o_ref[0] = out.astype(o_ref.dtype)


def _fox_sample(q, k_new, v_new, logf_new, cache_k, cache_v, cache_logf, page_table, pages_per_step):
    b, tn, _ = q.shape
    n_pool, page = cache_k.shape[:2]
    n_pages = page_table.shape[1]
    npp = pages_per_step
    assert n_pages % npp == 0 and tn <= page
    rows = N_HEADS * tn
    ck = cache_k.transpose(0, 2, 3, 1)
    cv = cache_v.transpose(0, 2, 3, 1)
    clf = cache_logf.transpose(0, 2, 1).reshape(n_pool * N_HEADS, page)
    sum_rows = 2048 if clf.shape[0] % 2048 == 0 else clf.shape[0]
    rev, tot = (a.reshape(n_pool, N_HEADS, page) for a in _page_sums(clf, sum_rows))
    pad_rows = lambda x: jnp.pad(x, ((0, 0), (0, page - tn), (0, 0)))
    lfn = jnp.pad(logf_new.transpose(0, 2, 1), ((0, 0), (0, 0), (0, page - tn)))

    def page_map(u, nd):
        return lambda bi, s, pt: (pt[bi, n_pages - 1 - (s * npp + u)],) + (0,) * nd

    tok_spec = lambda r: pl.BlockSpec((1, r, WIDTH), lambda bi, s, pt: (bi, 0, 0))
    in_specs = ([tok_spec(tn), tok_spec(page), tok_spec(page),
                 pl.BlockSpec((1, N_HEADS, page), lambda bi, s, pt: (bi, 0, 0))]
                + [pl.BlockSpec((1, N_HEADS, HEAD_DIM, page), page_map(u, 3)) for u in range(npp)] * 2
                + [pl.BlockSpec((1, N_HEADS, page), page_map(u, 2)) for u in range(npp)] * 2)
    grid_spec = pltpu.PrefetchScalarGridSpec(
        num_scalar_prefetch=1,
        grid=(b, n_pages // npp),
        in_specs=in_specs,
        out_specs=tok_spec(tn),
        scratch_shapes=[pltpu.VMEM((rows, WIDTH), BF16), pltpu.VMEM((rows, 1), F32), pltpu.VMEM((rows, 1), F32),
                        pltpu.VMEM((rows, WIDTH), F32), pltpu.VMEM((rows, page), F32), pltpu.VMEM((rows, 1), F32)])
    return pl.pallas_call(
        functools.partial(_fox_sample_kernel, pages_per_step=npp),
        grid_spec=grid_spec,
        out_shape=jax.ShapeDtypeStruct((b, tn, WIDTH), BF16),
        compiler_params=_cparams(("parallel", "arbitrary")),
        name="fox_sample",
    )(page_table, q, pad_rows(k_new), pad_rows(v_new), lfn, *([ck] * npp), *([cv] * npp), *([rev] * npp),
      *([tot] * npp))


def _merge_kernel(y_ref, bonus_ref, g_ref, yf_ref, gates_ref, x_ref, g1_ref, sh2_ref, sc2_ref,
                  lnw_ref, lnb_ref, seg_ref, wr_ref, wf_ref, wo_ref, n2_ref, wrt_ref,
                  x1_ref, h2_ref, lg_ref):
    seg = seg_ref[...]
    y = y_ref[0]
    mu = _dot_x3(y, seg) * (1.0 / HEAD_DIM)
    d = y - mu
    var = _dot_x3(d * d, seg) * (1.0 / HEAD_DIM)
    yn = d * lax.rsqrt(var + LNX_EPS) * lnw_ref[...] + lnb_ref[...]
    yr = ((yn + bonus_ref[0]) * g_ref[0]).astype(BF16)
    gates = gates_ref[0].astype(F32)
    d_model = x_ref.shape[2]
    merged = gates[:, :d_model] * _dot(yr, wr_ref[...]) + gates[:, d_model:] * _dot(yf_ref[0], wf_ref[...])
    x1 = x_ref[0] + g1_ref[0] * _dot(merged.astype(BF16), wo_ref[...])
    x1_ref[0] = x1
    h2 = _rmsnorm(x1, n2_ref[...]) * (1.0 + sc2_ref[0]) + sh2_ref[0]
    h2_ref[0] = h2
    hh, hl = _split2(h2)
    wrt = wrt_ref[...]
    lg_ref[0] = _dot(hh, wrt[0]) + (_dot(hh, wrt[1]) + _dot(hl, wrt[0]))


def _merge(y, bonus, g, yf, gates, x, g1, sh2, sc2, p, w_router_t, tm):
    b, t, d = x.shape
    per_tok = g1.shape[1] != 1
    mrows = tm if per_tok else 1
    mod_map = (lambda bi, i: (bi, i, 0)) if per_tok else (lambda bi, i: (bi, 0, 0))
    tok = lambda n: pl.BlockSpec((1, tm, n), lambda bi, i: (bi, i, 0))
    mod = pl.BlockSpec((1, mrows, d), mod_map)
    row = lambda a: a.reshape(1, -1)
    consts = [row(p["lnx_w"]), row(p["lnx_b"]), _seg_ones(WIDTH, HEAD_DIM), p["w_br_r"].astype(BF16),
              p["w_br_f"].astype(BF16), p["w_out"].astype(BF16), row(p["norm2_g"]), w_router_t]
    const_specs = [pl.BlockSpec(c.shape, (lambda bi, i: (0, 0)) if c.ndim == 2 else (lambda bi, i: (0, 0, 0)))
                   for c in consts]
    return pl.pallas_call(
        _merge_kernel,
        grid=(b, t // tm),
        in_specs=[tok(WIDTH), tok(WIDTH), tok(WIDTH), tok(WIDTH), tok(2 * d), tok(d), mod, mod, mod] + const_specs,
        out_specs=[tok(d), tok(d), tok(LANES)],
        out_shape=[jax.ShapeDtypeStruct((b, t, d), F32), jax.ShapeDtypeStruct((b, t, d), F32),
                   jax.ShapeDtypeStruct((b, t, LANES), F32)],
        compiler_params=_cparams(("parallel", "parallel")),
        name="merge",
    )(y, bonus, g, yf, gates, x, g1, sh2, sc2, *consts)


def _route_kernel(lg_ref, bias_ref, eidx_ref, rank_ref, wt_ref, cnt_ref, carry):
    @pl.when(pl.program_id(0) == 0)
    def _():
        carry[...] = jnp.zeros_like(carry)

    lt = lg_ref[...].T
    tm = lt.shape[1]
    score = _sigmoid(lt[:N_EXPERTS])
    biased = score + bias_ref[...][:N_EXPERTS]
    slab = [biased[k * N_GROUPS:(k + 1) * N_GROUPS] for k in range(GROUP_SIZE)]
    neg = jnp.full((N_GROUPS, tm), -jnp.inf, F32)
    m1 = functools.reduce(jnp.maximum, slab)
    taken = jnp.zeros((N_GROUPS, tm), jnp.bool_)
    m2 = neg
    for k in range(GROUP_SIZE):
        is_first = (slab[k] == m1) & jnp.logical_not(taken)
        taken = taken | is_first
        m2 = jnp.maximum(m2, jnp.where(is_first, neg, slab[k]))
    gs = m1 + m2
    g_iota = lax.broadcasted_iota(jnp.int32, (N_GROUPS, tm), 0)
    cnt = jnp.zeros((N_GROUPS, tm), jnp.int32)
    for g2 in range(N_GROUPS):
        other = gs[g2:g2 + 1, :]
        beats = (other > gs) | ((g_iota > g2) & (other == gs))
        cnt = cnt + beats.astype(jnp.int32)
    g_sel = cnt < TOPK_GROUPS
    cand = [jnp.where(g_sel, slab[k], neg) for k in range(GROUP_SIZE)]
    rank = [jnp.zeros((N_GROUPS, tm), jnp.int32) for _ in range(GROUP_SIZE)]
    for k2 in range(GROUP_SIZE):
        for g2 in range(N_GROUPS):
            other = cand[k2][g2:g2 + 1, :]
            for k in range(GROUP_SIZE):
                first = (g_iota >= g2) if k2 < k else (g_iota > g2)
                beats = (other > cand[k]) | (first & (other == cand[k]))
                rank[k] = rank[k] + beats.astype(jnp.int32)
    sel = [rank[k] < TOP_K for k in range(GROUP_SIZE)]
    sc = [score[k * N_GROUPS:(k + 1) * N_GROUPS] for k in range(GROUP_SIZE)]
    picked = [jnp.where(sel[k], sc[k], 0.0) for k in range(GROUP_SIZE)]
    total = jnp.sum(functools.reduce(jnp.add, picked), axis=0, keepdims=True)
    gate = jnp.concatenate([pk / total * ROUTED_SCALE for pk in picked], axis=0)
    chosen = jnp.concatenate([s.astype(F32) for s in sel], axis=0)
    chosen_b = chosen.astype(BF16)
    ri = lax.broadcasted_iota(jnp.int32, (N_EXPERTS, N_EXPERTS), 0)
    ci = lax.broadcasted_iota(jnp.int32, (N_EXPERTS, N_EXPERTS), 1)
    ordinal = _dot((ci < ri).astype(BF16), chosen_b)
    ta = lax.broadcasted_iota(jnp.int32, (tm, tm), 0)
    tc = lax.broadcasted_iota(jnp.int32, (tm, tm), 1)
    rank_tok = _dot(chosen_b, (ta < tc).astype(BF16)) + carry[...]
    carry[...] = carry[...] + jnp.sum(chosen, axis=1, keepdims=True)
    cnt_ref[...] = jnp.broadcast_to(carry[...], cnt_ref.shape)
    row = lax.broadcasted_iota(jnp.int32, (N_EXPERTS, tm), 0)
    expert_id = ((row % N_GROUPS) * GROUP_SIZE + row // N_GROUPS).astype(F32)
    e_rows, r_rows, w_rows = [], [], []
    for n in range(TOP_K):
        hit = (chosen > 0.0) & (ordinal == float(n))
        e_rows.append(jnp.sum(jnp.where(hit, expert_id, 0.0), axis=0, keepdims=True))
        r_rows.append(jnp.sum(jnp.where(hit, rank_tok, 0.0), axis=0, keepdims=True))
        w_rows.append(jnp.sum(jnp.where(hit, gate, 0.0), axis=0, keepdims=True))
    zrow = jnp.zeros((1, tm), F32)
    pad8 = lambda rows: jnp.concatenate(rows + [zrow] * (8 - TOP_K), axis=0)
    eidx_ref[...] = pad8(e_rows).astype(jnp.int32)
    rank_ref[...] = pad8(r_rows).astype(jnp.int32)
    wt_ref[...] = jnp.concatenate([pad8(w_rows), jnp.zeros((LANES - 8, tm), F32)], axis=0).T


def _route(logits, e_bias_perm, tm):
    m = logits.shape[0]
    bias = jnp.pad(e_bias_perm, (0, LANES - N_EXPERTS)).reshape(LANES, 1)
    lane_spec = pl.BlockSpec((8, tm), lambda i: (0, i))
    return pl.pallas_call(
        _route_kernel,
        grid=(m // tm,),
        in_specs=[pl.BlockSpec((tm, LANES), lambda i: (i, 0)), pl.BlockSpec((LANES, 1), lambda i: (0, 0))],
        out_specs=[lane_spec, lane_spec, pl.BlockSpec((tm, LANES), lambda i: (i, 0)),
                   pl.BlockSpec((N_EXPERTS, LANES), lambda i: (0, 0))],
        out_shape=[jax.ShapeDtypeStruct((8, m), jnp.int32), jax.ShapeDtypeStruct((8, m), jnp.int32),
                   jax.ShapeDtypeStruct((m, LANES), F32), jax.ShapeDtypeStruct((N_EXPERTS, LANES), F32)],
        scratch_shapes=[pltpu.VMEM((N_EXPERTS, 1), F32)],
        compiler_params=_cparams(("arbitrary",)),
        name="route",
    )(logits, bias)


def _scatter_kernel(off_ref, eidx_ref, rank_ref, h_ref, xs_in_ref, xs_ref, sem, *, tile):
    del xs_in_ref

    def row_copy(t, slot):
        return pltpu.make_async_copy(h_ref.at[pl.ds(t, 1), :], xs_ref.at[pl.ds(slot, 1), :], sem)

    def issue(t, c):
        for n in range(TOP_K):
            row_copy(t, off_ref[eidx_ref[n, t]] + rank_ref[n, t]).start(priority=n % 2)
        return c

    def drain(t, c):
        for n in range(TOP_K):
            row_copy(0, 0).wait()
        return c

    lax.fori_loop(0, tile, issue, 0)
    lax.fori_loop(0, tile, drain, 0)


def _scatter_rows(h2, eidx, rank, off, n_rows, tile):
    m, d = h2.shape
    smem = pl.BlockSpec((8, tile), lambda i, off: (0, i), memory_space=pltpu.SMEM)
    grid_spec = pltpu.PrefetchScalarGridSpec(
        num_scalar_prefetch=1,
        grid=(m // tile,),
        in_specs=[smem, smem, pl.BlockSpec((tile, d), lambda i, off: (i, 0)), pl.BlockSpec(memory_space=pl.ANY)],
        out_specs=pl.BlockSpec(memory_space=pl.ANY),
        scratch_shapes=[pltpu.SemaphoreType.DMA(())])
    return pl.pallas_call(
        functools.partial(_scatter_kernel, tile=tile),
        grid_spec=grid_spec,
        out_shape=jax.ShapeDtypeStruct((n_rows, d), F32),
        input_output_aliases={4: 0},
        compiler_params=_cparams(("arbitrary",)),
        name="scatter_rows",
    )(off, eidx, rank, h2, jnp.zeros((n_rows, d), F32))


def _expert_kernel(te_ref, nu_ref, x_ref, wg_ref, wu_ref, wd_ref, o_ref, wgb, wub, wdb):
    i = pl.program_id(0)
    changed = (i == 0) | (te_ref[i] != te_ref[jnp.maximum(i - 1, 0)])

    @pl.when(changed)
    def _():
        wgb[...] = wg_ref[0].astype(BF16)
        wub[...] = wu_ref[0].astype(BF16)
        wdb[...] = wd_ref[0].astype(BF16)

    @pl.when(i < nu_ref[0])
    def _():
        x = x_ref[...].astype(BF16)
        act = _silu(_dot(x, wgb[...])) * _dot(x, wub[...])
        o_ref[...] = _dot(act.astype(BF16), wdb[...])

    @pl.when(i >= nu_ref[0])
    def _():
        o_ref[...] = jnp.zeros_like(o_ref)


def _expert_tiles(xs, tile_expert, n_used, wg, wu, wd, te):
    n_rows, d = xs.shape
    ff = wg.shape[2]
    last = lambda i, nu: jnp.minimum(i, nu[0] - 1)
    grid_spec = pltpu.PrefetchScalarGridSpec(
        num_scalar_prefetch=2,
        grid=(n_rows // te,),
        in_specs=[pl.BlockSpec((te, d), lambda i, tx, nu: (last(i, nu), 0)),
                  pl.BlockSpec((1, d, ff), lambda i, tx, nu: (tx[i], 0, 0)),
                  pl.BlockSpec((1, d, ff), lambda i, tx, nu: (tx[i], 0, 0)),
                  pl.BlockSpec((1, ff, d), lambda i, tx, nu: (tx[i], 0, 0))],
        out_specs=pl.BlockSpec((te, d), lambda i, tx, nu: (i, 0)),
        scratch_shapes=[pltpu.VMEM((d, ff), BF16), pltpu.VMEM((d, ff), BF16), pltpu.VMEM((ff, d), BF16)])
    return pl.pallas_call(
        _expert_kernel,
        grid_spec=grid_spec,
        out_shape=jax.ShapeDtypeStruct((n_rows, d), F32),
        compiler_params=_cparams(("arbitrary",)),
        name="expert_tiles",
    )(tile_expert, n_used, xs, wg, wu, wd)


def _combine_kernel(off_ref, eidx_ref, rank_ref, wt_ref, h_ref, x1_ref, g2_ref, nf_ref, wsg_ref, wsu_ref, wsd_ref,
                    os_ref, y_ref, buf, sem, *, tile):
    def row_copy(n, t, slot):
        return pltpu.make_async_copy(os_ref.at[pl.ds(slot, 1), :], buf.at[n, pl.ds(t, 1), :], sem)

    def issue(t, c):
        for n in range(TOP_K):
            row_copy(n, t, off_ref[eidx_ref[n, t]] + rank_ref[n, t]).start(priority=n % 2)
        return c

    def drain(t, c):
        for n in range(TOP_K):
            row_copy(0, 0, 0).wait()
        return c

    lax.fori_loop(0, tile, issue, 0)
    h = h_ref[0].astype(BF16)
    shared = _dot((_silu(_dot(h, wsg_ref[...])) * _dot(h, wsu_ref[...])).astype(BF16), wsd_ref[...])
    lax.fori_loop(0, tile, drain, 0)
    w = wt_ref[0]
    routed = w[:, 0:1] * buf[0]
    for n in range(1, TOP_K):
        routed = routed + w[:, n:n + 1] * buf[n]
    x2 = x1_ref[0] + g2_ref[0] * (routed + shared)
    y_ref[0] = _rmsnorm(x2, nf_ref[...])


def _combine(os, eidx, rank, off, wt, h2, x1, g2, normf_g, wsg, wsu, wsd, tile):
    b, t, d = x1.shape
    nt = t // tile
    per_tok = g2.shape[1] != 1
    mrows = tile if per_tok else 1
    mod_map = (lambda bi, i, off: (bi, i, 0)) if per_tok else (lambda bi, i, off: (bi, 0, 0))
    tok = lambda n: pl.BlockSpec((1, tile, n), lambda bi, i, off: (bi, i, 0))
    smem = pl.BlockSpec((8, tile), lambda bi, i, off: (0, bi * nt + i), memory_space=pltpu.SMEM)
    const = lambda a: pl.BlockSpec(a.shape, lambda bi, i, off: (0, 0))
    grid_spec = pltpu.PrefetchScalarGridSpec(
        num_scalar_prefetch=1,
        grid=(b, nt),
        in_specs=[smem, smem, tok(LANES), tok(d), tok(d), pl.BlockSpec((1, mrows, d), mod_map),
                  pl.BlockSpec((1, d), lambda bi, i, off: (0, 0)), const(wsg), const(wsu), const(wsd),
                  pl.BlockSpec(memory_space=pl.ANY)],
        out_specs=tok(d),
        scratch_shapes=[pltpu.VMEM((TOP_K, tile, d), F32), pltpu.SemaphoreType.DMA(())])
    return pl.pallas_call(
        functools.partial(_combine_kernel, tile=tile),
        grid_spec=grid_spec,
        out_shape=jax.ShapeDtypeStruct((b, t, d), F32),
        compiler_params=_cparams(("arbitrary", "arbitrary")),
        name="combine",
    )(off, eidx, rank, wt.reshape(b, t, LANES), h2, x1, g2, normf_g.reshape(1, d), wsg, wsu, wsd, os)


def _moe(h2, logits, x1, g2, p, w, cfg):
    bx, tx, d = x1.shape
    m = bx * tx
    te = cfg["te"]
    eidx, rank, wt, cnt = _route(logits.reshape(m, LANES), w["e_bias_perm"], cfg["tm_route"])
    counts = cnt[:, 0].astype(jnp.int32).reshape(GROUP_SIZE, N_GROUPS).T.reshape(N_EXPERTS)
    tiles = (counts + te - 1) // te
    ends = jnp.cumsum(tiles)
    off = ((ends - tiles) * te).astype(jnp.int32)
    n_tiles = (m * TOP_K) // te + N_EXPERTS
    n_used = ends[-1:].astype(jnp.int32)
    tile_ids = jnp.minimum(jnp.arange(n_tiles, dtype=jnp.int32), n_used[0] - 1)
    tile_expert = jnp.minimum(jnp.sum(ends[None, :] <= tile_ids[:, None], axis=1), N_EXPERTS - 1).astype(jnp.int32)
    xs = _scatter_rows(h2.reshape(m, d), eidx, rank, off, n_tiles * te, cfg["tile_rows"])
    os = _expert_tiles(xs, tile_expert, n_used, p["w_exp_gate"], p["w_exp_up"], p["w_exp_down"], te)
    return _combine(os, eidx, rank, off, wt, h2, x1, g2, p["normf_g"], w["sh_gate"], w["sh_up"], w["sh_down"],
                    cfg["tile_rows"])


def _layer(x, mod, shift_prev, wkv0, attend, p, w, cfg):
    b, t, d = x.shape
    bx, tx = cfg["rows"]
    tm = cfg["tm"]
    xr = x.reshape(bx, tx, d)
    if bx == b:
        part = lambda i: mod[:, i:i + 1, :]
    else:
        part = lambda i: jnp.repeat(mod[:, i, :], t, axis=0).reshape(bx, tx, d)
    sh1, sc1, g1, sh2, sc2, g2 = (part(i) for i in range(6))
    h_rows = 1 if bx == b else tx
    proj = functools.partial(_in_proj, xr, sh1, sc1, p["norm1_g"], tm=tm, h_rows=h_rows)
    pr, h_keep = proj(w["in_r"], None, tn=cfg["tn_r"], epilogue="none", out_dtype=F32)
    qkv, _ = proj(w["in_qkv"], None, tn=512, epilogue="none", out_dtype=F32)
    logf_pad, _ = proj(w["in_f"], w["b_f_pad"], tn=LANES, epilogue="log_sigmoid", out_dtype=F32)
    gates, _ = proj(w["in_g"], None, tn=512, epilogue="sigmoid", out_dtype=BF16)
    shift_new = h_keep[:, 0, :] if bx == b else h_keep.reshape(b, t, d)[:, -1, :]
    logf = logf_pad.reshape(b, t, LANES)[:, :, :N_HEADS]
    qkv = qkv.reshape(b, t, 3 * WIDTH)

    prev = _dense(shift_prev, w["in_r_f32"], jnp.zeros((R_COLS,), F32), act=False, tn=R_COLS // 2)
    r, dec, k2, v, kk, kka, g, bonus = _rwkv_prep(pr.reshape(b, t, R_COLS), prev, p, cfg["tt"])
    y_scan, wkv_new = _wkv_scan(r, dec, k2, v, kk, kka, wkv0, cfg["tc"])

    y_f = attend(qkv, logf)

    rs = lambda a: a.reshape(bx, tx, a.shape[-1])
    x1, h2, logits = _merge(rs(y_scan), rs(bonus), rs(g), rs(y_f), gates, xr, g1, sh2, sc2, p, w["router_t"], tm)
    y = _moe(h2, logits, x1, g2, p, w, cfg)
    k_out = qkv[:, :, WIDTH:2 * WIDTH].reshape(b, t, N_HEADS, HEAD_DIM)
    v_out = qkv[:, :, 2 * WIDTH:].reshape(b, t, N_HEADS, HEAD_DIM)
    return y.reshape(b, t, d), k_out, v_out, logf, wkv_new, shift_new


def kernel(x_prompt, x_sample, c_prompt, c_sample, cache_k, cache_v, cache_logf, page_table, state_wkv, state_shift, w_ada, b_ada, norm1_g, w_in, mu_shift, w0, w_w2, a0, w_a2, w_g2, k_k, k_a, r_k, lnx_w, lnx_b, b_f, w_br_r, w_br_f, w_out, norm2_g, w_router, e_bias, w_exp_gate, w_exp_up, w_exp_down, w_sh_gate, w_sh_up, w_sh_down, normf_g):
    p = dict(norm1_g=norm1_g, mu_shift=mu_shift, w0=w0, w_w2=w_w2, a0=a0, w_a2=w_a2, w_g2=w_g2, k_k=k_k, k_a=k_a,
             r_k=r_k, lnx_w=lnx_w, lnx_b=lnx_b, w_br_r=w_br_r, w_br_f=w_br_f, w_out=w_out, norm2_g=norm2_g,
             normf_g=normf_g, w_exp_gate=w_exp_gate, w_exp_up=w_exp_up, w_exp_down=w_exp_down)
    bp, tp, d = x_prompt.shape
    bs, ts, _ = x_sample.shape
    off_f = R_COLS + 3 * WIDTH
    off_g = off_f + N_HEADS
    perm = lambda a: a.reshape(a.shape[:-1] + (N_GROUPS, GROUP_SIZE)).swapaxes(-1, -2).reshape(a.shape)
    router = jnp.pad(perm(w_router), ((0, 0), (0, LANES - N_EXPERTS)))
    r_hi = router.astype(BF16)
    w = dict(
        in_r=w_in[:, :R_COLS].astype(BF16), in_r_f32=w_in[:, :R_COLS],
        in_qkv=w_in[:, R_COLS:off_f].astype(BF16),
        in_f=jnp.pad(w_in[:, off_f:off_g], ((0, 0), (0, LANES - N_HEADS))).astype(BF16),
        in_g=w_in[:, off_g:].astype(BF16),
        b_f_pad=jnp.pad(b_f, (0, LANES - N_HEADS)).reshape(1, LANES),
        router_t=jnp.stack([r_hi, (router - r_hi.astype(F32)).astype(BF16)]),
        e_bias_perm=perm(e_bias),
        sh_gate=w_sh_gate.astype(BF16), sh_up=w_sh_up.astype(BF16), sh_down=w_sh_down.astype(BF16),
    )
    mod = _dense(jnp.concatenate([c_prompt, c_sample], axis=0), w_ada, b_ada, act=True).reshape(bp + bs, 6, d)

    def attend_prompt(qkv, logf):
        lf = jnp.pad(logf, ((0, 0), (0, 0), (0, LANES - N_HEADS)))
        cum = _cumsum_time(lf, 256)[:, :, :N_HEADS]
        return _fox_prompt(qkv, cum, min(512, tp))

    def attend_sample(qkv, logf):
        return _fox_sample(qkv[:, :, :WIDTH], qkv[:, :, WIDTH:2 * WIDTH], qkv[:, :, 2 * WIDTH:], logf,
                           cache_k, cache_v, cache_logf, page_table, 8)

    cfg_p = dict(rows=(bp, tp), tm=min(512, tp), tn_r=R_COLS // 2, tt=min(256, tp), tc=16,
                 tm_route=min(1024, tp), te=256, tile_rows=256)
    cfg_s = dict(rows=(1, bs * ts), tm=bs * ts, tn_r=R_COLS // 2, tt=ts, tc=ts, tm_route=bs * ts, te=256,
                 tile_rows=bs * ts)
    yp, kp, vp, lfp, wkvp, shp = _layer(x_prompt, mod[:bp], jnp.zeros((bp, d), F32),
                                        jnp.zeros((bp, N_HEADS, HEAD_DIM, HEAD_DIM), F32), attend_prompt, p, w, cfg_p)
    ys, ks, vs, lfs, wkvs, shs = _layer(x_sample, mod[bp:], state_shift, state_wkv, attend_sample, p, w, cfg_s)
    return (yp, ys, kp, vp, lfp, wkvp, shp, ks, vs, lfs, wkvs, shs)
```

```python
import functools

import jax
import jax.numpy as jnp
from jax import lax
from jax.experimental import pallas as pl
from jax.experimental.pallas import tpu as pltpu

F32 = jnp.float32
BF16 = jnp.bfloat16

HEAD_DIM = 64
N_HEADS = 8
WIDTH = N_HEADS * HEAD_DIM
DECAY_LORA = 64
AAA_LORA = 64
GATE_LORA = 128
R_COLS = 3 * WIDTH + DECAY_LORA + AAA_LORA + GATE_LORA
LNX_EPS = 64e-5
NORM_EPS = 1e-6
N_EXPERTS = 64
N_GROUPS = 8
GROUP_SIZE = N_EXPERTS // N_GROUPS
TOPK_GROUPS = 4
TOP_K = 6
ROUTED_SCALE = 2.5
LANES = 128
RUN = 16
SORT_BLOCK = 256
VMEM_LIMIT = 56 * 1024 * 1024


def _cparams(sem):
    return pltpu.CompilerParams(dimension_semantics=sem, vmem_limit_bytes=VMEM_LIMIT)


def _dot(a, b):
    return jnp.dot(a, b, preferred_element_type=F32)


def _dot_nt(a, b):
    return lax.dot_general(a, b, (((1,), (1,)), ((), ())), preferred_element_type=F32)


def _split2(x):
    hi = x.astype(BF16)
    lo = (x - hi.astype(F32)).astype(BF16)
    return hi, lo


def _split3(x):
    hi = x.astype(BF16)
    r = x - hi.astype(F32)
    mid = r.astype(BF16)
    lo = (r - mid.astype(F32)).astype(BF16)
    return hi, mid, lo


def _dot_x3(x, m):
    hi, mid, lo = _split3(x)
    return _dot(hi, m) + _dot(mid, m) + _dot(lo, m)


def _dot_3x(m, x):
    hi, mid, lo = _split3(x)
    return _dot(m, hi) + _dot(m, mid) + _dot(m, lo)


def _dot_hp(x, w):
    xh, xm, xl = _split3(x)
    wh, wl = _split2(w)
    return _dot(xh, wh) + (_dot(xh, wl) + _dot(xm, wh)) + (_dot(xm, wl) + _dot(xl, wh))


def _sigmoid(x):
    return 1.0 / (1.0 + jnp.exp(-x))


def _softplus(x):
    return jnp.maximum(x, 0.0) + jnp.log1p(jnp.exp(-jnp.abs(x)))


def _silu(x):
    return x * _sigmoid(x)


def _rmsnorm(x, g):
    return x * lax.rsqrt(jnp.mean(x * x, axis=-1, keepdims=True) + NORM_EPS) * g


def _dense_kernel(x_ref, w_ref, b_ref, o_ref, *, act):
    x = x_ref[...]
    if act:
        x = _silu(x)
    o_ref[...] = _dot_hp(x, w_ref[...]) + b_ref[...]


def _dense(x, w, b, act, tn=512):
    m, k = x.shape
    n = w.shape[1]
    assert n % tn == 0
    return pl.pallas_call(
        functools.partial(_dense_kernel, act=act),
        grid=(n // tn,),
        in_specs=[pl.BlockSpec((m, k), lambda j: (0, 0)),
                  pl.BlockSpec((k, tn), lambda j: (0, j)),
                  pl.BlockSpec((1, tn), lambda j: (0, j))],
        out_specs=pl.BlockSpec((m, tn), lambda j: (0, j)),
        out_shape=jax.ShapeDtypeStruct((m, n), F32),
        compiler_params=_cparams(("parallel",)),
        name="dense",
    )(x, w, b.reshape(1, n))


def _in_proj_kernel(x_ref, sh_ref, sc_ref, g_ref, w_ref, b_ref, o_ref, h_ref, h_scr, *, epilogue, h_rows):
    @pl.when(pl.program_id(2) == 0)
    def _():
        h = _rmsnorm(x_ref[0], g_ref[...]) * (1.0 + sc_ref[0]) + sh_ref[0]
        h_scr[...] = h.astype(BF16)
        h_ref[0] = h[h.shape[0] - h_rows:, :]

    acc = _dot(h_scr[...], w_ref[...])
    if epilogue == "sigmoid":
        acc = _sigmoid(acc)
    elif epilogue == "log_sigmoid":
        acc = -_softplus(-(acc + b_ref[...]))
    o_ref[0] = acc.astype(o_ref.dtype)


def _in_proj(x, sh, sc, g, w, bias, *, tm, tn, epilogue, out_dtype, h_rows):
    b, t, d = x.shape
    n = w.shape[1]
    assert t % tm == 0 and n % tn == 0
    per_tok = sh.shape[1] != 1
    mrows = tm if per_tok else 1
    mod_map = (lambda bi, i, j: (bi, i, 0)) if per_tok else (lambda bi, i, j: (bi, 0, 0))
    if bias is None:
        bias = jnp.zeros((1, n), F32)
    out, h = pl.pallas_call(
        functools.partial(_in_proj_kernel, epilogue=epilogue, h_rows=h_rows),
        grid=(b, t // tm, n // tn),
        in_specs=[pl.BlockSpec((1, tm, d), lambda bi, i, j: (bi, i, 0)),
                  pl.BlockSpec((1, mrows, d), mod_map),
                  pl.BlockSpec((1, mrows, d), mod_map),
                  pl.BlockSpec((1, d), lambda bi, i, j: (0, 0)),
                  pl.BlockSpec((d, tn), lambda bi, i, j: (0, j)),
                  pl.BlockSpec((1, tn), lambda bi, i, j: (0, j))],
        out_specs=[pl.BlockSpec((1, tm, tn), lambda bi, i, j: (bi, i, j)),
                   pl.BlockSpec((1, h_rows, d), lambda bi, i, j: (bi, 0, 0))],
        out_shape=[jax.ShapeDtypeStruct((b, t, n), out_dtype),
                   jax.ShapeDtypeStruct((b, h_rows, d), F32)],
        scratch_shapes=[pltpu.VMEM((tm, d), BF16)],
        compiler_params=_cparams(("parallel", "arbitrary", "arbitrary")),
        name="in_proj_" + epilogue,
    )(x, sh, sc, g.reshape(1, d), w, bias)
    return out, h


def _prep_kernel(pr_ref, prev_ref, mu_ref, w0_ref, ww2_ref, a0_ref, wa2_ref, wg2_ref, kk_ref, ka_ref, rk_ref,
                 seg_ref, r_o, w_o, k_o, v_o, kk_o, kka_o, g_o, bonus_o, carry):
    @pl.when(pl.program_id(1) == 0)
    def _():
        carry[...] = prev_ref[0]

    pr = pr_ref[0]
    tt = pr.shape[0]
    first = lax.broadcasted_iota(jnp.int32, pr.shape, 0) == 0
    pprev = jnp.where(first, carry[...], pltpu.roll(pr, 1, axis=0))
    carry[...] = pr[tt - 1:tt, :]
    pm = pr + (pprev - pr) * mu_ref[...]
    r = pm[:, 0:WIDTH]
    k = pm[:, WIDTH:2 * WIDTH]
    v = pm[:, 2 * WIDTH:3 * WIDTH]
    lo = pm[:, 3 * WIDTH:3 * WIDTH + LANES]
    glo = pm[:, 3 * WIDTH + LANES:3 * WIDTH + 2 * LANES]
    seg = seg_ref[...]
    w_log = -_softplus(-(w0_ref[...] + _dot(jnp.tanh(lo).astype(BF16), ww2_ref[...]))) - 0.5
    decay = jnp.exp(-jnp.exp(w_log))
    a = _sigmoid(a0_ref[...] + _dot(lo.astype(BF16), wa2_ref[...]))
    g = _dot(_sigmoid(glo).astype(BF16), wg2_ref[...])
    kk = k * kk_ref[...]
    kk = kk / jnp.maximum(jnp.sqrt(_dot_x3(kk * kk, seg)), 1e-12)
    k2 = k * (1.0 + (a - 1.0) * ka_ref[...])
    r_o[0] = r
    w_o[0] = decay
    k_o[0] = k2
    v_o[0] = v
    kk_o[0] = kk
    kka_o[0] = kk * a
    g_o[0] = g
    bonus_o[0] = _dot_x3(r * k2 * rk_ref[...], seg) * v


def _seg_ones(width, seg):
    i = jnp.arange(width) // seg
    return (i[:, None] == i[None, :]).astype(BF16)


def _rwkv_prep(pr, prev, p, tt):
    b, t, _ = pr.shape
    zpad = jnp.zeros((LANES - DECAY_LORA, WIDTH), F32)
    ww2 = jnp.concatenate([p["w_w2"], zpad], axis=0).astype(BF16)
    wa2 = jnp.concatenate([zpad, p["w_a2"]], axis=0).astype(BF16)
    row = lambda a: a.reshape(1, -1)
    consts = [row(p["mu_shift"]), row(p["w0"]), ww2, row(p["a0"]), wa2, p["w_g2"].astype(BF16),
              row(p["k_k"]), row(p["k_a"]), row(p["r_k"]), _seg_ones(WIDTH, HEAD_DIM)]
    const_specs = [pl.BlockSpec(c.shape, lambda bi, i: (0, 0)) for c in consts]
    o_spec = pl.BlockSpec((1, tt, WIDTH), lambda bi, i: (bi, i, 0))
    return pl.pallas_call(
        _prep_kernel,
        grid=(b, t // tt),
        in_specs=[pl.BlockSpec((1, tt, R_COLS), lambda bi, i: (bi, i, 0)),
                  pl.BlockSpec((1, 1, R_COLS), lambda bi, i: (bi, 0, 0))] + const_specs,
        out_specs=[o_spec] * 8,
        out_shape=[jax.ShapeDtypeStruct((b, t, WIDTH), F32)] * 8,
        scratch_shapes=[pltpu.VMEM((1, R_COLS), F32)],
        compiler_params=_cparams(("parallel", "arbitrary")),
        name="rwkv_prep",
    )(pr, prev.reshape(b, 1, R_COLS), *consts)


def _scan_kernel(kk_ref, w_ref, k_ref, kka_ref, r_ref, v_ref, s0_ref, y_ref, s_ref, *, steps, n_slabs):
    @pl.when(pl.program_id(0) == 0)
    def _():
        s_ref[...] = s0_ref[...]

    def step(t, carry):
        kk = kk_ref[t]
        w = w_ref[t]
        k = k_ref[t]
        kka = kka_ref[t]
        r = r_ref[t]
        rw = r * w
        c1 = jnp.sum(r * kka, axis=0, keepdims=True)
        c2 = jnp.sum(r * k, axis=0, keepdims=True)
        for n in range(n_slabs):
            s = s_ref[n]
            sa = -jnp.sum(s * kk, axis=0, keepdims=True)
            yp = jnp.sum(s * rw, axis=0, keepdims=True)
            vn = v_ref[t, pl.ds(n, 1), :]
            s_ref[n] = s * w + sa * kka + vn * k
            y_ref[t, pl.ds(n, 1), :] = yp + c1 * sa + c2 * vn
        return carry

    lax.fori_loop(0, steps, step, 0)


def _wkv_scan(r, w, k, v, kk, kka, s0, tc):
    b, t, _ = r.shape
    bh = b * N_HEADS
    dup = max(1, LANES // bh)
    lanes = dup * bh
    ni = HEAD_DIM // dup

    def key_layout(x):
        x = x.reshape(b, t, N_HEADS, HEAD_DIM).transpose(1, 3, 0, 2).reshape(t, HEAD_DIM, 1, bh)
        return jnp.broadcast_to(x, (t, HEAD_DIM, dup, bh)).reshape(t, HEAD_DIM, lanes)

    v_l = v.reshape(b, t, N_HEADS, ni, dup).transpose(1, 3, 4, 0, 2).reshape(t, ni, lanes)
    s_l = s0.reshape(b, N_HEADS, ni, dup, HEAD_DIM).transpose(2, 4, 3, 0, 1).reshape(ni, HEAD_DIM, lanes)
    key_spec = pl.BlockSpec((tc, HEAD_DIM, lanes), lambda c: (c, 0, 0))
    val_spec = pl.BlockSpec((tc, ni, lanes), lambda c: (c, 0, 0))
    st_spec = pl.BlockSpec((ni, HEAD_DIM, lanes), lambda c: (0, 0, 0))
    y_l, s_out = pl.pallas_call(
        functools.partial(_scan_kernel, steps=tc, n_slabs=ni),
        grid=(t // tc,),
        in_specs=[key_spec] * 5 + [val_spec, st_spec],
        out_specs=[val_spec, st_spec],
        out_shape=[jax.ShapeDtypeStruct((t, ni, lanes), F32),
                   jax.ShapeDtypeStruct((ni, HEAD_DIM, lanes), F32)],
        compiler_params=_cparams(("arbitrary",)),
        name="wkv_scan",
    )(key_layout(kk), key_layout(w), key_layout(k), key_layout(kka), key_layout(r), v_l, s_l)
    y = y_l.reshape(t, ni, dup, b, N_HEADS).transpose(3, 0, 4, 1, 2).reshape(b, t, WIDTH)
    s_new = s_out.reshape(ni, HEAD_DIM, dup, b, N_HEADS).transpose(3, 4, 0, 2, 1).reshape(
        b, N_HEADS, HEAD_DIM, HEAD_DIM)
    return y, s_new


def _cumsum_kernel(x_ref, o_ref, carry):
    @pl.when(pl.program_id(1) == 0)
    def _():
        carry[...] = jnp.zeros_like(carry)

    x = x_ref[0]
    tb = x.shape[0]
    tril = (lax.broadcasted_iota(jnp.int32, (tb, tb), 1) <= lax.broadcasted_iota(jnp.int32, (tb, tb), 0)).astype(BF16)
    c = _dot_3x(tril, x) + carry[...]
    o_ref[0] = c
    carry[...] = c[tb - 1:tb, :]


def _cumsum_time(x, tb):
    b, t, n = x.shape
    spec = pl.BlockSpec((1, tb, n), lambda bi, i: (bi, i, 0))
    return pl.pallas_call(
        _cumsum_kernel, grid=(b, t // tb), in_specs=[spec], out_specs=spec,
        out_shape=jax.ShapeDtypeStruct((b, t, n), F32),
        scratch_shapes=[pltpu.VMEM((1, n), F32)],
        compiler_params=_cparams(("parallel", "arbitrary")),
        name="cumsum_time",
    )(x)


def _bias_lanes(x, col, lane, base, own, key_side):
    hi, mid, lo = _split3(col)
    first, second = (base + 3, base) if key_side else (base, base + 3)
    out = jnp.where(own, x, 0.0)
    out = jnp.where((lane >= second) & (lane < second + 3), 1.0, out)
    out = jnp.where(lane == first, hi.astype(F32), out)
    out = jnp.where(lane == first + 1, mid.astype(F32), out)
    return jnp.where(lane == first + 2, lo.astype(F32), out)


def _fox_prompt_kernel(q_ref, k_ref, v_ref, cq_ref, ck_ref, o_ref, kaug, vb, *, tq):
    i = pl.program_id(2)
    t = k_ref.shape[1]

    @pl.when(i == 0)
    def _():
        lane_k = lax.broadcasted_iota(jnp.int32, (t, LANES), 1)
        vb[...] = v_ref[0].astype(BF16)
        k = k_ref[0]
        for hh in range(2):
            own = (lane_k >= hh * HEAD_DIM) & (lane_k < (hh + 1) * HEAD_DIM)
            kaug[hh] = _bias_lanes(k, -ck_ref[0, hh], lane_k, (1 - hh) * HEAD_DIM, own, True).astype(BF16)

    lane = lax.broadcasted_iota(jnp.int32, (tq, LANES), 1)
    q = q_ref[0] * (HEAD_DIM ** -0.5)
    qa = []
    for hh in range(2):
        own = (lane >= hh * HEAD_DIM) & (lane < (hh + 1) * HEAD_DIM)
        qa.append(_bias_lanes(q, cq_ref[0, hh], lane, (1 - hh) * HEAD_DIM, own, False).astype(BF16))
    causal = lax.broadcasted_iota(jnp.int32, (tq, tq), 1) <= lax.broadcasted_iota(jnp.int32, (tq, tq), 0)

    def block(j, carry, masked):
        start = pl.multiple_of(j * tq, tq)
        v_blk = vb[pl.ds(start, tq), :]
        out = []
        for hh in range(2):
            m, l, acc = carry[hh]
            s = _dot_nt(qa[hh], kaug[hh, pl.ds(start, tq), :])
            if masked:
                s = jnp.where(causal, s, -jnp.inf)
            m_new = jnp.maximum(m, jnp.max(s, axis=-1, keepdims=True))
            p = jnp.exp(s - m_new)
            alpha = jnp.exp(m - m_new)
            out.append((m_new, alpha * l + jnp.sum(p, axis=-1, keepdims=True),
                        alpha * acc + _dot(p.astype(BF16), v_blk)))
        return tuple(out)

    one = (jnp.full((tq, 1), -jnp.inf, F32), jnp.zeros((tq, 1), F32), jnp.zeros((tq, LANES), F32))
    carry = lax.fori_loop(0, i, lambda j, c: block(j, c, False), (one, one))
    (_, l0, a0), (_, l1, a1) = block(i, carry, True)
    o_ref[0] = jnp.where(lane < HEAD_DIM, a0 / l0, a1 / l1).astype(o_ref.dtype)


def _fox_prompt(qkv, cum, tq):
    b, t, _ = qkv.shape
    npair = WIDTH // LANES
    cum_col = cum.transpose(0, 2, 1).reshape(b, N_HEADS, t, 1)
    return pl.pallas_call(
        functools.partial(_fox_prompt_kernel, tq=tq),
        grid=(b, npair, t // tq),
        in_specs=[pl.BlockSpec((1, tq, LANES), lambda bi, p, i: (bi, i, p)),
                  pl.BlockSpec((1, t, LANES), lambda bi, p, i: (bi, 0, npair + p)),
                  pl.BlockSpec((1, t, LANES), lambda bi, p, i: (bi, 0, 2 * npair + p)),
                  pl.BlockSpec((1, 2, tq, 1), lambda bi, p, i: (bi, p, i, 0)),
                  pl.BlockSpec((1, 2, t, 1), lambda bi, p, i: (bi, p, 0, 0))],
        out_specs=pl.BlockSpec((1, tq, LANES), lambda bi, p, i: (bi, i, p)),
        out_shape=jax.ShapeDtypeStruct((b, t, WIDTH), BF16),
        scratch_shapes=[pltpu.VMEM((2, t, LANES), BF16), pltpu.VMEM((t, LANES), BF16)],
        compiler_params=_cparams(("parallel", "parallel", "arbitrary")),
        name="fox_prompt",
    )(qkv, qkv, qkv, cum_col, cum_col)


def _page_sums_kernel(lf_ref, rev_ref, tot_ref):
    n = lf_ref.shape[1]
    a = lax.broadcasted_iota(jnp.int32, (n, n), 0)
    c = lax.broadcasted_iota(jnp.int32, (n, n), 1)
    hi, mid, lo = _split3(lf_ref[...])
    later = (a > c).astype(BF16)
    rev_ref[...] = _dot(hi, later) + _dot(mid, later) + _dot(lo, later)
    every = jnp.ones((n, n), BF16)
    tot_ref[...] = _dot(hi, every) + _dot(mid, every) + _dot(lo, every)


def _page_sums(lf_rows, rows):
    n_rows, n = lf_rows.shape
    spec = pl.BlockSpec((rows, n), lambda i: (i, 0))
    return pl.pallas_call(
        _page_sums_kernel, grid=(n_rows // rows,), in_specs=[spec], out_specs=[spec, spec],
        out_shape=[jax.ShapeDtypeStruct((n_rows, n), F32)] * 2,
        compiler_params=_cparams(("parallel",)),
        name="page_sums",
    )(lf_rows)


def _fox_sample_kernel(pt_ref, q_ref, kn_ref, vn_ref, lfn_ref, *rest, pages_per_step):
    npp = pages_per_step
    k_refs = rest[:npp]
    v_refs = rest[npp:2 * npp]
    rev_refs = rest[2 * npp:3 * npp]
    tot_refs = rest[3 * npp:4 * npp]
    o_ref = rest[4 * npp]
    qrep, m_s, l_s, acc_s, suf_s, cn_s = rest[4 * npp + 1:]
    step = pl.program_id(1)
    nq = q_ref.shape[1]
    rows = N_HEADS * nq
    page = kn_ref.shape[1]
    row_head = lax.broadcasted_iota(jnp.int32, (rows, WIDTH), 0) // nq
    lane_head = lax.broadcasted_iota(jnp.int32, (rows, WIDTH), 1) // HEAD_DIM

    def rep_heads(x):
        return jnp.concatenate([jnp.broadcast_to(x[h:h + 1, :], (nq, x.shape[1])) for h in range(N_HEADS)], axis=0)

    def update(s_list, pv):
        m_old = m_s[...]
        m_new = functools.reduce(jnp.maximum, [jnp.max(s, axis=-1, keepdims=True) for s in s_list] + [m_old])
        p_list = [jnp.exp(s - m_new) for s in s_list]
        alpha = jnp.exp(m_old - m_new)
        l_s[...] = alpha * l_s[...] + functools.reduce(jnp.add, [jnp.sum(p, axis=-1, keepdims=True) for p in p_list])
        acc_s[...] = alpha * acc_s[...] + functools.reduce(jnp.add, [pv(u, p.astype(BF16)) for u, p in enumerate(p_list)])
        m_s[...] = m_new

    @pl.when(step == 0)
    def _():
        q = q_ref[0] * (HEAD_DIM ** -0.5)
        qrep[...] = jnp.where(row_head == lane_head, jnp.concatenate([q] * N_HEADS, axis=0), 0.0).astype(BF16)
        key_i = lax.broadcasted_iota(jnp.int32, (page, page), 0)
        key_j = lax.broadcasted_iota(jnp.int32, (page, page), 1)
        cn_row = _dot_x3(rep_heads(lfn_ref[0]), (key_i <= key_j).astype(BF16))
        rq = lax.broadcasted_iota(jnp.int32, (rows, page), 0) % nq
        kc = lax.broadcasted_iota(jnp.int32, (rows, page), 1)
        cn_col = jnp.sum(jnp.where(kc == rq, cn_row, 0.0), axis=-1, keepdims=True)
        cn_s[...] = cn_col
        suf_s[...] = jnp.zeros_like(suf_s)
        m_s[...] = jnp.full_like(m_s, -jnp.inf)
        l_s[...] = jnp.zeros_like(l_s)
        acc_s[...] = jnp.zeros_like(acc_s)
        s = _dot_nt(qrep[...], kn_ref[0].astype(BF16)) + cn_col - cn_row
        vn = vn_ref[0].astype(BF16)
        update([jnp.where(kc <= rq, s, -jnp.inf)], lambda u, p: _dot(p, vn))

    q_all = qrep[...]
    cn = cn_s[...]
    suf = suf_s[...]
    s_list = []
    for u in range(npp):
        kt = k_refs[u][0].reshape(WIDTH, page).astype(BF16)
        s_list.append(_dot(q_all, kt) + ((cn + suf) + rep_heads(rev_refs[u][0])))
        suf = suf + rep_heads(tot_refs[u][0])
    update(s_list, lambda u, p: _dot_nt(p, v_refs[u][0].reshape(WIDTH, page).astype(BF16)))
    suf_s[...] = suf

    @pl.when(step == pl.num_programs(1) - 1)
    def _():
        o_sel = jnp.where(row_head == lane_head, acc_s[...] / l_s[...], 0.0)
        out = o_sel[0:nq]
        for h in range(1, N_HEADS):
            out = out + o_sel[h * nq:(h + 1) * nq]
        o_ref[0] = out.astype(o_ref.dtype)


def _fox_sample(q, k_new, v_new, logf_new, cache_k, cache_v, cache_logf, page_table, pages_per_step):
    b, tn, _ = q.shape
    n_pool, page = cache_k.shape[:2]
    n_pages = page_table.shape[1]
    npp = pages_per_step
    assert n_pages % npp == 0 and tn <= page
    rows = N_HEADS * tn
    ck = cache_k.transpose(0, 2, 3, 1)
    cv = cache_v.transpose(0, 2, 3, 1)
    clf = cache_logf.transpose(0, 2, 1).reshape(n_pool * N_HEADS, page)
    sum_rows = 2048 if clf.shape[0] % 2048 == 0 else clf.shape[0]
    rev, tot = (a.reshape(n_pool, N_HEADS, page) for a in _page_sums(clf, sum_rows))
    pad_rows = lambda x: jnp.pad(x, ((0, 0), (0, page - tn), (0, 0)))
    lfn = jnp.pad(logf_new.transpose(0, 2, 1), ((0, 0), (0, 0), (0, page - tn)))

    def page_map(u, nd):
        return lambda bi, s, pt: (pt[bi, n_pages - 1 - (s * npp + u)],) + (0,) * nd

    tok_spec = lambda r: pl.BlockSpec((1, r, WIDTH), lambda bi, s, pt: (bi, 0, 0))
    in_specs = ([tok_spec(tn), tok_spec(page), tok_spec(page),
                 pl.BlockSpec((1, N_HEADS, page), lambda bi, s, pt: (bi, 0, 0))]
                + [pl.BlockSpec((1, N_HEADS, HEAD_DIM, page), page_map(u, 3)) for u in range(npp)] * 2
                + [pl.BlockSpec((1, N_HEADS, page), page_map(u, 2)) for u in range(npp)] * 2)
    grid_spec = pltpu.PrefetchScalarGridSpec(
        num_scalar_prefetch=1,
        grid=(b, n_pages // npp),
        in_specs=in_specs,
        out_specs=tok_spec(tn),
        scratch_shapes=[pltpu.VMEM((rows, WIDTH), BF16), pltpu.VMEM((rows, 1), F32), pltpu.VMEM((rows, 1), F32),
                        pltpu.VMEM((rows, WIDTH), F32), pltpu.VMEM((rows, page), F32), pltpu.VMEM((rows, 1), F32)])
    return pl.pallas_call(
        functools.partial(_fox_sample_kernel, pages_per_step=npp),
        grid_spec=grid_spec,
        out_shape=jax.ShapeDtypeStruct((b, tn, WIDTH), BF16),
        compiler_params=_cparams(("parallel", "arbitrary")),
        name="fox_sample",
    )(page_table, q, pad_rows(k_new), pad_rows(v_new), lfn, *([ck] * npp), *([cv] * npp), *([rev] * npp),
      *([tot] * npp))


def _merge_kernel(y_ref, bonus_ref, g_ref, yf_ref, gates_ref, x_ref, g1_ref, sh2_ref, sc2_ref,
                  lnw_ref, lnb_ref, seg_ref, wr_ref, wf_ref, wo_ref, n2_ref, wrt_ref,
                  x1_ref, h2_ref, lg_ref):
    seg = seg_ref[...]
    y = y_ref[0]
    mu = _dot_x3(y, seg) * (1.0 / HEAD_DIM)
    d = y - mu
    var = _dot_x3(d * d, seg) * (1.0 / HEAD_DIM)
    yn = d * lax.rsqrt(var + LNX_EPS) * lnw_ref[...] + lnb_ref[...]
    yr = ((yn + bonus_ref[0]) * g_ref[0]).astype(BF16)
    gates = gates_ref[0].astype(F32)
    d_model = x_ref.shape[2]
    merged = gates[:, :d_model] * _dot(yr, wr_ref[...]) + gates[:, d_model:] * _dot(yf_ref[0], wf_ref[...])
    x1 = x_ref[0] + g1_ref[0] * _dot(merged.astype(BF16), wo_ref[...])
    x1_ref[0] = x1
    h2 = _rmsnorm(x1, n2_ref[...]) * (1.0 + sc2_ref[0]) + sh2_ref[0]
    h2_ref[0] = h2
    hh, hl = _split2(h2)
    wrt = wrt_ref[...]
    lg_ref[0] = _dot(hh, wrt[0]) + (_dot(hh, wrt[1]) + _dot(hl, wrt[0]))


def _merge(y, bonus, g, yf, gates, x, g1, sh2, sc2, p, w_router_t, tm):
    b, t, d = x.shape
    per_tok = g1.shape[1] != 1
    mrows = tm if per_tok else 1
    mod_map = (lambda bi, i: (bi, i, 0)) if per_tok else (lambda bi, i: (bi, 0, 0))
    tok = lambda n: pl.BlockSpec((1, tm, n), lambda bi, i: (bi, i, 0))
    mod = pl.BlockSpec((1, mrows, d), mod_map)
    row = lambda a: a.reshape(1, -1)
    consts = [row(p["lnx_w"]), row(p["lnx_b"]), _seg_ones(WIDTH, HEAD_DIM), p["w_br_r"].astype(BF16),
              p["w_br_f"].astype(BF16), p["w_out"].astype(BF16), row(p["norm2_g"]), w_router_t]
    const_specs = [pl.BlockSpec(c.shape, (lambda bi, i: (0, 0)) if c.ndim == 2 else (lambda bi, i: (0, 0, 0)))
                   for c in consts]
    return pl.pallas_call(
        _merge_kernel,
        grid=(b, t // tm),
        in_specs=[tok(WIDTH), tok(WIDTH), tok(WIDTH), tok(WIDTH), tok(2 * d), tok(d), mod, mod, mod] + const_specs,
        out_specs=[tok(d), tok(d), tok(LANES)],
        out_shape=[jax.ShapeDtypeStruct((b, t, d), F32), jax.ShapeDtypeStruct((b, t, d), F32),
                   jax.ShapeDtypeStruct((b, t, LANES), F32)],
        compiler_params=_cparams(("parallel", "parallel")),
        name="merge",
    )(y, bonus, g, yf, gates, x, g1, sh2, sc2, *consts)


def _route_kernel(lg_ref, bias_ref, row_ref, rowt_ref, wt_ref, cnt_ref):
    lt = lg_ref[...].T
    tm = lt.shape[1]
    score = _sigmoid(lt[:N_EXPERTS])
    biased = score + bias_ref[...][:N_EXPERTS]
    slab = [biased[k * N_GROUPS:(k + 1) * N_GROUPS] for k in range(GROUP_SIZE)]
    neg = jnp.full((N_GROUPS, tm), -jnp.inf, F32)
    m1 = functools.reduce(jnp.maximum, slab)
    taken = jnp.zeros((N_GROUPS, tm), jnp.bool_)
    m2 = neg
    for k in range(GROUP_SIZE):
        is_first = (slab[k] == m1) & jnp.logical_not(taken)
        taken = taken | is_first
        m2 = jnp.maximum(m2, jnp.where(is_first, neg, slab[k]))
    gs = m1 + m2
    g_iota = lax.broadcasted_iota(jnp.int32, (N_GROUPS, tm), 0)
    cnt = jnp.zeros((N_GROUPS, tm), jnp.int32)
    for g2 in range(N_GROUPS):
        other = gs[g2:g2 + 1, :]
        beats = (other > gs) | ((g_iota > g2) & (other == gs))
        cnt = cnt + beats.astype(jnp.int32)
    g_sel = cnt < TOPK_GROUPS
    cand = [jnp.where(g_sel, slab[k], neg) for k in range(GROUP_SIZE)]
    rank = [jnp.zeros((N_GROUPS, tm), jnp.int32) for _ in range(GROUP_SIZE)]
    for k2 in range(GROUP_SIZE):
        for g2 in range(N_GROUPS):
            other = cand[k2][g2:g2 + 1, :]
            for k in range(GROUP_SIZE):
                first = (g_iota >= g2) if k2 < k else (g_iota > g2)
                beats = (other > cand[k]) | (first & (other == cand[k]))
                rank[k] = rank[k] + beats.astype(jnp.int32)
    sel = [rank[k] < TOP_K for k in range(GROUP_SIZE)]
    sc = [score[k * N_GROUPS:(k + 1) * N_GROUPS] for k in range(GROUP_SIZE)]
    picked = [jnp.where(sel[k], sc[k], 0.0) for k in range(GROUP_SIZE)]
    total = jnp.sum(functools.reduce(jnp.add, picked), axis=0, keepdims=True)
    gate = jnp.concatenate([pk / total * ROUTED_SCALE for pk in picked], axis=0)
    chosen = jnp.concatenate([s.astype(F32) for s in sel], axis=0)
    chosen_b = chosen.astype(BF16)
    ri = lax.broadcasted_iota(jnp.int32, (N_EXPERTS, N_EXPERTS), 0)
    ci = lax.broadcasted_iota(jnp.int32, (N_EXPERTS, N_EXPERTS), 1)
    lower = (ci < ri).astype(BF16)
    ordinal = _dot(lower, chosen_b)
    ta = lax.broadcasted_iota(jnp.int32, (tm, tm), 0)
    tc = lax.broadcasted_iota(jnp.int32, (tm, tm), 1)
    rank_tok = _dot(chosen_b, (ta < tc).astype(BF16))
    count = jnp.sum(chosen, axis=1, keepdims=True)
    runs = jnp.broadcast_to(jnp.floor((count + (RUN - 1)) * (1.0 / RUN)), (N_EXPERTS, LANES))
    cnt_ref[0] = runs
    start = _dot(lower, runs.astype(BF16))[:, 0:1] * float(RUN)
    tile_row = start + rank_tok
    r_rows, w_rows = [], []
    for n in range(TOP_K):
        hit = (chosen > 0.0) & (ordinal == float(n))
        r_rows.append(jnp.sum(jnp.where(hit, tile_row, 0.0), axis=0, keepdims=True))
        w_rows.append(jnp.sum(jnp.where(hit, gate, 0.0), axis=0, keepdims=True))
    zrow = jnp.zeros((1, tm), F32)
    pad8 = lambda rows: jnp.concatenate(rows + [zrow] * (8 - TOP_K), axis=0)
    zpad = jnp.zeros((LANES - 8, tm), F32)
    row_ref[...] = pad8(r_rows).astype(jnp.int32)
    rowt_ref[...] = jnp.concatenate([pad8(r_rows), zpad], axis=0).T.astype(jnp.int32)
    wt_ref[...] = jnp.concatenate([pad8(w_rows), zpad], axis=0).T


def _route(logits, e_bias_perm, tm):
    m = logits.shape[0]
    bias = jnp.pad(e_bias_perm, (0, LANES - N_EXPERTS)).reshape(LANES, 1)
    tok_spec = pl.BlockSpec((tm, LANES), lambda i: (i, 0))
    return pl.pallas_call(
        _route_kernel,
        grid=(m // tm,),
        in_specs=[tok_spec, pl.BlockSpec((LANES, 1), lambda i: (0, 0))],
        out_specs=[pl.BlockSpec((8, tm), lambda i: (0, i)), tok_spec, tok_spec,
                   pl.BlockSpec((1, N_EXPERTS, LANES), lambda i: (i, 0, 0))],
        out_shape=[jax.ShapeDtypeStruct((8, m), jnp.int32), jax.ShapeDtypeStruct((m, LANES), jnp.int32),
                   jax.ShapeDtypeStruct((m, LANES), F32), jax.ShapeDtypeStruct((m // tm, N_EXPERTS, LANES), F32)],
        compiler_params=_cparams(("parallel",)),
        name="route",
    )(logits, bias)


def _sorted_capacity(tm):
    rows = TOP_K * tm + N_EXPERTS * (RUN - 1)
    return -(-rows // SORT_BLOCK) * SORT_BLOCK


def _run_pieces(i, nrun_ref, src_ref, dst_ref, piece):
    def per_expert(e, total):
        j = i * N_EXPERTS + e
        n, s0, d0 = nrun_ref[j], src_ref[j], dst_ref[j]

        def one(c, carry):
            piece(s0 + c, d0 + c).start()
            return carry

        lax.fori_loop(0, n, one, 0)
        return total + n

    return lax.fori_loop(0, N_EXPERTS, per_expert, 0)


def _sort_kernel(nrun_ref, src_ref, dst_ref, nblk_ref, row_ref, h_ref, xs_in_ref, xs_ref, buf, sem):
    del xs_in_ref
    i = pl.program_id(0)
    tm = h_ref.shape[0]
    hb = h_ref[...].astype(BF16)
    rows = row_ref[...]
    riota = lax.broadcasted_iota(jnp.int32, (SORT_BLOCK, tm), 0)

    def block(b, carry):
        r0 = pl.multiple_of(b * SORT_BLOCK, SORT_BLOCK)
        hit = riota + r0 == rows[0:1, :]
        for n in range(1, TOP_K):
            hit = hit | (riota + r0 == rows[n:n + 1, :])
        buf[pl.ds(r0, SORT_BLOCK), :] = _dot(hit.astype(BF16), hb).astype(BF16)
        return carry

    lax.fori_loop(0, nblk_ref[i], block, 0)

    def piece(s, d):
        return pltpu.make_async_copy(buf.at[pl.ds(pl.multiple_of(s * RUN, RUN), RUN), :],
                                     xs_ref.at[pl.ds(pl.multiple_of(d * RUN, RUN), RUN), :], sem)

    started = _run_pieces(i, nrun_ref, src_ref, dst_ref, piece)

    def drain(c, carry):
        piece(0, 0).wait()
        return carry

    lax.fori_loop(0, started, drain, 0)


def _sort_rows(h2, row, tables, n_rows, tm):
    m, d = h2.shape
    grid_spec = pltpu.PrefetchScalarGridSpec(
        num_scalar_prefetch=4,
        grid=(m // tm,),
        in_specs=[pl.BlockSpec((8, tm), lambda i, *_: (0, i)), pl.BlockSpec((tm, d), lambda i, *_: (i, 0)),
                  pl.BlockSpec(memory_space=pl.ANY)],
        out_specs=pl.BlockSpec(memory_space=pl.ANY),
        scratch_shapes=[pltpu.VMEM((_sorted_capacity(tm), d), BF16), pltpu.SemaphoreType.DMA(())])
    return pl.pallas_call(
        _sort_kernel,
        grid_spec=grid_spec,
        out_shape=jax.ShapeDtypeStruct((n_rows, d), BF16),
        input_output_aliases={6: 0},
        compiler_params=_cparams(("arbitrary",)),
        name="sort_rows",
    )(*tables, row, h2, jnp.zeros((n_rows, d), BF16))


def _expert_kernel(te_ref, nu_ref, x_ref, wg_ref, wu_ref, wd_ref, o_ref, wgb, wub, wdb):
    i = pl.program_id(0)
    changed = (i == 0) | (te_ref[i] != te_ref[jnp.maximum(i - 1, 0)])

    @pl.when(changed)
    def _():
        wgb[...] = wg_ref[0].astype(BF16)
        wub[...] = wu_ref[0].astype(BF16)
        wdb[...] = wd_ref[0].astype(BF16)

    @pl.when(i < nu_ref[0])
    def _():
        x = x_ref[...]
        act = _silu(_dot(x, wgb[...])) * _dot(x, wub[...])
        o_ref[...] = _dot(act.astype(BF16), wdb[...]).astype(o_ref.dtype)

    @pl.when(i >= nu_ref[0])
    def _():
        o_ref[...] = jnp.zeros_like(o_ref)


def _expert_tiles(xs, tile_expert, n_used, wg, wu, wd, te):
    n_rows, d = xs.shape
    ff = wg.shape[2]
    last = lambda i, nu: jnp.minimum(i, nu[0] - 1)
    grid_spec = pltpu.PrefetchScalarGridSpec(
        num_scalar_prefetch=2,
        grid=(n_rows // te,),
        in_specs=[pl.BlockSpec((te, d), lambda i, tx, nu: (last(i, nu), 0)),
                  pl.BlockSpec((1, d, ff), lambda i, tx, nu: (tx[i], 0, 0)),
                  pl.BlockSpec((1, d, ff), lambda i, tx, nu: (tx[i], 0, 0)),
                  pl.BlockSpec((1, ff, d), lambda i, tx, nu: (tx[i], 0, 0))],
        out_specs=pl.BlockSpec((te, d), lambda i, tx, nu: (i, 0)),
        scratch_shapes=[pltpu.VMEM((d, ff), BF16), pltpu.VMEM((d, ff), BF16), pltpu.VMEM((ff, d), BF16)])
    return pl.pallas_call(
        _expert_kernel,
        grid_spec=grid_spec,
        out_shape=jax.ShapeDtypeStruct((n_rows, d), BF16),
        compiler_params=_cparams(("arbitrary",)),
        name="expert_tiles",
    )(tile_expert, n_used, xs, wg, wu, wd)


def _combine_kernel(nrun_ref, src_ref, dst_ref, nblk_ref, rowt_ref, wt_ref, h_ref, x1_ref, g2_ref, nf_ref,
                    wsg_ref, wsu_ref, wsd_ref, os_ref, y_ref, buf, sem):
    i = pl.program_id(0) * pl.num_programs(1) + pl.program_id(1)
    tm = h_ref.shape[1]
    nblk = nblk_ref[i]
    tail = pl.multiple_of((nblk - 1) * SORT_BLOCK, SORT_BLOCK)
    buf[pl.ds(tail, SORT_BLOCK), :] = jnp.zeros((SORT_BLOCK, buf.shape[1]), BF16)

    def piece(s, d):
        return pltpu.make_async_copy(os_ref.at[pl.ds(pl.multiple_of(d * RUN, RUN), RUN), :],
                                     buf.at[pl.ds(pl.multiple_of(s * RUN, RUN), RUN), :], sem)

    started = _run_pieces(i, nrun_ref, src_ref, dst_ref, piece)
    h = h_ref[0].astype(BF16)
    shared = _dot((_silu(_dot(h, wsg_ref[...])) * _dot(h, wsu_ref[...])).astype(BF16), wsd_ref[...])

    def drain(c, carry):
        piece(0, 0).wait()
        return carry

    lax.fori_loop(0, started, drain, 0)
    w = wt_ref[0]
    rt = rowt_ref[0]
    ciota = lax.broadcasted_iota(jnp.int32, (tm, SORT_BLOCK), 1)

    def block(b, acc):
        r0 = pl.multiple_of(b * SORT_BLOCK, SORT_BLOCK)
        c = jnp.where(ciota + r0 == rt[:, 0:1], w[:, 0:1], 0.0)
        for n in range(1, TOP_K):
            c = c + jnp.where(ciota + r0 == rt[:, n:n + 1], w[:, n:n + 1], 0.0)
        return acc + _dot(c.astype(BF16), buf[pl.ds(r0, SORT_BLOCK), :])

    routed = lax.fori_loop(0, nblk, block, jnp.zeros((tm, buf.shape[1]), F32))
    x2 = x1_ref[0] + g2_ref[0] * (routed + shared)
    y_ref[0] = _rmsnorm(x2, nf_ref[...])


def _combine(os, rowt, wt, tables, h2, x1, g2, normf_g, wsg, wsu, wsd, tm):
    b, t, d = x1.shape
    nt = t // tm
    per_tok = g2.shape[1] != 1
    mrows = tm if per_tok else 1
    mod_map = (lambda bi, i, *_: (bi, i, 0)) if per_tok else (lambda bi, i, *_: (bi, 0, 0))
    tok = lambda n: pl.BlockSpec((1, tm, n), lambda bi, i, *_: (bi, i, 0))
    const = lambda a: pl.BlockSpec(a.shape, lambda bi, i, *_: (0, 0))
    grid_spec = pltpu.PrefetchScalarGridSpec(
        num_scalar_prefetch=4,
        grid=(b, nt),
        in_specs=[tok(LANES), tok(LANES), tok(d), tok(d), pl.BlockSpec((1, mrows, d), mod_map),
                  pl.BlockSpec((1, d), lambda bi, i, *_: (0, 0)), const(wsg), const(wsu), const(wsd),
                  pl.BlockSpec(memory_space=pl.ANY)],
        out_specs=tok(d),
        scratch_shapes=[pltpu.VMEM((_sorted_capacity(tm), d), BF16), pltpu.SemaphoreType.DMA(())])
    return pl.pallas_call(
        _combine_kernel,
        grid_spec=grid_spec,
        out_shape=jax.ShapeDtypeStruct((b, t, d), F32),
        compiler_params=_cparams(("arbitrary", "arbitrary")),
        name="combine",
    )(*tables, rowt.reshape(b, t, LANES), wt.reshape(b, t, LANES), h2, x1, g2, normf_g.reshape(1, d), wsg, wsu, wsd, os)


def _moe(h2, logits, x1, g2, p, w, cfg):
    bx, tx, d = x1.shape
    m = bx * tx
    te, tm = cfg["te"], cfg["tm_route"]
    row, rowt, wt, cnt = _route(logits.reshape(m, LANES), w["e_bias_perm"], tm)
    runs = cnt[:, :, 0].astype(jnp.int32)
    src = jnp.cumsum(runs, axis=1) - runs
    nblk = (jnp.sum(runs, axis=1) * RUN + SORT_BLOCK - 1) // SORT_BLOCK
    per_expert = jnp.sum(runs, axis=0)
    tiles_e = (per_expert * RUN + te - 1) // te
    ends = jnp.cumsum(tiles_e)
    dst = ((ends - tiles_e) * (te // RUN))[None, :] + jnp.cumsum(runs, axis=0) - runs
    n_tiles = (m * TOP_K + (m // tm) * N_EXPERTS * (RUN - 1)) // te + N_EXPERTS
    n_used = ends[-1:].astype(jnp.int32)
    tile_ids = jnp.minimum(jnp.arange(n_tiles, dtype=jnp.int32), n_used[0] - 1)
    tile_row = jnp.minimum(jnp.sum(ends[None, :] <= tile_ids[:, None], axis=1), N_EXPERTS - 1)
    tile_expert = ((tile_row % N_GROUPS) * GROUP_SIZE + tile_row // N_GROUPS).astype(jnp.int32)
    flat = lambda a: a.reshape(-1).astype(jnp.int32)
    tables = (flat(runs), flat(src), flat(dst), flat(nblk))
    xs = _sort_rows(h2.reshape(m, d), row, tables, n_tiles * te, tm)
    os = _expert_tiles(xs, tile_expert, n_used, p["w_exp_gate"], p["w_exp_up"], p["w_exp_down"], te)
    return _combine(os, rowt, wt, tables, h2, x1, g2, p["normf_g"], w["sh_gate"], w["sh_up"], w["sh_down"], tm)


def _layer(x, mod, shift_prev, wkv0, attend, p, w, cfg):
    b, t, d = x.shape
    bx, tx = cfg["rows"]
    tm = cfg["tm"]
    xr = x.reshape(bx, tx, d)
    if bx == b:
        part = lambda i: mod[:, i:i + 1, :]
    else:
        part = lambda i: jnp.repeat(mod[:, i, :], t, axis=0).reshape(bx, tx, d)
    sh1, sc1, g1, sh2, sc2, g2 = (part(i) for i in range(6))
    h_rows = 1 if bx == b else tx
    proj = functools.partial(_in_proj, xr, sh1, sc1, p["norm1_g"], tm=tm, h_rows=h_rows)
    pr, h_keep = proj(w["in_r"], None, tn=cfg["tn_r"], epilogue="none", out_dtype=F32)
    qkv, _ = proj(w["in_qkv"], None, tn=512, epilogue="none", out_dtype=F32)
    logf_pad, _ = proj(w["in_f"], w["b_f_pad"], tn=LANES, epilogue="log_sigmoid", out_dtype=F32)
    gates, _ = proj(w["in_g"], None, tn=512, epilogue="sigmoid", out_dtype=BF16)
    shift_new = h_keep[:, 0, :] if bx == b else h_keep.reshape(b, t, d)[:, -1, :]
    logf = logf_pad.reshape(b, t, LANES)[:, :, :N_HEADS]
    qkv = qkv.reshape(b, t, 3 * WIDTH)

    prev = _dense(shift_prev, w["in_r_f32"], jnp.zeros((R_COLS,), F32), act=False, tn=R_COLS // 2)
    r, dec, k2, v, kk, kka, g, bonus = _rwkv_prep(pr.reshape(b, t, R_COLS), prev, p, cfg["tt"])
    y_scan, wkv_new = _wkv_scan(r, dec, k2, v, kk, kka, wkv0, cfg["tc"])

    y_f = attend(qkv, logf)

    rs = lambda a: a.reshape(bx, tx, a.shape[-1])
    x1, h2, logits = _merge(rs(y_scan), rs(bonus), rs(g), rs(y_f), gates, xr, g1, sh2, sc2, p, w["router_t"], tm)
    y = _moe(h2, logits, x1, g2, p, w, cfg)
    k_out = qkv[:, :, WIDTH:2 * WIDTH].reshape(b, t, N_HEADS, HEAD_DIM)
    v_out = qkv[:, :, 2 * WIDTH:].reshape(b, t, N_HEADS, HEAD_DIM)
    return y.reshape(b, t, d), k_out, v_out, logf, wkv_new, shift_new


def kernel(x_prompt, x_sample, c_prompt, c_sample, cache_k, cache_v, cache_logf, page_table, state_wkv, state_shift, w_ada, b_ada, norm1_g, w_in, mu_shift, w0, w_w2, a0, w_a2, w_g2, k_k, k_a, r_k, lnx_w, lnx_b, b_f, w_br_r, w_br_f, w_out, norm2_g, w_router, e_bias, w_exp_gate, w_exp_up, w_exp_down, w_sh_gate, w_sh_up, w_sh_down, normf_g):
    p = dict(norm1_g=norm1_g, mu_shift=mu_shift, w0=w0, w_w2=w_w2, a0=a0, w_a2=w_a2, w_g2=w_g2, k_k=k_k, k_a=k_a,
             r_k=r_k, lnx_w=lnx_w, lnx_b=lnx_b, w_br_r=w_br_r, w_br_f=w_br_f, w_out=w_out, norm2_g=norm2_g,
             normf_g=normf_g, w_exp_gate=w_exp_gate, w_exp_up=w_exp_up, w_exp_down=w_exp_down)
    bp, tp, d = x_prompt.shape
    bs, ts, _ = x_sample.shape
    off_f = R_COLS + 3 * WIDTH
    off_g = off_f + N_HEADS
    perm = lambda a: a.reshape(a.shape[:-1] + (N_GROUPS, GROUP_SIZE)).swapaxes(-1, -2).reshape(a.shape)
    router = jnp.pad(perm(w_router), ((0, 0), (0, LANES - N_EXPERTS)))
    r_hi = router.astype(BF16)
    w = dict(
        in_r=w_in[:, :R_COLS].astype(BF16), in_r_f32=w_in[:, :R_COLS],
        in_qkv=w_in[:, R_COLS:off_f].astype(BF16),
        in_f=jnp.pad(w_in[:, off_f:off_g], ((0, 0), (0, LANES - N_HEADS))).astype(BF16),
        in_g=w_in[:, off_g:].astype(BF16),
        b_f_pad=jnp.pad(b_f, (0, LANES - N_HEADS)).reshape(1, LANES),
        router_t=jnp.stack([r_hi, (router - r_hi.astype(F32)).astype(BF16)]),
        e_bias_perm=perm(e_bias),
        sh_gate=w_sh_gate.astype(BF16), sh_up=w_sh_up.astype(BF16), sh_down=w_sh_down.astype(BF16),
    )
    mod = _dense(jnp.concatenate([c_prompt, c_sample], axis=0), w_ada, b_ada, act=True).reshape(bp + bs, 6, d)

    def attend_prompt(qkv, logf):
        lf = jnp.pad(logf, ((0, 0), (0, 0), (0, LANES - N_HEADS)))
        cum = _cumsum_time(lf, 256)[:, :, :N_HEADS]
        return _fox_prompt(qkv, cum, min(512, tp))

    def attend_sample(qkv, logf):
        return _fox_sample(qkv[:, :, :WIDTH], qkv[:, :, WIDTH:2 * WIDTH], qkv[:, :, 2 * WIDTH:], logf,
                           cache_k, cache_v, cache_logf, page_table, 8)

    cfg_p = dict(rows=(bp, tp), tm=min(512, tp), tn_r=R_COLS // 2, tt=min(256, tp), tc=16,
                 tm_route=min(512, tp), te=256)
    cfg_s = dict(rows=(1, bs * ts), tm=bs * ts, tn_r=R_COLS // 2, tt=ts, tc=ts, tm_route=bs * ts, te=256)
    yp, kp, vp, lfp, wkvp, shp = _layer(x_prompt, mod[:bp], jnp.zeros((bp, d), F32),
                                        jnp.zeros((bp, N_HEADS, HEAD_DIM, HEAD_DIM), F32), attend_prompt, p, w, cfg_p)
    ys, ks, vs, lfs, wkvs, shs = _layer(x_sample, mod[bp:], state_shift, state_wkv, attend_sample, p, w, cfg_s)
    return (yp, ys, kp, vp, lfp, wkvp, shp, ks, vs, lfs, wkvs, shs)
```

```python
import functools

import jax
import jax.numpy as jnp
from jax import lax
from jax.experimental import pallas as pl
from jax.experimental.pallas import tpu as pltpu

F32 = jnp.float32
BF16 = jnp.bfloat16

HEAD_DIM = 64
N_HEADS = 8
WIDTH = N_HEADS * HEAD_DIM
DECAY_LORA = 64
AAA_LORA = 64
GATE_LORA = 128
R_COLS = 3 * WIDTH + DECAY_LORA + AAA_LORA + GATE_LORA
LNX_EPS = 64e-5
NORM_EPS = 1e-6
N_EXPERTS = 64
N_GROUPS = 8
GROUP_SIZE = N_EXPERTS // N_GROUPS
TOPK_GROUPS = 4
TOP_K = 6
ROUTED_SCALE = 2.5
LANES = 128
RUN = 16
SORT_BLOCK = 256
VMEM_LIMIT = 56 * 1024 * 1024


def _cparams(sem):
    return pltpu.CompilerParams(dimension_semantics=sem, vmem_limit_bytes=VMEM_LIMIT)


def _dot(a, b):
    return jnp.dot(a, b, preferred_element_type=F32)


def _dot_nt(a, b):
    return lax.dot_general(a, b, (((1,), (1,)), ((), ())), preferred_element_type=F32)


def _split2(x):
    hi = x.astype(BF16)
    lo = (x - hi.astype(F32)).astype(BF16)
    return hi, lo


def _split3(x):
    hi = x.astype(BF16)
    r = x - hi.astype(F32)
    mid = r.astype(BF16)
    lo = (r - mid.astype(F32)).astype(BF16)
    return hi, mid, lo


def _dot_x3(x, m):
    hi, mid, lo = _split3(x)
    return _dot(hi, m) + _dot(mid, m) + _dot(lo, m)


def _dot_3x(m, x):
    hi, mid, lo = _split3(x)
    return _dot(m, hi) + _dot(m, mid) + _dot(m, lo)


def _dot_hp(x, w):
    xh, xm, xl = _split3(x)
    wh, wl = _split2(w)
    return _dot(xh, wh) + (_dot(xh, wl) + _dot(xm, wh)) + (_dot(xm, wl) + _dot(xl, wh))


def _sigmoid(x):
    return 1.0 / (1.0 + jnp.exp(-x))


def _softplus(x):
    return jnp.maximum(x, 0.0) + jnp.log1p(jnp.exp(-jnp.abs(x)))


def _silu(x):
    return x * _sigmoid(x)


def _rmsnorm(x, g):
    return x * lax.rsqrt(jnp.mean(x * x, axis=-1, keepdims=True) + NORM_EPS) * g


def _dense_kernel(x_ref, w_ref, b_ref, o_ref, *, act):
    x = x_ref[...]
    if act:
        x = _silu(x)
    o_ref[...] = _dot_hp(x, w_ref[...]) + b_ref[...]


def _dense(x, w, b, act, tn=512):
    m, k = x.shape
    n = w.shape[1]
    assert n % tn == 0
    return pl.pallas_call(
        functools.partial(_dense_kernel, act=act),
        grid=(n // tn,),
        in_specs=[pl.BlockSpec((m, k), lambda j: (0, 0)),
                  pl.BlockSpec((k, tn), lambda j: (0, j)),
                  pl.BlockSpec((1, tn), lambda j: (0, j))],
        out_specs=pl.BlockSpec((m, tn), lambda j: (0, j)),
        out_shape=jax.ShapeDtypeStruct((m, n), F32),
        compiler_params=_cparams(("parallel",)),
        name="dense",
    )(x, w, b.reshape(1, n))


def _in_proj_kernel(x_ref, sh_ref, sc_ref, g_ref, w_ref, b_ref, o_ref, h_ref, h_scr, *, epilogue, h_rows):
    @pl.when(pl.program_id(2) == 0)
    def _():
        h = _rmsnorm(x_ref[0], g_ref[...]) * (1.0 + sc_ref[0]) + sh_ref[0]
        h_scr[...] = h.astype(BF16)
        h_ref[0] = h[h.shape[0] - h_rows:, :]

    acc = _dot(h_scr[...], w_ref[...])
    if epilogue == "sigmoid":
        acc = _sigmoid(acc)
    elif epilogue == "log_sigmoid":
        acc = -_softplus(-(acc + b_ref[...]))
    o_ref[0] = acc.astype(o_ref.dtype)


def _in_proj(x, sh, sc, g, w, bias, *, tm, tn, epilogue, out_dtype, h_rows):
    b, t, d = x.shape
    n = w.shape[1]
    assert t % tm == 0 and n % tn == 0
    per_tok = sh.shape[1] != 1
    mrows = tm if per_tok else 1
    mod_map = (lambda bi, i, j: (bi, i, 0)) if per_tok else (lambda bi, i, j: (bi, 0, 0))
    if bias is None:
        bias = jnp.zeros((1, n), F32)
    out, h = pl.pallas_call(
        functools.partial(_in_proj_kernel, epilogue=epilogue, h_rows=h_rows),
        grid=(b, t // tm, n // tn),
        in_specs=[pl.BlockSpec((1, tm, d), lambda bi, i, j: (bi, i, 0)),
                  pl.BlockSpec((1, mrows, d), mod_map),
                  pl.BlockSpec((1, mrows, d), mod_map),
                  pl.BlockSpec((1, d), lambda bi, i, j: (0, 0)),
                  pl.BlockSpec((d, tn), lambda bi, i, j: (0, j)),
                  pl.BlockSpec((1, tn), lambda bi, i, j: (0, j))],
        out_specs=[pl.BlockSpec((1, tm, tn), lambda bi, i, j: (bi, i, j)),
                   pl.BlockSpec((1, h_rows, d), lambda bi, i, j: (bi, 0, 0))],
        out_shape=[jax.ShapeDtypeStruct((b, t, n), out_dtype),
                   jax.ShapeDtypeStruct((b, h_rows, d), F32)],
        scratch_shapes=[pltpu.VMEM((tm, d), BF16)],
        compiler_params=_cparams(("parallel", "arbitrary", "arbitrary")),
        name="in_proj_" + epilogue,
    )(x, sh, sc, g.reshape(1, d), w, bias)
    return out, h


def _prep_kernel(pr_ref, prev_ref, mu_ref, w0_ref, ww2_ref, a0_ref, wa2_ref, wg2_ref, kk_ref, ka_ref, rk_ref,
                 seg_ref, r_o, w_o, k_o, v_o, kk_o, kka_o, g_o, bonus_o, carry):
    @pl.when(pl.program_id(1) == 0)
    def _():
        carry[...] = prev_ref[0]

    pr = pr_ref[0]
    tt = pr.shape[0]
    first = lax.broadcasted_iota(jnp.int32, pr.shape, 0) == 0
    pprev = jnp.where(first, carry[...], pltpu.roll(pr, 1, axis=0))
    carry[...] = pr[tt - 1:tt, :]
    pm = pr + (pprev - pr) * mu_ref[...]
    r = pm[:, 0:WIDTH]
    k = pm[:, WIDTH:2 * WIDTH]
    v = pm[:, 2 * WIDTH:3 * WIDTH]
    lo = pm[:, 3 * WIDTH:3 * WIDTH + LANES]
    glo = pm[:, 3 * WIDTH + LANES:3 * WIDTH + 2 * LANES]
    seg = seg_ref[...]
    w_log = -_softplus(-(w0_ref[...] + _dot(jnp.tanh(lo).astype(BF16), ww2_ref[...]))) - 0.5
    decay = jnp.exp(-jnp.exp(w_log))
    a = _sigmoid(a0_ref[...] + _dot(lo.astype(BF16), wa2_ref[...]))
    g = _dot(_sigmoid(glo).astype(BF16), wg2_ref[...])
    kk = k * kk_ref[...]
    kk = kk / jnp.maximum(jnp.sqrt(_dot_x3(kk * kk, seg)), 1e-12)
    k2 = k * (1.0 + (a - 1.0) * ka_ref[...])
    for d in range(r_o.shape[0]):
        r_o[d, 0] = r
        w_o[d, 0] = decay
        k_o[d, 0] = k2
        kk_o[d, 0] = kk
        kka_o[d, 0] = kk * a
    v_o[0] = v
    g_o[0] = g
    bonus_o[0] = _dot_x3(r * k2 * rk_ref[...], seg) * v


def _seg_ones(width, seg):
    i = jnp.arange(width) // seg
    return (i[:, None] == i[None, :]).astype(BF16)


def _lane_dup(b):
    return max(1, LANES // (b * N_HEADS))


def _rwkv_prep(pr, prev, p, tt):
    b, t, _ = pr.shape
    dup = _lane_dup(b)
    zpad = jnp.zeros((LANES - DECAY_LORA, WIDTH), F32)
    ww2 = jnp.concatenate([p["w_w2"], zpad], axis=0).astype(BF16)
    wa2 = jnp.concatenate([zpad, p["w_a2"]], axis=0).astype(BF16)
    row = lambda a: a.reshape(1, -1)
    consts = [row(p["mu_shift"]), row(p["w0"]), ww2, row(p["a0"]), wa2, p["w_g2"].astype(BF16),
              row(p["k_k"]), row(p["k_a"]), row(p["r_k"]), _seg_ones(WIDTH, HEAD_DIM)]
    const_specs = [pl.BlockSpec(c.shape, lambda bi, i: (0, 0)) for c in consts]
    o_spec = pl.BlockSpec((1, tt, WIDTH), lambda bi, i: (bi, i, 0))
    k_spec = pl.BlockSpec((dup, 1, tt, WIDTH), lambda bi, i: (0, bi, i, 0))
    o_shape = jax.ShapeDtypeStruct((b, t, WIDTH), F32)
    k_shape = jax.ShapeDtypeStruct((dup, b, t, WIDTH), F32)
    r, dec, k2, v, kk, kka, g, bonus = pl.pallas_call(
        _prep_kernel,
        grid=(b, t // tt),
        in_specs=[pl.BlockSpec((1, tt, R_COLS), lambda bi, i: (bi, i, 0)),
                  pl.BlockSpec((1, 1, R_COLS), lambda bi, i: (bi, 0, 0))] + const_specs,
        out_specs=[k_spec, k_spec, k_spec, o_spec, k_spec, k_spec, o_spec, o_spec],
        out_shape=[k_shape, k_shape, k_shape, o_shape, k_shape, k_shape, o_shape, o_shape],
        scratch_shapes=[pltpu.VMEM((1, R_COLS), F32)],
        compiler_params=_cparams(("parallel", "arbitrary")),
        name="rwkv_prep",
    )(pr, prev.reshape(b, 1, R_COLS), *consts)
    return r, dec, k2, v, kk, kka, g, bonus


def _scan_kernel(kk_ref, w_ref, k_ref, kka_ref, r_ref, v_ref, s0_ref, y_ref, s_ref, *, steps, n_slabs):
    @pl.when(pl.program_id(0) == 0)
    def _():
        s_ref[...] = s0_ref[...]

    def step(t, carry):
        kk = kk_ref[t]
        w = w_ref[t]
        k = k_ref[t]
        kka = kka_ref[t]
        r = r_ref[t]
        rw = r * w
        c1 = jnp.sum(r * kka, axis=0, keepdims=True)
        c2 = jnp.sum(r * k, axis=0, keepdims=True)
        for n in range(n_slabs):
            s = s_ref[n]
            sa = -jnp.sum(s * kk, axis=0, keepdims=True)
            yp = jnp.sum(s * rw, axis=0, keepdims=True)
            vn = v_ref[t, pl.ds(n, 1), :]
            s_ref[n] = s * w + sa * kka + vn * k
            y_ref[t, pl.ds(n, 1), :] = yp + c1 * sa + c2 * vn
        return carry

    lax.fori_loop(0, steps, step, 0)


def _wkv_scan(r, w, k, v, kk, kka, s0, tc):
    b, t, _ = v.shape
    bh = b * N_HEADS
    dup = _lane_dup(b)
    lanes = dup * bh
    ni = HEAD_DIM // dup

    def key_layout(x):
        return x.reshape(dup, b, t, N_HEADS, HEAD_DIM).transpose(2, 4, 0, 1, 3).reshape(t, HEAD_DIM, lanes)

    v_l = v.reshape(b, t, N_HEADS, ni, dup).transpose(1, 3, 4, 0, 2).reshape(t, ni, lanes)
    s_l = s0.reshape(b, N_HEADS, ni, dup, HEAD_DIM).transpose(2, 4, 3, 0, 1).reshape(ni, HEAD_DIM, lanes)
    key_spec = pl.BlockSpec((tc, HEAD_DIM, lanes), lambda c: (c, 0, 0))
    val_spec = pl.BlockSpec((tc, ni, lanes), lambda c: (c, 0, 0))
    st_spec = pl.BlockSpec((ni, HEAD_DIM, lanes), lambda c: (0, 0, 0))
    y_l, s_out = pl.pallas_call(
        functools.partial(_scan_kernel, steps=tc, n_slabs=ni),
        grid=(t // tc,),
        in_specs=[key_spec] * 5 + [val_spec, st_spec],
        out_specs=[val_spec, st_spec],
        out_shape=[jax.ShapeDtypeStruct((t, ni, lanes), F32),
                   jax.ShapeDtypeStruct((ni, HEAD_DIM, lanes), F32)],
        compiler_params=_cparams(("arbitrary",)),
        name="wkv_scan",
    )(key_layout(kk), key_layout(w), key_layout(k), key_layout(kka), key_layout(r), v_l, s_l)
    y = y_l.reshape(t, ni, dup, b, N_HEADS).transpose(3, 0, 4, 1, 2).reshape(b, t, WIDTH)
    s_new = s_out.reshape(ni, HEAD_DIM, dup, b, N_HEADS).transpose(3, 4, 0, 2, 1).reshape(
        b, N_HEADS, HEAD_DIM, HEAD_DIM)
    return y, s_new


def _cumsum_kernel(x_ref, o_ref, carry):
    @pl.when(pl.program_id(1) == 0)
    def _():
        carry[...] = jnp.zeros_like(carry)

    x = x_ref[0]
    tb = x.shape[0]
    tril = (lax.broadcasted_iota(jnp.int32, (tb, tb), 1) <= lax.broadcasted_iota(jnp.int32, (tb, tb), 0)).astype(BF16)
    c = _dot_3x(tril, x) + carry[...]
    o_ref[0] = c
    carry[...] = c[tb - 1:tb, :]


def _cumsum_time(x, tb):
    b, t, n = x.shape
    spec = pl.BlockSpec((1, tb, n), lambda bi, i: (bi, i, 0))
    return pl.pallas_call(
        _cumsum_kernel, grid=(b, t // tb), in_specs=[spec], out_specs=spec,
        out_shape=jax.ShapeDtypeStruct((b, t, n), F32),
        scratch_shapes=[pltpu.VMEM((1, n), F32)],
        compiler_params=_cparams(("parallel", "arbitrary")),
        name="cumsum_time",
    )(x)


def _bias_lanes(x, col, lane, base, own, key_side):
    hi, mid, lo = _split3(col)
    first, second = (base + 3, base) if key_side else (base, base + 3)
    out = jnp.where(own, x, 0.0)
    out = jnp.where((lane >= second) & (lane < second + 3), 1.0, out)
    out = jnp.where(lane == first, hi.astype(F32), out)
    out = jnp.where(lane == first + 1, mid.astype(F32), out)
    return jnp.where(lane == first + 2, lo.astype(F32), out)


def _fox_prompt_kernel(q_ref, k_ref, v_ref, cq_ref, ck_ref, o_ref, kaug, vb, *, tq):
    i = pl.program_id(2)
    t = k_ref.shape[1]

    @pl.when(i == 0)
    def _():
        lane_k = lax.broadcasted_iota(jnp.int32, (t, LANES), 1)
        vb[...] = v_ref[0].astype(BF16)
        k = k_ref[0]
        for hh in range(2):
            own = (lane_k >= hh * HEAD_DIM) & (lane_k < (hh + 1) * HEAD_DIM)
            kaug[hh] = _bias_lanes(k, -ck_ref[0, hh], lane_k, (1 - hh) * HEAD_DIM, own, True).astype(BF16)

    lane = lax.broadcasted_iota(jnp.int32, (tq, LANES), 1)
    q = q_ref[0] * (HEAD_DIM ** -0.5)
    qa = []
    for hh in range(2):
        own = (lane >= hh * HEAD_DIM) & (lane < (hh + 1) * HEAD_DIM)
        qa.append(_bias_lanes(q, cq_ref[0, hh], lane, (1 - hh) * HEAD_DIM, own, False).astype(BF16))
    causal = lax.broadcasted_iota(jnp.int32, (tq, tq), 1) <= lax.broadcasted_iota(jnp.int32, (tq, tq), 0)

    def block(j, carry, masked):
        start = pl.multiple_of(j * tq, tq)
        v_blk = vb[pl.ds(start, tq), :]
        out = []
        for hh in range(2):
            m, l, acc = carry[hh]
            s = _dot_nt(qa[hh], kaug[hh, pl.ds(start, tq), :])
            if masked:
                s = jnp.where(causal, s, -jnp.inf)
            m_new = jnp.maximum(m, jnp.max(s, axis=-1, keepdims=True))
            p = jnp.exp(s - m_new)
            alpha = jnp.exp(m - m_new)
            out.append((m_new, alpha * l + jnp.sum(p, axis=-1, keepdims=True),
                        alpha * acc + _dot(p.astype(BF16), v_blk)))
        return tuple(out)

    one = (jnp.full((tq, 1), -jnp.inf, F32), jnp.zeros((tq, 1), F32), jnp.zeros((tq, LANES), F32))
    carry = lax.fori_loop(0, i, lambda j, c: block(j, c, False), (one, one))
    (_, l0, a0), (_, l1, a1) = block(i, carry, True)
    o_ref[0] = jnp.where(lane < HEAD_DIM, a0 / l0, a1 / l1).astype(o_ref.dtype)


def _fox_prompt(qkv, cum, tq):
    b, t, _ = qkv.shape
    npair = WIDTH // LANES
    cum_col = cum.transpose(0, 2, 1).reshape(b, N_HEADS, t, 1)
    return pl.pallas_call(
        functools.partial(_fox_prompt_kernel, tq=tq),
        grid=(b, npair, t // tq),
        in_specs=[pl.BlockSpec((1, tq, LANES), lambda bi, p, i: (bi, i, p)),
                  pl.BlockSpec((1, t, LANES), lambda bi, p, i: (bi, 0, npair + p)),
                  pl.BlockSpec((1, t, LANES), lambda bi, p, i: (bi, 0, 2 * npair + p)),
                  pl.BlockSpec((1, 2, tq, 1), lambda bi, p, i: (bi, p, i, 0)),
                  pl.BlockSpec((1, 2, t, 1), lambda bi, p, i: (bi, p, 0, 0))],
        out_specs=pl.BlockSpec((1, tq, LANES), lambda bi, p, i: (bi, i, p)),
        out_shape=jax.ShapeDtypeStruct((b, t, WIDTH), BF16),
        scratch_shapes=[pltpu.VMEM((2, t, LANES), BF16), pltpu.VMEM((t, LANES), BF16)],
        compiler_params=_cparams(("parallel", "parallel", "arbitrary")),
        name="fox_prompt",
    )(qkv, qkv, qkv, cum_col, cum_col)


def _page_sums_kernel(lf_ref, rev_ref, tot_ref):
    n = lf_ref.shape[1]
    a = lax.broadcasted_iota(jnp.int32, (n, n), 0)
    c = lax.broadcasted_iota(jnp.int32, (n, n), 1)
    hi, mid, lo = _split3(lf_ref[...])
    later = (a > c).astype(BF16)
    rev_ref[...] = _dot(hi, later) + _dot(mid, later) + _dot(lo, later)
    every = jnp.ones((n, n), BF16)
    tot_ref[...] = _dot(hi, every) + _dot(mid, every) + _dot(lo, every)


def _page_sums(lf_rows, rows):
    n_rows, n = lf_rows.shape
    spec = pl.BlockSpec((rows, n), lambda i: (i, 0))
    return pl.pallas_call(
        _page_sums_kernel, grid=(n_rows // rows,), in_specs=[spec], out_specs=[spec, spec],
        out_shape=[jax.ShapeDtypeStruct((n_rows, n), F32)] * 2,
        compiler_params=_cparams(("parallel",)),
        name="page_sums",
    )(lf_rows)


def _fox_sample_kernel(pt_ref, q_ref, kn_ref, vn_ref, lfn_ref, *rest, pages_per_step):
    npp = pages_per_step
    k_refs = rest[:npp]
    v_refs = rest[npp:2 * npp]
    rev_refs = rest[2 * npp:3 * npp]
    tot_refs = rest[3 * npp:4 * npp]
    o_ref = rest[4 * npp]
    qrep, m_s, l_s, acc_s, suf_s, cn_s = rest[4 * npp + 1:]
    step = pl.program_id(1)
    nq = q_ref.shape[1]
    rows = N_HEADS * nq
    page = kn_ref.shape[1]
    row_head = lax.broadcasted_iota(jnp.int32, (rows, WIDTH), 0) // nq
    lane_head = lax.broadcasted_iota(jnp.int32, (rows, WIDTH), 1) // HEAD_DIM

    def rep_heads(x):
        return jnp.concatenate([jnp.broadcast_to(x[h:h + 1, :], (nq, x.shape[1])) for h in range(N_HEADS)], axis=0)

    def update(s_list, pv):
        m_old = m_s[...]
        m_new = functools.reduce(jnp.maximum, [jnp.max(s, axis=-1, keepdims=True) for s in s_list] + [m_old])
        p_list = [jnp.exp(s - m_new) for s in s_list]
        alpha = jnp.exp(m_old - m_new)
        l_s[...] = alpha * l_s[...] + functools.reduce(jnp.add, [jnp.sum(p, axis=-1, keepdims=True) for p in p_list])
        acc_s[...] = alpha * acc_s[...] + functools.reduce(jnp.add, [pv(u, p.astype(BF16)) for u, p in enumerate(p_list)])
        m_s[...] = m_new

    @pl.when(step == 0)
    def _():
        q = q_ref[0] * (HEAD_DIM ** -0.5)
        qrep[...] = jnp.where(row_head == lane_head, jnp.concatenate([q] * N_HEADS, axis=0), 0.0).astype(BF16)
        key_i = lax.broadcasted_iota(jnp.int32, (page, page), 0)
        key_j = lax.broadcasted_iota(jnp.int32, (page, page), 1)
        cn_row = _dot_x3(rep_heads(lfn_ref[0]), (key_i <= key_j).astype(BF16))
        rq = lax.broadcasted_iota(jnp.int32, (rows, page), 0) % nq
        kc = lax.broadcasted_iota(jnp.int32, (rows, page), 1)
        cn_col = jnp.sum(jnp.where(kc == rq, cn_row, 0.0), axis=-1, keepdims=True)
        cn_s[...] = cn_col
        suf_s[...] = jnp.zeros_like(suf_s)
        m_s[...] = jnp.full_like(m_s, -jnp.inf)
        l_s[...] = jnp.zeros_like(l_s)
        acc_s[...] = jnp.zeros_like(acc_s)
        s = _dot_nt(qrep[...], kn_ref[0].astype(BF16)) + cn_col - cn_row
        vn = vn_ref[0].astype(BF16)
        update([jnp.where(kc <= rq, s, -jnp.inf)], lambda u, p: _dot(p, vn))

    q_all = qrep[...]
    cn = cn_s[...]
    suf = suf_s[...]
    s_list = []
    for u in range(npp):
        kt = k_refs[u][0].reshape(WIDTH, page).astype(BF16)
        s_list.append(_dot(q_all, kt) + ((cn + suf) + rep_heads(rev_refs[u][0])))
        suf = suf + rep_heads(tot_refs[u][0])
    update(s_list, lambda u, p: _dot_nt(p, v_refs[u][0].reshape(WIDTH, page).astype(BF16)))
    suf_s[...] = suf

    @pl.when(step == pl.num_programs(1) - 1)
    def _():
        o_sel = jnp.where(row_head == lane_head, acc_s[...] / l_s[...], 0.0)
        out = o_sel[0:nq]
        for h in range(1, N_HEADS):
            out = out + o_sel[h * nq:(h + 1) * nq]
        o_ref[0] = out.astype(o_ref.dtype)


def _fox_sample(q, k_new, v_new, logf_new, cache_k, cache_v, cache_logf, page_table, pages_per_step):
    b, tn, _ = q.shape
    n_pool, page = cache_k.shape[:2]
    n_pages = page_table.shape[1]
    npp = pages_per_step
    assert n_pages % npp == 0 and tn <= page
    rows = N_HEADS * tn
    ck = cache_k.transpose(0, 2, 3, 1)
    cv = cache_v.transpose(0, 2, 3, 1)
    clf = cache_logf.transpose(0, 2, 1).reshape(n_pool * N_HEADS, page)
    sum_rows = 2048 if clf.shape[0] % 2048 == 0 else clf.shape[0]
    rev, tot = (a.reshape(n_pool, N_HEADS, page) for a in _page_sums(clf, sum_rows))
    pad_rows = lambda x: jnp.pad(x, ((0, 0), (0, page - tn), (0, 0)))
    lfn = jnp.pad(logf_new.transpose(0, 2, 1), ((0, 0), (0, 0), (0, page - tn)))

    def page_map(u, nd):
        return lambda bi, s, pt: (pt[bi, n_pages - 1 - (s * npp + u)],) + (0,) * nd

    tok_spec = lambda r: pl.BlockSpec((1, r, WIDTH), lambda bi, s, pt: (bi, 0, 0))
    in_specs = ([tok_spec(tn), tok_spec(page), tok_spec(page),
                 pl.BlockSpec((1, N_HEADS, page), lambda bi, s, pt: (bi, 0, 0))]
                + [pl.BlockSpec((1, N_HEADS, HEAD_DIM, page), page_map(u, 3)) for u in range(npp)] * 2
                + [pl.BlockSpec((1, N_HEADS, page), page_map(u, 2)) for u in range(npp)] * 2)
    grid_spec = pltpu.PrefetchScalarGridSpec(
        num_scalar_prefetch=1,
        grid=(b, n_pages // npp),
        in_specs=in_specs,
        out_specs=tok_spec(tn),
        scratch_shapes=[pltpu.VMEM((rows, WIDTH), BF16), pltpu.VMEM((rows, 1), F32), pltpu.VMEM((rows, 1), F32),
                        pltpu.VMEM((rows, WIDTH), F32), pltpu.VMEM((rows, page), F32), pltpu.VMEM((rows, 1), F32)])
    return pl.pallas_call(
        functools.partial(_fox_sample_kernel, pages_per_step=npp),
        grid_spec=grid_spec,
        out_shape=jax.ShapeDtypeStruct((b, tn, WIDTH), BF16),
        compiler_params=_cparams(("parallel", "arbitrary")),
        name="fox_sample",
    )(page_table, q, pad_rows(k_new), pad_rows(v_new), lfn, *([ck] * npp), *([cv] * npp), *([rev] * npp),
      *([tot] * npp))


def _merge_kernel(y_ref, bonus_ref, g_ref, yf_ref, gates_ref, x_ref, g1_ref, sh2_ref, sc2_ref,
                  lnw_ref, lnb_ref, seg_ref, wr_ref, wf_ref, wo_ref, n2_ref, wrt_ref,
                  x1_ref, h2_ref, lg_ref):
    seg = seg_ref[...]
    y = y_ref[0]
    mu = _dot_x3(y, seg) * (1.0 / HEAD_DIM)
    d = y - mu
    var = _dot_x3(d * d, seg) * (1.0 / HEAD_DIM)
    yn = d * lax.rsqrt(var + LNX_EPS) * lnw_ref[...] + lnb_ref[...]
    yr = ((yn + bonus_ref[0]) * g_ref[0]).astype(BF16)
    gates = gates_ref[0].astype(F32)
    d_model = x_ref.shape[2]
    merged = gates[:, :d_model] * _dot(yr, wr_ref[...]) + gates[:, d_model:] * _dot(yf_ref[0], wf_ref[...])
    x1 = x_ref[0] + g1_ref[0] * _dot(merged.astype(BF16), wo_ref[...])
    x1_ref[0] = x1
    h2 = _rmsnorm(x1, n2_ref[...]) * (1.0 + sc2_ref[0]) + sh2_ref[0]
    h2_ref[0] = h2
    hh, hl = _split2(h2)
    wrt = wrt_ref[...]
    lg_ref[0] = _dot(hh, wrt[0]) + (_dot(hh, wrt[1]) + _dot(hl, wrt[0]))


def _merge(y, bonus, g, yf, gates, x, g1, sh2, sc2, p, w_router_t, tm):
    b, t, d = x.shape
    per_tok = g1.shape[1] != 1
    mrows = tm if per_tok else 1
    mod_map = (lambda bi, i: (bi, i, 0)) if per_tok else (lambda bi, i: (bi, 0, 0))
    tok = lambda n: pl.BlockSpec((1, tm, n), lambda bi, i: (bi, i, 0))
    mod = pl.BlockSpec((1, mrows, d), mod_map)
    row = lambda a: a.reshape(1, -1)
    consts = [row(p["lnx_w"]), row(p["lnx_b"]), _seg_ones(WIDTH, HEAD_DIM), p["w_br_r"].astype(BF16),
              p["w_br_f"].astype(BF16), p["w_out"].astype(BF16), row(p["norm2_g"]), w_router_t]
    const_specs = [pl.BlockSpec(c.shape, (lambda bi, i: (0, 0)) if c.ndim == 2 else (lambda bi, i: (0, 0, 0)))
                   for c in consts]
    return pl.pallas_call(
        _merge_kernel,
        grid=(b, t // tm),
        in_specs=[tok(WIDTH), tok(WIDTH), tok(WIDTH), tok(WIDTH), tok(2 * d), tok(d), mod, mod, mod] + const_specs,
        out_specs=[tok(d), tok(d), tok(LANES)],
        out_shape=[jax.ShapeDtypeStruct((b, t, d), F32), jax.ShapeDtypeStruct((b, t, d), F32),
                   jax.ShapeDtypeStruct((b, t, LANES), F32)],
        compiler_params=_cparams(("parallel", "parallel")),
        name="merge",
    )(y, bonus, g, yf, gates, x, g1, sh2, sc2, *consts)


def _route_kernel(lg_ref, bias_ref, row_ref, rowt_ref, wt_ref, cnt_ref):
    lt = lg_ref[...].T
    tm = lt.shape[1]
    score = _sigmoid(lt[:N_EXPERTS])
    biased = score + bias_ref[...][:N_EXPERTS]
    slab = [biased[k * N_GROUPS:(k + 1) * N_GROUPS] for k in range(GROUP_SIZE)]
    neg = jnp.full((N_GROUPS, tm), -jnp.inf, F32)
    m1 = functools.reduce(jnp.maximum, slab)
    taken = jnp.zeros((N_GROUPS, tm), jnp.bool_)
    m2 = neg
    for k in range(GROUP_SIZE):
        is_first = (slab[k] == m1) & jnp.logical_not(taken)
        taken = taken | is_first
        m2 = jnp.maximum(m2, jnp.where(is_first, neg, slab[k]))
    gs = m1 + m2
    g_iota = lax.broadcasted_iota(jnp.int32, (N_GROUPS, tm), 0)
    cnt = jnp.zeros((N_GROUPS, tm), jnp.int32)
    for g2 in range(N_GROUPS):
        other = gs[g2:g2 + 1, :]
        beats = (other > gs) | ((g_iota > g2) & (other == gs))
        cnt = cnt + beats.astype(jnp.int32)
    g_sel = cnt < TOPK_GROUPS
    cand = [jnp.where(g_sel, slab[k], neg) for k in range(GROUP_SIZE)]
    rank = [jnp.zeros((N_GROUPS, tm), jnp.int32) for _ in range(GROUP_SIZE)]
    for k2 in range(GROUP_SIZE):
        for g2 in range(N_GROUPS):
            other = cand[k2][g2:g2 + 1, :]
            for k in range(GROUP_SIZE):
                first = (g_iota >= g2) if k2 < k else (g_iota > g2)
                beats = (other > cand[k]) | (first & (other == cand[k]))
                rank[k] = rank[k] + beats.astype(jnp.int32)
    sel = [rank[k] < TOP_K for k in range(GROUP_SIZE)]
    sc = [score[k * N_GROUPS:(k + 1) * N_GROUPS] for k in range(GROUP_SIZE)]
    picked = [jnp.where(sel[k], sc[k], 0.0) for k in range(GROUP_SIZE)]
    total = jnp.sum(functools.reduce(jnp.add, picked), axis=0, keepdims=True)
    gate = jnp.concatenate([pk / total * ROUTED_SCALE for pk in picked], axis=0)
    chosen = jnp.concatenate([s.astype(F32) for s in sel], axis=0)
    chosen_b = chosen.astype(BF16)
    ri = lax.broadcasted_iota(jnp.int32, (N_EXPERTS, N_EXPERTS), 0)
    ci = lax.broadcasted_iota(jnp.int32, (N_EXPERTS, N_EXPERTS), 1)
    lower = (ci < ri).astype(BF16)
    ordinal = _dot(lower, chosen_b)
    ta = lax.broadcasted_iota(jnp.int32, (tm, tm), 0)
    tc = lax.broadcasted_iota(jnp.int32, (tm, tm), 1)
    rank_tok = _dot(chosen_b, (ta < tc).astype(BF16))
    count = jnp.sum(chosen, axis=1, keepdims=True)
    runs = jnp.broadcast_to(jnp.floor((count + (RUN - 1)) * (1.0 / RUN)), (N_EXPERTS, LANES))
    cnt_ref[0] = runs
    start = _dot(lower, runs.astype(BF16))[:, 0:1] * float(RUN)
    tile_row = start + rank_tok
    r_rows, w_rows = [], []
    for n in range(TOP_K):
        hit = (chosen > 0.0) & (ordinal == float(n))
        r_rows.append(jnp.sum(jnp.where(hit, tile_row, 0.0), axis=0, keepdims=True))
        w_rows.append(jnp.sum(jnp.where(hit, gate, 0.0), axis=0, keepdims=True))
    zrow = jnp.zeros((1, tm), F32)
    pad8 = lambda rows: jnp.concatenate(rows + [zrow] * (8 - TOP_K), axis=0)
    zpad = jnp.zeros((LANES - 8, tm), F32)
    row_ref[...] = pad8(r_rows).astype(jnp.int32)
    rowt_ref[...] = jnp.concatenate([pad8(r_rows), zpad], axis=0).T.astype(jnp.int32)
    wt_ref[...] = jnp.concatenate([pad8(w_rows), zpad], axis=0).T


def _route(logits, e_bias_perm, tm):
    m = logits.shape[0]
    bias = jnp.pad(e_bias_perm, (0, LANES - N_EXPERTS)).reshape(LANES, 1)
    tok_spec = pl.BlockSpec((tm, LANES), lambda i: (i, 0))
    return pl.pallas_call(
        _route_kernel,
        grid=(m // tm,),
        in_specs=[tok_spec, pl.BlockSpec((LANES, 1), lambda i: (0, 0))],
        out_specs=[pl.BlockSpec((8, tm), lambda i: (0, i)), tok_spec, tok_spec,
                   pl.BlockSpec((1, N_EXPERTS, LANES), lambda i: (i, 0, 0))],
        out_shape=[jax.ShapeDtypeStruct((8, m), jnp.int32), jax.ShapeDtypeStruct((m, LANES), jnp.int32),
                   jax.ShapeDtypeStruct((m, LANES), F32), jax.ShapeDtypeStruct((m // tm, N_EXPERTS, LANES), F32)],
        compiler_params=_cparams(("parallel",)),
        name="route",
    )(logits, bias)


def _sorted_capacity(tm):
    rows = TOP_K * tm + N_EXPERTS * (RUN - 1)
    return -(-rows // SORT_BLOCK) * SORT_BLOCK


def _run_pieces(i, nrun_ref, src_ref, dst_ref, piece):
    def per_expert(e, total):
        j = i * N_EXPERTS + e
        n, s0, d0 = nrun_ref[j], src_ref[j], dst_ref[j]

        def one(c, carry):
            piece(s0 + c, d0 + c).start()
            return carry

        lax.fori_loop(0, n, one, 0)
        return total + n

    return lax.fori_loop(0, N_EXPERTS, per_expert, 0)


def _sort_kernel(nrun_ref, src_ref, dst_ref, nblk_ref, row_ref, h_ref, xs_in_ref, xs_ref, buf, sem):
    del xs_in_ref
    i = pl.program_id(0)
    tm = h_ref.shape[0]
    hb = h_ref[...].astype(BF16)
    rows = row_ref[...]
    riota = lax.broadcasted_iota(jnp.int32, (SORT_BLOCK, tm), 0)

    def block(b, carry):
        r0 = pl.multiple_of(b * SORT_BLOCK, SORT_BLOCK)
        hit = riota + r0 == rows[0:1, :]
        for n in range(1, TOP_K):
            hit = hit | (riota + r0 == rows[n:n + 1, :])
        buf[pl.ds(r0, SORT_BLOCK), :] = _dot(hit.astype(BF16), hb).astype(BF16)
        return carry

    lax.fori_loop(0, nblk_ref[i], block, 0)

    def piece(s, d):
        return pltpu.make_async_copy(buf.at[pl.ds(pl.multiple_of(s * RUN, RUN), RUN), :],
                                     xs_ref.at[pl.ds(pl.multiple_of(d * RUN, RUN), RUN), :], sem)

    started = _run_pieces(i, nrun_ref, src_ref, dst_ref, piece)

    def drain(c, carry):
        piece(0, 0).wait()
        return carry

    lax.fori_loop(0, started, drain, 0)


def _sort_rows(h2, row, tables, n_rows, tm):
    m, d = h2.shape
    grid_spec = pltpu.PrefetchScalarGridSpec(
        num_scalar_prefetch=4,
        grid=(m // tm,),
        in_specs=[pl.BlockSpec((8, tm), lambda i, *_: (0, i)), pl.BlockSpec((tm, d), lambda i, *_: (i, 0)),
                  pl.BlockSpec(memory_space=pl.ANY)],
        out_specs=pl.BlockSpec(memory_space=pl.ANY),
        scratch_shapes=[pltpu.VMEM((_sorted_capacity(tm), d), BF16), pltpu.SemaphoreType.DMA(())])
    return pl.pallas_call(
        _sort_kernel,
        grid_spec=grid_spec,
        out_shape=jax.ShapeDtypeStruct((n_rows, d), BF16),
        input_output_aliases={6: 0},
        compiler_params=_cparams(("arbitrary",)),
        name="sort_rows",
    )(*tables, row, h2, jnp.zeros((n_rows, d), BF16))


def _expert_kernel(te_ref, nu_ref, x_ref, wg_ref, wu_ref, wd_ref, o_ref, wgb, wub, wdb):
    i = pl.program_id(0)
    changed = (i == 0) | (te_ref[i] != te_ref[jnp.maximum(i - 1, 0)])

    @pl.when(changed)
    def _():
        wgb[...] = wg_ref[0].astype(BF16)
        wub[...] = wu_ref[0].astype(BF16)
        wdb[...] = wd_ref[0].astype(BF16)

    @pl.when(i < nu_ref[0])
    def _():
        x = x_ref[...]
        act = _silu(_dot(x, wgb[...])) * _dot(x, wub[...])
        o_ref[...] = _dot(act.astype(BF16), wdb[...]).astype(o_ref.dtype)

    @pl.when(i >= nu_ref[0])
    def _():
        o_ref[...] = jnp.zeros_like(o_ref)


def _expert_tiles(xs, tile_expert, n_used, wg, wu, wd, te):
    n_rows, d = xs.shape
    ff = wg.shape[2]
    last = lambda i, nu: jnp.minimum(i, nu[0] - 1)
    grid_spec = pltpu.PrefetchScalarGridSpec(
        num_scalar_prefetch=2,
        grid=(n_rows // te,),
        in_specs=[pl.BlockSpec((te, d), lambda i, tx, nu: (last(i, nu), 0)),
                  pl.BlockSpec((1, d, ff), lambda i, tx, nu: (tx[i], 0, 0)),
                  pl.BlockSpec((1, d, ff), lambda i, tx, nu: (tx[i], 0, 0)),
                  pl.BlockSpec((1, ff, d), lambda i, tx, nu: (tx[i], 0, 0))],
        out_specs=pl.BlockSpec((te, d), lambda i, tx, nu: (i, 0)),
        scratch_shapes=[pltpu.VMEM((d, ff), BF16), pltpu.VMEM((d, ff), BF16), pltpu.VMEM((ff, d), BF16)])
    return pl.pallas_call(
        _expert_kernel,
        grid_spec=grid_spec,
        out_shape=jax.ShapeDtypeStruct((n_rows, d), BF16),
        compiler_params=_cparams(("arbitrary",)),
        name="expert_tiles",
    )(tile_expert, n_used, xs, wg, wu, wd)


def _combine_kernel(nrun_ref, src_ref, dst_ref, nblk_ref, rowt_ref, wt_ref, h_ref, x1_ref, g2_ref, nf_ref,
                    wsg_ref, wsu_ref, wsd_ref, os_ref, y_ref, buf, rowb, wb, sem):
    i = pl.program_id(0) * pl.num_programs(1) + pl.program_id(1)
    tm = h_ref.shape[1]
    nblk = nblk_ref[i]
    tail = pl.multiple_of((nblk - 1) * SORT_BLOCK, SORT_BLOCK)
    buf[pl.ds(tail, SORT_BLOCK), :] = jnp.zeros((SORT_BLOCK, buf.shape[1]), BF16)

    def piece(s, d):
        return pltpu.make_async_copy(os_ref.at[pl.ds(pl.multiple_of(d * RUN, RUN), RUN), :],
                                     buf.at[pl.ds(pl.multiple_of(s * RUN, RUN), RUN), :], sem)

    started = _run_pieces(i, nrun_ref, src_ref, dst_ref, piece)
    h = h_ref[0].astype(BF16)
    shared = _dot((_silu(_dot(h, wsg_ref[...])) * _dot(h, wsu_ref[...])).astype(BF16), wsd_ref[...])

    def drain(c, carry):
        piece(0, 0).wait()
        return carry

    lax.fori_loop(0, started, drain, 0)
    w = wt_ref[0]
    rt = rowt_ref[0]
    for n in range(TOP_K):
        rowb[n] = jnp.broadcast_to(rt[:, n:n + 1], (tm, SORT_BLOCK))
        wb[n] = jnp.broadcast_to(w[:, n:n + 1], (tm, SORT_BLOCK))
    ciota = lax.broadcasted_iota(jnp.int32, (tm, SORT_BLOCK), 1)

    def block(b, acc):
        r0 = pl.multiple_of(b * SORT_BLOCK, SORT_BLOCK)
        c = jnp.where(ciota == rowb[0] - r0, wb[0], 0.0)
        for n in range(1, TOP_K):
            c = c + jnp.where(ciota == rowb[n] - r0, wb[n], 0.0)
        return acc + _dot(c.astype(BF16), buf[pl.ds(r0, SORT_BLOCK), :])

    routed = lax.fori_loop(0, nblk, block, jnp.zeros((tm, buf.shape[1]), F32))
    x2 = x1_ref[0] + g2_ref[0] * (routed + shared)
    y_ref[0] = _rmsnorm(x2, nf_ref[...])


def _combine(os, rowt, wt, tables, h2, x1, g2, normf_g, wsg, wsu, wsd, tm):
    b, t, d = x1.shape
    nt = t // tm
    per_tok = g2.shape[1] != 1
    mrows = tm if per_tok else 1
    mod_map = (lambda bi, i, *_: (bi, i, 0)) if per_tok else (lambda bi, i, *_: (bi, 0, 0))
    tok = lambda n: pl.BlockSpec((1, tm, n), lambda bi, i, *_: (bi, i, 0))
    const = lambda a: pl.BlockSpec(a.shape, lambda bi, i, *_: (0, 0))
    grid_spec = pltpu.PrefetchScalarGridSpec(
        num_scalar_prefetch=4,
        grid=(b, nt),
        in_specs=[tok(LANES), tok(LANES), tok(d), tok(d), pl.BlockSpec((1, mrows, d), mod_map),
                  pl.BlockSpec((1, d), lambda bi, i, *_: (0, 0)), const(wsg), const(wsu), const(wsd),
                  pl.BlockSpec(memory_space=pl.ANY)],
        out_specs=tok(d),
        scratch_shapes=[pltpu.VMEM((_sorted_capacity(tm), d), BF16), pltpu.VMEM((TOP_K, tm, SORT_BLOCK), jnp.int32),
                        pltpu.VMEM((TOP_K, tm, SORT_BLOCK), F32), pltpu.SemaphoreType.DMA(())])
    return pl.pallas_call(
        _combine_kernel,
        grid_spec=grid_spec,
        out_shape=jax.ShapeDtypeStruct((b, t, d), F32),
        compiler_params=_cparams(("arbitrary", "arbitrary")),
        name="combine",
    )(*tables, rowt.reshape(b, t, LANES), wt.reshape(b, t, LANES), h2, x1, g2, normf_g.reshape(1, d), wsg, wsu, wsd, os)


def _moe(h2, logits, x1, g2, p, w, cfg):
    bx, tx, d = x1.shape
    m = bx * tx
    te, tm = cfg["te"], cfg["tm_route"]
    row, rowt, wt, cnt = _route(logits.reshape(m, LANES), w["e_bias_perm"], tm)
    runs = cnt[:, :, 0].astype(jnp.int32)
    src = jnp.cumsum(runs, axis=1) - runs
    nblk = (jnp.sum(runs, axis=1) * RUN + SORT_BLOCK - 1) // SORT_BLOCK
    per_expert = jnp.sum(runs, axis=0)
    tiles_e = (per_expert * RUN + te - 1) // te
    ends = jnp.cumsum(tiles_e)
    dst = ((ends - tiles_e) * (te // RUN))[None, :] + jnp.cumsum(runs, axis=0) - runs
    n_tiles = (m * TOP_K + (m // tm) * N_EXPERTS * (RUN - 1)) // te + N_EXPERTS
    n_used = ends[-1:].astype(jnp.int32)
    tile_ids = jnp.minimum(jnp.arange(n_tiles, dtype=jnp.int32), n_used[0] - 1)
    tile_row = jnp.minimum(jnp.sum(ends[None, :] <= tile_ids[:, None], axis=1), N_EXPERTS - 1)
    tile_expert = ((tile_row % N_GROUPS) * GROUP_SIZE + tile_row // N_GROUPS).astype(jnp.int32)
    flat = lambda a: a.reshape(-1).astype(jnp.int32)
    tables = (flat(runs), flat(src), flat(dst), flat(nblk))
    xs = _sort_rows(h2.reshape(m, d), row, tables, n_tiles * te, tm)
    os = _expert_tiles(xs, tile_expert, n_used, p["w_exp_gate"], p["w_exp_up"], p["w_exp_down"], te)
    return _combine(os, rowt, wt, tables, h2, x1, g2, p["normf_g"], w["sh_gate"], w["sh_up"], w["sh_down"], tm)


def _layer(x, mod, shift_prev, wkv0, attend, p, w, cfg):
    b, t, d = x.shape
    bx, tx = cfg["rows"]
    tm = cfg["tm"]
    xr = x.reshape(bx, tx, d)
    if bx == b:
        part = lambda i: mod[:, i:i + 1, :]
    else:
        part = lambda i: jnp.repeat(mod[:, i, :], t, axis=0).reshape(bx, tx, d)
    sh1, sc1, g1, sh2, sc2, g2 = (part(i) for i in range(6))
    h_rows = 1 if bx == b else tx
    proj = functools.partial(_in_proj, xr, sh1, sc1, p["norm1_g"], tm=tm, h_rows=h_rows)
    pr, h_keep = proj(w["in_r"], None, tn=cfg["tn_r"], epilogue="none", out_dtype=F32)
    qkv, _ = proj(w["in_qkv"], None, tn=512, epilogue="none", out_dtype=F32)
    logf_pad, _ = proj(w["in_f"], w["b_f_pad"], tn=LANES, epilogue="log_sigmoid", out_dtype=F32)
    gates, _ = proj(w["in_g"], None, tn=512, epilogue="sigmoid", out_dtype=BF16)
    shift_new = h_keep[:, 0, :] if bx == b else h_keep.reshape(b, t, d)[:, -1, :]
    logf = logf_pad.reshape(b, t, LANES)[:, :, :N_HEADS]
    qkv = qkv.reshape(b, t, 3 * WIDTH)

    prev = _dense(shift_prev, w["in_r_f32"], jnp.zeros((R_COLS,), F32), act=False, tn=R_COLS // 2)
    r, dec, k2, v, kk, kka, g, bonus = _rwkv_prep(pr.reshape(b, t, R_COLS), prev, p, cfg["tt"])
    y_scan, wkv_new = _wkv_scan(r, dec, k2, v, kk, kka, wkv0, cfg["tc"])

    y_f = attend(qkv, logf)

    rs = lambda a: a.reshape(bx, tx, a.shape[-1])
    x1, h2, logits = _merge(rs(y_scan), rs(bonus), rs(g), rs(y_f), gates, xr, g1, sh2, sc2, p, w["router_t"], tm)
    y = _moe(h2, logits, x1, g2, p, w, cfg)
    k_out = qkv[:, :, WIDTH:2 * WIDTH].reshape(b, t, N_HEADS, HEAD_DIM)
    v_out = qkv[:, :, 2 * WIDTH:].reshape(b, t, N_HEADS, HEAD_DIM)
    return y.reshape(b, t, d), k_out, v_out, logf, wkv_new, shift_new


def kernel(x_prompt, x_sample, c_prompt, c_sample, cache_k, cache_v, cache_logf, page_table, state_wkv, state_shift, w_ada, b_ada, norm1_g, w_in, mu_shift, w0, w_w2, a0, w_a2, w_g2, k_k, k_a, r_k, lnx_w, lnx_b, b_f, w_br_r, w_br_f, w_out, norm2_g, w_router, e_bias, w_exp_gate, w_exp_up, w_exp_down, w_sh_gate, w_sh_up, w_sh_down, normf_g):
    p = dict(norm1_g=norm1_g, mu_shift=mu_shift, w0=w0, w_w2=w_w2, a0=a0, w_a2=w_a2, w_g2=w_g2, k_k=k_k, k_a=k_a,
             r_k=r_k, lnx_w=lnx_w, lnx_b=lnx_b, w_br_r=w_br_r, w_br_f=w_br_f, w_out=w_out, norm2_g=norm2_g,
             normf_g=normf_g, w_exp_gate=w_exp_gate, w_exp_up=w_exp_up, w_exp_down=w_exp_down)
    bp, tp, d = x_prompt.shape
    bs, ts, _ = x_sample.shape
    off_f = R_COLS + 3 * WIDTH
    off_g = off_f + N_HEADS
    perm = lambda a: a.reshape(a.shape[:-1] + (N_GROUPS, GROUP_SIZE)).swapaxes(-1, -2).reshape(a.shape)
    router = jnp.pad(perm(w_router), ((0, 0), (0, LANES - N_EXPERTS)))
    r_hi = router.astype(BF16)
    w = dict(
        in_r=w_in[:, :R_COLS].astype(BF16), in_r_f32=w_in[:, :R_COLS],
        in_qkv=w_in[:, R_COLS:off_f].astype(BF16),
        in_f=jnp.pad(w_in[:, off_f:off_g], ((0, 0), (0, LANES - N_HEADS))).astype(BF16),
        in_g=w_in[:, off_g:].astype(BF16),
        b_f_pad=jnp.pad(b_f, (0, LANES - N_HEADS)).reshape(1, LANES),
        router_t=jnp.stack([r_hi, (router - r_hi.astype(F32)).astype(BF16)]),
        e_bias_perm=perm(e_bias),
        sh_gate=w_sh_gate.astype(BF16), sh_up=w_sh_up.astype(BF16), sh_down=w_sh_down.astype(BF16),
    )
    mod = _dense(jnp.concatenate([c_prompt, c_sample], axis=0), w_ada, b_ada, act=True).reshape(bp + bs, 6, d)

    def attend_prompt(qkv, logf):
        lf = jnp.pad(logf, ((0, 0), (0, 0), (0, LANES - N_HEADS)))
        cum = _cumsum_time(lf, 256)[:, :, :N_HEADS]
        return _fox_prompt(qkv, cum, min(512, tp))

    def attend_sample(qkv, logf):
        return _fox_sample(qkv[:, :, :WIDTH], qkv[:, :, WIDTH:2 * WIDTH], qkv[:, :, 2 * WIDTH:], logf,
                           cache_k, cache_v, cache_logf, page_table, 8)

    cfg_p = dict(rows=(bp, tp), tm=min(512, tp), tn_r=R_COLS // 2, tt=min(256, tp), tc=16,
                 tm_route=min(512, tp), te=512)
    cfg_s = dict(rows=(1, bs * ts), tm=bs * ts, tn_r=R_COLS // 2, tt=ts, tc=ts, tm_route=bs * ts, te=256)
    yp, kp, vp, lfp, wkvp, shp = _layer(x_prompt, mod[:bp], jnp.zeros((bp, d), F32),
                                        jnp.zeros((bp, N_HEADS, HEAD_DIM, HEAD_DIM), F32), attend_prompt, p, w, cfg_p)
    ys, ks, vs, lfs, wkvs, shs = _layer(x_sample, mod[bp:], state_shift, state_wkv, attend_sample, p, w, cfg_s)
    return (yp, ys, kp, vp, lfp, wkvp, shp, ks, vs, lfs, wkvs, shs)
```

```python
import functools

import jax
import jax.numpy as jnp
from jax import lax
from jax.experimental import pallas as pl
from jax.experimental.pallas import tpu as pltpu

F32 = jnp.float32
BF16 = jnp.bfloat16

HEAD_DIM = 64
N_HEADS = 8
WIDTH = N_HEADS * HEAD_DIM
DECAY_LORA = 64
AAA_LORA = 64
GATE_LORA = 128
R_COLS = 3 * WIDTH + DECAY_LORA + AAA_LORA + GATE_LORA
LNX_EPS = 64e-5
NORM_EPS = 1e-6
N_EXPERTS = 64
N_GROUPS = 8
GROUP_SIZE = N_EXPERTS // N_GROUPS
TOPK_GROUPS = 4
TOP_K = 6
ROUTED_SCALE = 2.5
LANES = 128
RUN = 16
SORT_BLOCK = 256
VMEM_LIMIT = 56 * 1024 * 1024


def _cparams(sem):
    return pltpu.CompilerParams(dimension_semantics=sem, vmem_limit_bytes=VMEM_LIMIT)


def _dot(a, b):
    return jnp.dot(a, b, preferred_element_type=F32)


def _dot_nt(a, b):
    return lax.dot_general(a, b, (((1,), (1,)), ((), ())), preferred_element_type=F32)


def _split2(x):
    hi = x.astype(BF16)
    lo = (x - hi.astype(F32)).astype(BF16)
    return hi, lo


def _split3(x):
    hi = x.astype(BF16)
    r = x - hi.astype(F32)
    mid = r.astype(BF16)
    lo = (r - mid.astype(F32)).astype(BF16)
    return hi, mid, lo


def _dot_x3(x, m):
    hi, mid, lo = _split3(x)
    return _dot(hi, m) + _dot(mid, m) + _dot(lo, m)


def _dot_3x(m, x):
    hi, mid, lo = _split3(x)
    return _dot(m, hi) + _dot(m, mid) + _dot(m, lo)


def _dot_hp(x, w):
    xh, xm, xl = _split3(x)
    wh, wl = _split2(w)
    return _dot(xh, wh) + (_dot(xh, wl) + _dot(xm, wh)) + (_dot(xm, wl) + _dot(xl, wh))


def _sigmoid(x):
    return 1.0 / (1.0 + jnp.exp(-x))


def _softplus(x):
    return jnp.maximum(x, 0.0) + jnp.log1p(jnp.exp(-jnp.abs(x)))


def _silu(x):
    return x * _sigmoid(x)


def _rmsnorm(x, g):
    return x * lax.rsqrt(jnp.mean(x * x, axis=-1, keepdims=True) + NORM_EPS) * g


def _dense_kernel(x_ref, w_ref, b_ref, o_ref, *, act):
    x = x_ref[...]
    if act:
        x = _silu(x)
    o_ref[...] = _dot_hp(x, w_ref[...]) + b_ref[...]


def _dense(x, w, b, act, tn=512):
    m, k = x.shape
    n = w.shape[1]
    assert n % tn == 0
    return pl.pallas_call(
        functools.partial(_dense_kernel, act=act),
        grid=(n // tn,),
        in_specs=[pl.BlockSpec((m, k), lambda j: (0, 0)),
                  pl.BlockSpec((k, tn), lambda j: (0, j)),
                  pl.BlockSpec((1, tn), lambda j: (0, j))],
        out_specs=pl.BlockSpec((m, tn), lambda j: (0, j)),
        out_shape=jax.ShapeDtypeStruct((m, n), F32),
        compiler_params=_cparams(("parallel",)),
        name="dense",
    )(x, w, b.reshape(1, n))


def _in_proj_kernel(x_ref, sh_ref, sc_ref, g_ref, w_ref, b_ref, o_ref, h_ref, h_scr, *, epilogue, h_rows):
    @pl.when(pl.program_id(2) == 0)
    def _():
        h = _rmsnorm(x_ref[0], g_ref[...]) * (1.0 + sc_ref[0]) + sh_ref[0]
        h_scr[...] = h.astype(BF16)
        h_ref[0] = h[h.shape[0] - h_rows:, :]

    acc = _dot(h_scr[...], w_ref[...])
    if epilogue == "sigmoid":
        acc = _sigmoid(acc)
    elif epilogue == "log_sigmoid":
        acc = -_softplus(-(acc + b_ref[...]))
    o_ref[0] = acc.astype(o_ref.dtype)


def _in_proj(x, sh, sc, g, w, bias, *, tm, tn, epilogue, out_dtype, h_rows):
    b, t, d = x.shape
    n = w.shape[1]
    assert t % tm == 0 and n % tn == 0
    per_tok = sh.shape[1] != 1
    mrows = tm if per_tok else 1
    mod_map = (lambda bi, i, j: (bi, i, 0)) if per_tok else (lambda bi, i, j: (bi, 0, 0))
    if bias is None:
        bias = jnp.zeros((1, n), F32)
    out, h = pl.pallas_call(
        functools.partial(_in_proj_kernel, epilogue=epilogue, h_rows=h_rows),
        grid=(b, t // tm, n // tn),
        in_specs=[pl.BlockSpec((1, tm, d), lambda bi, i, j: (bi, i, 0)),
                  pl.BlockSpec((1, mrows, d), mod_map),
                  pl.BlockSpec((1, mrows, d), mod_map),
                  pl.BlockSpec((1, d), lambda bi, i, j: (0, 0)),
                  pl.BlockSpec((d, tn), lambda bi, i, j: (0, j)),
                  pl.BlockSpec((1, tn), lambda bi, i, j: (0, j))],
        out_specs=[pl.BlockSpec((1, tm, tn), lambda bi, i, j: (bi, i, j)),
                   pl.BlockSpec((1, h_rows, d), lambda bi, i, j: (bi, 0, 0))],
        out_shape=[jax.ShapeDtypeStruct((b, t, n), out_dtype),
                   jax.ShapeDtypeStruct((b, h_rows, d), F32)],
        scratch_shapes=[pltpu.VMEM((tm, d), BF16)],
        compiler_params=_cparams(("parallel", "arbitrary", "arbitrary")),
        name="in_proj_" + epilogue,
    )(x, sh, sc, g.reshape(1, d), w, bias)
    return out, h


def _prep_kernel(pr_ref, prev_ref, mu_ref, w0_ref, ww2_ref, a0_ref, wa2_ref, wg2_ref, kk_ref, ka_ref, rk_ref,
                 seg_ref, r_o, w_o, k_o, v_o, kk_o, kka_o, g_o, bonus_o, carry):
    @pl.when(pl.program_id(1) == 0)
    def _():
        carry[...] = prev_ref[0]

    pr = pr_ref[0]
    tt = pr.shape[0]
    first = lax.broadcasted_iota(jnp.int32, pr.shape, 0) == 0
    pprev = jnp.where(first, carry[...], pltpu.roll(pr, 1, axis=0))
    carry[...] = pr[tt - 1:tt, :]
    pm = pr + (pprev - pr) * mu_ref[...]
    r = pm[:, 0:WIDTH]
    k = pm[:, WIDTH:2 * WIDTH]
    v = pm[:, 2 * WIDTH:3 * WIDTH]
    lo = pm[:, 3 * WIDTH:3 * WIDTH + LANES]
    glo = pm[:, 3 * WIDTH + LANES:3 * WIDTH + 2 * LANES]
    seg = seg_ref[...]
    w_log = -_softplus(-(w0_ref[...] + _dot(jnp.tanh(lo).astype(BF16), ww2_ref[...]))) - 0.5
    decay = jnp.exp(-jnp.exp(w_log))
    a = _sigmoid(a0_ref[...] + _dot(lo.astype(BF16), wa2_ref[...]))
    g = _dot(_sigmoid(glo).astype(BF16), wg2_ref[...])
    kk = k * kk_ref[...]
    kk = kk / jnp.maximum(jnp.sqrt(_dot_x3(kk * kk, seg)), 1e-12)
    k2 = k * (1.0 + (a - 1.0) * ka_ref[...])
    for d in range(r_o.shape[0]):
        r_o[d, 0] = r
        w_o[d, 0] = decay
        k_o[d, 0] = k2
        kk_o[d, 0] = kk
        kka_o[d, 0] = kk * a
    v_o[0] = v
    g_o[0] = g
    bonus_o[0] = _dot_x3(r * k2 * rk_ref[...], seg) * v


def _seg_ones(width, seg):
    i = jnp.arange(width) // seg
    return (i[:, None] == i[None, :]).astype(BF16)


def _lane_dup(b):
    return max(1, LANES // (b * N_HEADS))


def _rwkv_prep(pr, prev, p, tt):
    b, t, _ = pr.shape
    dup = _lane_dup(b)
    zpad = jnp.zeros((LANES - DECAY_LORA, WIDTH), F32)
    ww2 = jnp.concatenate([p["w_w2"], zpad], axis=0).astype(BF16)
    wa2 = jnp.concatenate([zpad, p["w_a2"]], axis=0).astype(BF16)
    row = lambda a: a.reshape(1, -1)
    consts = [row(p["mu_shift"]), row(p["w0"]), ww2, row(p["a0"]), wa2, p["w_g2"].astype(BF16),
              row(p["k_k"]), row(p["k_a"]), row(p["r_k"]), _seg_ones(WIDTH, HEAD_DIM)]
    const_specs = [pl.BlockSpec(c.shape, lambda bi, i: (0, 0)) for c in consts]
    o_spec = pl.BlockSpec((1, tt, WIDTH), lambda bi, i: (bi, i, 0))
    k_spec = pl.BlockSpec((dup, 1, tt, WIDTH), lambda bi, i: (0, bi, i, 0))
    o_shape = jax.ShapeDtypeStruct((b, t, WIDTH), F32)
    k_shape = jax.ShapeDtypeStruct((dup, b, t, WIDTH), F32)
    r, dec, k2, v, kk, kka, g, bonus = pl.pallas_call(
        _prep_kernel,
        grid=(b, t // tt),
        in_specs=[pl.BlockSpec((1, tt, R_COLS), lambda bi, i: (bi, i, 0)),
                  pl.BlockSpec((1, 1, R_COLS), lambda bi, i: (bi, 0, 0))] + const_specs,
        out_specs=[k_spec, k_spec, k_spec, o_spec, k_spec, k_spec, o_spec, o_spec],
        out_shape=[k_shape, k_shape, k_shape, o_shape, k_shape, k_shape, o_shape, o_shape],
        scratch_shapes=[pltpu.VMEM((1, R_COLS), F32)],
        compiler_params=_cparams(("parallel", "arbitrary")),
        name="rwkv_prep",
    )(pr, prev.reshape(b, 1, R_COLS), *consts)
    return r, dec, k2, v, kk, kka, g, bonus


def _scan_kernel(kk_ref, w_ref, k_ref, kka_ref, r_ref, v_ref, s0_ref, y_ref, s_ref, *, steps, n_slabs):
    @pl.when(pl.program_id(0) == 0)
    def _():
        s_ref[...] = s0_ref[...]

    def step(t, carry):
        kk = kk_ref[t]
        w = w_ref[t]
        k = k_ref[t]
        kka = kka_ref[t]
        r = r_ref[t]
        rw = r * w
        c1 = jnp.sum(r * kka, axis=0, keepdims=True)
        c2 = jnp.sum(r * k, axis=0, keepdims=True)
        for n in range(n_slabs):
            s = s_ref[n]
            sa = -jnp.sum(s * kk, axis=0, keepdims=True)
            yp = jnp.sum(s * rw, axis=0, keepdims=True)
            vn = v_ref[t, pl.ds(n, 1), :]
            s_ref[n] = s * w + sa * kka + vn * k
            y_ref[t, pl.ds(n, 1), :] = yp + c1 * sa + c2 * vn
        return carry

    lax.fori_loop(0, steps, step, 0)


def _wkv_scan(r, w, k, v, kk, kka, s0, tc):
    b, t, _ = v.shape
    bh = b * N_HEADS
    dup = _lane_dup(b)
    lanes = dup * bh
    ni = HEAD_DIM // dup

    def key_layout(x):
        return x.reshape(dup, b, t, N_HEADS, HEAD_DIM).transpose(2, 4, 0, 1, 3).reshape(t, HEAD_DIM, lanes)

    v_l = v.reshape(b, t, N_HEADS, ni, dup).transpose(1, 3, 4, 0, 2).reshape(t, ni, lanes)
    s_l = s0.reshape(b, N_HEADS, ni, dup, HEAD_DIM).transpose(2, 4, 3, 0, 1).reshape(ni, HEAD_DIM, lanes)
    key_spec = pl.BlockSpec((tc, HEAD_DIM, lanes), lambda c: (c, 0, 0))
    val_spec = pl.BlockSpec((tc, ni, lanes), lambda c: (c, 0, 0))
    st_spec = pl.BlockSpec((ni, HEAD_DIM, lanes), lambda c: (0, 0, 0))
    y_l, s_out = pl.pallas_call(
        functools.partial(_scan_kernel, steps=tc, n_slabs=ni),
        grid=(t // tc,),
        in_specs=[key_spec] * 5 + [val_spec, st_spec],
        out_specs=[val_spec, st_spec],
        out_shape=[jax.ShapeDtypeStruct((t, ni, lanes), F32),
                   jax.ShapeDtypeStruct((ni, HEAD_DIM, lanes), F32)],
        compiler_params=_cparams(("arbitrary",)),
        name="wkv_scan",
    )(key_layout(kk), key_layout(w), key_layout(k), key_layout(kka), key_layout(r), v_l, s_l)
    y = y_l.reshape(t, ni, dup, b, N_HEADS).transpose(3, 0, 4, 1, 2).reshape(b, t, WIDTH)
    s_new = s_out.reshape(ni, HEAD_DIM, dup, b, N_HEADS).transpose(3, 4, 0, 2, 1).reshape(
        b, N_HEADS, HEAD_DIM, HEAD_DIM)
    return y, s_new


def _cumsum_kernel(x_ref, o_ref, carry):
    @pl.when(pl.program_id(1) == 0)
    def _():
        carry[...] = jnp.zeros_like(carry)

    x = x_ref[0]
    tb = x.shape[0]
    tril = (lax.broadcasted_iota(jnp.int32, (tb, tb), 1) <= lax.broadcasted_iota(jnp.int32, (tb, tb), 0)).astype(BF16)
    c = _dot_3x(tril, x) + carry[...]
    o_ref[0] = c
    carry[...] = c[tb - 1:tb, :]


def _cumsum_time(x, tb):
    b, t, n = x.shape
    spec = pl.BlockSpec((1, tb, n), lambda bi, i: (bi, i, 0))
    return pl.pallas_call(
        _cumsum_kernel, grid=(b, t // tb), in_specs=[spec], out_specs=spec,
        out_shape=jax.ShapeDtypeStruct((b, t, n), F32),
        scratch_shapes=[pltpu.VMEM((1, n), F32)],
        compiler_params=_cparams(("parallel", "arbitrary")),
        name="cumsum_time",
    )(x)


def _bias_lanes(x, col, lane, base, own, key_side):
    hi, mid, lo = _split3(col)
    first, second = (base + 3, base) if key_side else (base, base + 3)
    out = jnp.where(own, x, 0.0)
    out = jnp.where((lane >= second) & (lane < second + 3), 1.0, out)
    out = jnp.where(lane == first, hi.astype(F32), out)
    out = jnp.where(lane == first + 1, mid.astype(F32), out)
    return jnp.where(lane == first + 2, lo.astype(F32), out)


def _fox_prompt_kernel(q_ref, k_ref, v_ref, cq_ref, ck_ref, o_ref, kaug, vb, *, tq):
    i = pl.program_id(2)
    t = k_ref.shape[1]
    pair = pl.program_id(1)

    @pl.when(i == 0)
    def _():
        lane_k = lax.broadcasted_iota(jnp.int32, (t, LANES), 1)
        vb[...] = v_ref[0].astype(BF16)
        k = k_ref[0]
        for hh in range(2):
            own = (lane_k >= hh * HEAD_DIM) & (lane_k < (hh + 1) * HEAD_DIM)
            ck = jnp.sum(jnp.where(lane_k == 2 * pair + hh, ck_ref[0], 0.0), axis=-1, keepdims=True)
            kaug[hh] = _bias_lanes(k, -ck, lane_k, (1 - hh) * HEAD_DIM, own, True).astype(BF16)

    lane = lax.broadcasted_iota(jnp.int32, (tq, LANES), 1)
    q = q_ref[0] * (HEAD_DIM ** -0.5)
    qa = []
    for hh in range(2):
        own = (lane >= hh * HEAD_DIM) & (lane < (hh + 1) * HEAD_DIM)
        cq = jnp.sum(jnp.where(lane == 2 * pair + hh, cq_ref[0], 0.0), axis=-1, keepdims=True)
        qa.append(_bias_lanes(q, cq, lane, (1 - hh) * HEAD_DIM, own, False).astype(BF16))
    causal = lax.broadcasted_iota(jnp.int32, (tq, tq), 1) <= lax.broadcasted_iota(jnp.int32, (tq, tq), 0)

    def block(j, carry, masked):
        start = pl.multiple_of(j * tq, tq)
        v_blk = vb[pl.ds(start, tq), :]
        out = []
        for hh in range(2):
            m, l, acc = carry[hh]
            s = _dot_nt(qa[hh], kaug[hh, pl.ds(start, tq), :])
            if masked:
                s = jnp.where(causal, s, -jnp.inf)
            m_new = jnp.maximum(m, jnp.max(s, axis=-1, keepdims=True))
            p = jnp.exp(s - m_new)
            alpha = jnp.exp(m - m_new)
            out.append((m_new, alpha * l + jnp.sum(p, axis=-1, keepdims=True),
                        alpha * acc + _dot(p.astype(BF16), v_blk)))
        return tuple(out)

    one = (jnp.full((tq, 1), -jnp.inf, F32), jnp.zeros((tq, 1), F32), jnp.zeros((tq, LANES), F32))
    carry = lax.fori_loop(0, i, lambda j, c: block(j, c, False), (one, one))
    (_, l0, a0), (_, l1, a1) = block(i, carry, True)
    o_ref[0] = jnp.where(lane < HEAD_DIM, a0 / l0, a1 / l1).astype(o_ref.dtype)


def _fox_prompt(qkv, cum, tq):
    b, t, _ = qkv.shape
    npair = WIDTH // LANES
    return pl.pallas_call(
        functools.partial(_fox_prompt_kernel, tq=tq),
        grid=(b, npair, t // tq),
        in_specs=[pl.BlockSpec((1, tq, LANES), lambda bi, p, i: (bi, i, p)),
                  pl.BlockSpec((1, t, LANES), lambda bi, p, i: (bi, 0, npair + p)),
                  pl.BlockSpec((1, t, LANES), lambda bi, p, i: (bi, 0, 2 * npair + p)),
                  pl.BlockSpec((1, tq, LANES), lambda bi, p, i: (bi, i, 0)),
                  pl.BlockSpec((1, t, LANES), lambda bi, p, i: (bi, 0, 0))],
        out_specs=pl.BlockSpec((1, tq, LANES), lambda bi, p, i: (bi, i, p)),
        out_shape=jax.ShapeDtypeStruct((b, t, WIDTH), BF16),
        scratch_shapes=[pltpu.VMEM((2, t, LANES), BF16), pltpu.VMEM((t, LANES), BF16)],
        compiler_params=_cparams(("parallel", "parallel", "arbitrary")),
        name="fox_prompt",
    )(qkv, qkv, qkv, cum, cum)


def _page_sums_kernel(lf_ref, rev_ref, tot_ref):
    n = lf_ref.shape[1]
    a = lax.broadcasted_iota(jnp.int32, (n, n), 0)
    c = lax.broadcasted_iota(jnp.int32, (n, n), 1)
    hi, mid, lo = _split3(lf_ref[...])
    later = (a > c).astype(BF16)
    rev_ref[...] = _dot(hi, later) + _dot(mid, later) + _dot(lo, later)
    every = jnp.ones((n, n), BF16)
    tot_ref[...] = _dot(hi, every) + _dot(mid, every) + _dot(lo, every)


def _page_sums(lf_rows, rows):
    n_rows, n = lf_rows.shape
    spec = pl.BlockSpec((rows, n), lambda i: (i, 0))
    return pl.pallas_call(
        _page_sums_kernel, grid=(n_rows // rows,), in_specs=[spec], out_specs=[spec, spec],
        out_shape=[jax.ShapeDtypeStruct((n_rows, n), F32)] * 2,
        compiler_params=_cparams(("parallel",)),
        name="page_sums",
    )(lf_rows)


def _fox_sample_kernel(pt_ref, q_ref, kn_ref, vn_ref, lfn_ref, *rest, pages_per_step):
    npp = pages_per_step
    k_refs = rest[:npp]
    v_refs = rest[npp:2 * npp]
    rev_refs = rest[2 * npp:3 * npp]
    tot_refs = rest[3 * npp:4 * npp]
    o_ref = rest[4 * npp]
    qrep, m_s, l_s, acc_s, suf_s, cn_s = rest[4 * npp + 1:]
    step = pl.program_id(1)
    nq = q_ref.shape[1]
    rows = N_HEADS * nq
    page = kn_ref.shape[1]
    row_head = lax.broadcasted_iota(jnp.int32, (rows, WIDTH), 0) // nq
    lane_head = lax.broadcasted_iota(jnp.int32, (rows, WIDTH), 1) // HEAD_DIM

    def rep_heads(x):
        return jnp.concatenate([jnp.broadcast_to(x[h:h + 1, :], (nq, x.shape[1])) for h in range(N_HEADS)], axis=0)

    def update(s_list, pv):
        m_old = m_s[...]
        m_new = functools.reduce(jnp.maximum, [jnp.max(s, axis=-1, keepdims=True) for s in s_list] + [m_old])
        p_list = [jnp.exp(s - m_new) for s in s_list]
        alpha = jnp.exp(m_old - m_new)
        l_s[...] = alpha * l_s[...] + functools.reduce(jnp.add, [jnp.sum(p, axis=-1, keepdims=True) for p in p_list])
        acc_s[...] = alpha * acc_s[...] + functools.reduce(jnp.add, [pv(u, p.astype(BF16)) for u, p in enumerate(p_list)])
        m_s[...] = m_new

    @pl.when(step == 0)
    def _():
        q = q_ref[0] * (HEAD_DIM ** -0.5)
        qrep[...] = jnp.where(row_head == lane_head, jnp.concatenate([q] * N_HEADS, axis=0), 0.0).astype(BF16)
        key_i = lax.broadcasted_iota(jnp.int32, (page, page), 0)
        key_j = lax.broadcasted_iota(jnp.int32, (page, page), 1)
        cn_row = _dot_x3(rep_heads(lfn_ref[0]), (key_i <= key_j).astype(BF16))
        rq = lax.broadcasted_iota(jnp.int32, (rows, page), 0) % nq
        kc = lax.broadcasted_iota(jnp.int32, (rows, page), 1)
        cn_col = jnp.sum(jnp.where(kc == rq, cn_row, 0.0), axis=-1, keepdims=True)
        cn_s[...] = cn_col
        suf_s[...] = jnp.zeros_like(suf_s)
        m_s[...] = jnp.full_like(m_s, -jnp.inf)
        l_s[...] = jnp.zeros_like(l_s)
        acc_s[...] = jnp.zeros_like(acc_s)
        s = _dot_nt(qrep[...], kn_ref[0].astype(BF16)) + cn_col - cn_row
        vn = vn_ref[0].astype(BF16)
        update([jnp.where(kc <= rq, s, -jnp.inf)], lambda u, p: _dot(p, vn))

    q_all = qrep[...]
    cn = cn_s[...]
    suf = suf_s[...]
    s_list = []
    for u in range(npp):
        kt = k_refs[u][0].reshape(WIDTH, page).astype(BF16)
        s_list.append(_dot(q_all, kt) + ((cn + suf) + rep_heads(rev_refs[u][0])))
        suf = suf + rep_heads(tot_refs[u][0])
    update(s_list, lambda u, p: _dot_nt(p, v_refs[u][0].reshape(WIDTH, page).astype(BF16)))
    suf_s[...] = suf

    @pl.when(step == pl.num_programs(1) - 1)
    def _():
        o_sel = jnp.where(row_head == lane_head, acc_s[...] / l_s[...], 0.0)
        out = o_sel[0:nq]
        for h in range(1, N_HEADS):
            out = out + o_sel[h * nq:(h + 1) * nq]
        o_ref[0] = out.astype(o_ref.dtype)


def _fox_sample(q, k_new, v_new, logf_new, cache_k, cache_v, cache_logf, page_table, pages_per_step):
    b, tn, _ = q.shape
    n_pool, page = cache_k.shape[:2]
    n_pages = page_table.shape[1]
    npp = pages_per_step
    assert n_pages % npp == 0 and tn <= page
    rows = N_HEADS * tn
    ck = cache_k.transpose(0, 2, 3, 1)
    cv = cache_v.transpose(0, 2, 3, 1)
    clf = cache_logf.transpose(0, 2, 1).reshape(n_pool * N_HEADS, page)
    sum_rows = 2048 if clf.shape[0] % 2048 == 0 else clf.shape[0]
    rev, tot = (a.reshape(n_pool, N_HEADS, page) for a in _page_sums(clf, sum_rows))
    pad_rows = lambda x: jnp.pad(x, ((0, 0), (0, page - tn), (0, 0)))
    lfn = jnp.pad(logf_new.transpose(0, 2, 1), ((0, 0), (0, 0), (0, page - tn)))

    def page_map(u, nd):
        return lambda bi, s, pt: (pt[bi, n_pages - 1 - (s * npp + u)],) + (0,) * nd

    tok_spec = lambda r: pl.BlockSpec((1, r, WIDTH), lambda bi, s, pt: (bi, 0, 0))
    in_specs = ([tok_spec(tn), tok_spec(page), tok_spec(page),
                 pl.BlockSpec((1, N_HEADS, page), lambda bi, s, pt: (bi, 0, 0))]
                + [pl.BlockSpec((1, N_HEADS, HEAD_DIM, page), page_map(u, 3)) for u in range(npp)] * 2
                + [pl.BlockSpec((1, N_HEADS, page), page_map(u, 2)) for u in range(npp)] * 2)
    grid_spec = pltpu.PrefetchScalarGridSpec(
        num_scalar_prefetch=1,
        grid=(b, n_pages // npp),
        in_specs=in_specs,
        out_specs=tok_spec(tn),
        scratch_shapes=[pltpu.VMEM((rows, WIDTH), BF16), pltpu.VMEM((rows, 1), F32), pltpu.VMEM((rows, 1), F32),
                        pltpu.VMEM((rows, WIDTH), F32), pltpu.VMEM((rows, page), F32), pltpu.VMEM((rows, 1), F32)])
    return pl.pallas_call(
        functools.partial(_fox_sample_kernel, pages_per_step=npp),
        grid_spec=grid_spec,
        out_shape=jax.ShapeDtypeStruct((b, tn, WIDTH), BF16),
        compiler_params=_cparams(("parallel", "arbitrary")),
        name="fox_sample",
    )(page_table, q, pad_rows(k_new), pad_rows(v_new), lfn, *([ck] * npp), *([cv] * npp), *([rev] * npp),
      *([tot] * npp))


def _merge_kernel(y_ref, bonus_ref, g_ref, yf_ref, gates_ref, x_ref, g1_ref, sh2_ref, sc2_ref,
                  lnw_ref, lnb_ref, seg_ref, wr_ref, wf_ref, wo_ref, n2_ref, wrt_ref,
                  x1_ref, h2_ref, lg_ref):
    seg = seg_ref[...]
    y = y_ref[0]
    mu = _dot_x3(y, seg) * (1.0 / HEAD_DIM)
    d = y - mu
    var = _dot_x3(d * d, seg) * (1.0 / HEAD_DIM)
    yn = d * lax.rsqrt(var + LNX_EPS) * lnw_ref[...] + lnb_ref[...]
    yr = ((yn + bonus_ref[0]) * g_ref[0]).astype(BF16)
    gates = gates_ref[0].astype(F32)
    d_model = x_ref.shape[2]
    merged = gates[:, :d_model] * _dot(yr, wr_ref[...]) + gates[:, d_model:] * _dot(yf_ref[0], wf_ref[...])
    x1 = x_ref[0] + g1_ref[0] * _dot(merged.astype(BF16), wo_ref[...])
    x1_ref[0] = x1
    h2 = _rmsnorm(x1, n2_ref[...]) * (1.0 + sc2_ref[0]) + sh2_ref[0]
    h2_ref[0] = h2
    hh, hl = _split2(h2)
    wrt = wrt_ref[...]
    lg_ref[0] = _dot(hh, wrt[0]) + (_dot(hh, wrt[1]) + _dot(hl, wrt[0]))


def _merge(y, bonus, g, yf, gates, x, g1, sh2, sc2, p, w_router_t, tm):
    b, t, d = x.shape
    per_tok = g1.shape[1] != 1
    mrows = tm if per_tok else 1
    mod_map = (lambda bi, i: (bi, i, 0)) if per_tok else (lambda bi, i: (bi, 0, 0))
    tok = lambda n: pl.BlockSpec((1, tm, n), lambda bi, i: (bi, i, 0))
    mod = pl.BlockSpec((1, mrows, d), mod_map)
    row = lambda a: a.reshape(1, -1)
    consts = [row(p["lnx_w"]), row(p["lnx_b"]), _seg_ones(WIDTH, HEAD_DIM), p["w_br_r"].astype(BF16),
              p["w_br_f"].astype(BF16), p["w_out"].astype(BF16), row(p["norm2_g"]), w_router_t]
    const_specs = [pl.BlockSpec(c.shape, (lambda bi, i: (0, 0)) if c.ndim == 2 else (lambda bi, i: (0, 0, 0)))
                   for c in consts]
    return pl.pallas_call(
        _merge_kernel,
        grid=(b, t // tm),
        in_specs=[tok(WIDTH), tok(WIDTH), tok(WIDTH), tok(WIDTH), tok(2 * d), tok(d), mod, mod, mod] + const_specs,
        out_specs=[tok(d), tok(d), tok(LANES)],
        out_shape=[jax.ShapeDtypeStruct((b, t, d), F32), jax.ShapeDtypeStruct((b, t, d), F32),
                   jax.ShapeDtypeStruct((b, t, LANES), F32)],
        compiler_params=_cparams(("parallel", "parallel")),
        name="merge",
    )(y, bonus, g, yf, gates, x, g1, sh2, sc2, *consts)


def _route_kernel(lg_ref, bias_ref, row_ref, rowt_ref, wt_ref, cnt_ref):
    lt = lg_ref[...].T
    tm = lt.shape[1]
    score = _sigmoid(lt[:N_EXPERTS])
    biased = score + bias_ref[...][:N_EXPERTS]
    slab = [biased[k * N_GROUPS:(k + 1) * N_GROUPS] for k in range(GROUP_SIZE)]
    neg = jnp.full((N_GROUPS, tm), -jnp.inf, F32)
    m1 = functools.reduce(jnp.maximum, slab)
    taken = jnp.zeros((N_GROUPS, tm), jnp.bool_)
    m2 = neg
    for k in range(GROUP_SIZE):
        is_first = (slab[k] == m1) & jnp.logical_not(taken)
        taken = taken | is_first
        m2 = jnp.maximum(m2, jnp.where(is_first, neg, slab[k]))
    gs = m1 + m2
    g_iota = lax.broadcasted_iota(jnp.int32, (N_GROUPS, tm), 0)
    cnt = jnp.zeros((N_GROUPS, tm), jnp.int32)
    for g2 in range(N_GROUPS):
        other = gs[g2:g2 + 1, :]
        beats = (other > gs) | ((g_iota > g2) & (other == gs))
        cnt = cnt + beats.astype(jnp.int32)
    g_sel = cnt < TOPK_GROUPS
    cand = [jnp.where(g_sel, slab[k], neg) for k in range(GROUP_SIZE)]
    rank = [jnp.zeros((N_GROUPS, tm), jnp.int32) for _ in range(GROUP_SIZE)]
    for k2 in range(GROUP_SIZE):
        for g2 in range(N_GROUPS):
            other = cand[k2][g2:g2 + 1, :]
            for k in range(GROUP_SIZE):
                first = (g_iota >= g2) if k2 < k else (g_iota > g2)
                beats = (other > cand[k]) | (first & (other == cand[k]))
                rank[k] = rank[k] + beats.astype(jnp.int32)
    sel = [rank[k] < TOP_K for k in range(GROUP_SIZE)]
    sc = [score[k * N_GROUPS:(k + 1) * N_GROUPS] for k in range(GROUP_SIZE)]
    picked = [jnp.where(sel[k], sc[k], 0.0) for k in range(GROUP_SIZE)]
    total = jnp.sum(functools.reduce(jnp.add, picked), axis=0, keepdims=True)
    gate = jnp.concatenate([pk / total * ROUTED_SCALE for pk in picked], axis=0)
    chosen = jnp.concatenate([s.astype(F32) for s in sel], axis=0)
    chosen_b = chosen.astype(BF16)
    ri = lax.broadcasted_iota(jnp.int32, (N_EXPERTS, N_EXPERTS), 0)
    ci = lax.broadcasted_iota(jnp.int32, (N_EXPERTS, N_EXPERTS), 1)
    lower = (ci < ri).astype(BF16)
    ordinal = _dot(lower, chosen_b)
    ta = lax.broadcasted_iota(jnp.int32, (tm, tm), 0)
    tc = lax.broadcasted_iota(jnp.int32, (tm, tm), 1)
    rank_tok = _dot(chosen_b, (ta < tc).astype(BF16))
    count = jnp.sum(chosen, axis=1, keepdims=True)
    runs = jnp.broadcast_to(jnp.floor((count + (RUN - 1)) * (1.0 / RUN)), (N_EXPERTS, LANES))
    cnt_ref[0] = runs
    start = _dot(lower, runs.astype(BF16))[:, 0:1] * float(RUN)
    tile_row = start + rank_tok
    r_rows, w_rows = [], []
    for n in range(TOP_K):
        hit = (chosen > 0.0) & (ordinal == float(n))
        r_rows.append(jnp.sum(jnp.where(hit, tile_row, 0.0), axis=0, keepdims=True))
        w_rows.append(jnp.sum(jnp.where(hit, gate, 0.0), axis=0, keepdims=True))
    zrow = jnp.zeros((1, tm), F32)
    pad8 = lambda rows: jnp.concatenate(rows + [zrow] * (8 - TOP_K), axis=0)
    zpad = jnp.zeros((LANES - 8, tm), F32)
    row_ref[...] = pad8(r_rows).astype(jnp.int32)
    rowt_ref[...] = jnp.concatenate([pad8(r_rows), zpad], axis=0).T.astype(jnp.int32)
    wt_ref[...] = jnp.concatenate([pad8(w_rows), zpad], axis=0).T


def _route(logits, e_bias_perm, tm):
    m = logits.shape[0]
    bias = jnp.pad(e_bias_perm, (0, LANES - N_EXPERTS)).reshape(LANES, 1)
    tok_spec = pl.BlockSpec((tm, LANES), lambda i: (i, 0))
    return pl.pallas_call(
        _route_kernel,
        grid=(m // tm,),
        in_specs=[tok_spec, pl.BlockSpec((LANES, 1), lambda i: (0, 0))],
        out_specs=[pl.BlockSpec((8, tm), lambda i: (0, i)), tok_spec, tok_spec,
                   pl.BlockSpec((1, N_EXPERTS, LANES), lambda i: (i, 0, 0))],
        out_shape=[jax.ShapeDtypeStruct((8, m), jnp.int32), jax.ShapeDtypeStruct((m, LANES), jnp.int32),
                   jax.ShapeDtypeStruct((m, LANES), F32), jax.ShapeDtypeStruct((m // tm, N_EXPERTS, LANES), F32)],
        compiler_params=_cparams(("parallel",)),
        name="route",
    )(logits, bias)


def _sorted_capacity(tm):
    rows = TOP_K * tm + N_EXPERTS * (RUN - 1)
    return -(-rows // SORT_BLOCK) * SORT_BLOCK


def _run_pieces(i, nrun_ref, src_ref, dst_ref, piece):
    def per_expert(e, total):
        j = i * N_EXPERTS + e
        n, s0, d0 = nrun_ref[j], src_ref[j], dst_ref[j]

        def one(c, carry):
            piece(s0 + c, d0 + c).start()
            return carry

        lax.fori_loop(0, n, one, 0)
        return total + n

    return lax.fori_loop(0, N_EXPERTS, per_expert, 0)


def _sort_kernel(nrun_ref, src_ref, dst_ref, nblk_ref, row_ref, h_ref, xs_in_ref, xs_ref, buf, sem):
    del xs_in_ref
    i = pl.program_id(0)
    tm = h_ref.shape[0]
    hb = h_ref[...].astype(BF16)
    rows = row_ref[...]
    riota = lax.broadcasted_iota(jnp.int32, (SORT_BLOCK, tm), 0)

    def block(b, carry):
        r0 = pl.multiple_of(b * SORT_BLOCK, SORT_BLOCK)
        hit = riota + r0 == rows[0:1, :]
        for n in range(1, TOP_K):
            hit = hit | (riota + r0 == rows[n:n + 1, :])
        buf[pl.ds(r0, SORT_BLOCK), :] = _dot(hit.astype(BF16), hb).astype(BF16)
        return carry

    lax.fori_loop(0, nblk_ref[i], block, 0)

    def piece(s, d):
        return pltpu.make_async_copy(buf.at[pl.ds(pl.multiple_of(s * RUN, RUN), RUN), :],
                                     xs_ref.at[pl.ds(pl.multiple_of(d * RUN, RUN), RUN), :], sem)

    started = _run_pieces(i, nrun_ref, src_ref, dst_ref, piece)

    def drain(c, carry):
        piece(0, 0).wait()
        return carry

    lax.fori_loop(0, started, drain, 0)


def _sort_rows(h2, row, tables, n_rows, tm):
    m, d = h2.shape
    grid_spec = pltpu.PrefetchScalarGridSpec(
        num_scalar_prefetch=4,
        grid=(m // tm,),
        in_specs=[pl.BlockSpec((8, tm), lambda i, *_: (0, i)), pl.BlockSpec((tm, d), lambda i, *_: (i, 0)),
                  pl.BlockSpec(memory_space=pl.ANY)],
        out_specs=pl.BlockSpec(memory_space=pl.ANY),
        scratch_shapes=[pltpu.VMEM((_sorted_capacity(tm), d), BF16), pltpu.SemaphoreType.DMA(())])
    return pl.pallas_call(
        _sort_kernel,
        grid_spec=grid_spec,
        out_shape=jax.ShapeDtypeStruct((n_rows, d), BF16),
        input_output_aliases={6: 0},
        compiler_params=_cparams(("arbitrary",)),
        name="sort_rows",
    )(*tables, row, h2, jnp.zeros((n_rows, d), BF16))


def _expert_kernel(te_ref, nu_ref, x_ref, wg_ref, wu_ref, wd_ref, o_ref, wgb, wub, wdb):
    i = pl.program_id(0)
    changed = (i == 0) | (te_ref[i] != te_ref[jnp.maximum(i - 1, 0)])

    @pl.when(changed)
    def _():
        wgb[...] = wg_ref[0].astype(BF16)
        wub[...] = wu_ref[0].astype(BF16)
        wdb[...] = wd_ref[0].astype(BF16)

    @pl.when(i < nu_ref[0])
    def _():
        x = x_ref[...]
        act = _silu(_dot(x, wgb[...])) * _dot(x, wub[...])
        o_ref[...] = _dot(act.astype(BF16), wdb[...]).astype(o_ref.dtype)

    @pl.when(i >= nu_ref[0])
    def _():
        o_ref[...] = jnp.zeros_like(o_ref)


def _expert_tiles(xs, tile_expert, n_used, wg, wu, wd, te):
    n_rows, d = xs.shape
    ff = wg.shape[2]
    last = lambda i, nu: jnp.minimum(i, nu[0] - 1)
    grid_spec = pltpu.PrefetchScalarGridSpec(
        num_scalar_prefetch=2,
        grid=(n_rows // te,),
        in_specs=[pl.BlockSpec((te, d), lambda i, tx, nu: (last(i, nu), 0)),
                  pl.BlockSpec((1, d, ff), lambda i, tx, nu: (tx[i], 0, 0)),
                  pl.BlockSpec((1, d, ff), lambda i, tx, nu: (tx[i], 0, 0)),
                  pl.BlockSpec((1, ff, d), lambda i, tx, nu: (tx[i], 0, 0))],
        out_specs=pl.BlockSpec((te, d), lambda i, tx, nu: (i, 0)),
        scratch_shapes=[pltpu.VMEM((d, ff), BF16), pltpu.VMEM((d, ff), BF16), pltpu.VMEM((ff, d), BF16)])
    return pl.pallas_call(
        _expert_kernel,
        grid_spec=grid_spec,
        out_shape=jax.ShapeDtypeStruct((n_rows, d), BF16),
        compiler_params=_cparams(("arbitrary",)),
        name="expert_tiles",
    )(tile_expert, n_used, xs, wg, wu, wd)


def _combine_kernel(nrun_ref, src_ref, dst_ref, nblk_ref, rowt_ref, wt_ref, h_ref, x1_ref, g2_ref, nf_ref,
                    wsg_ref, wsu_ref, wsd_ref, os_ref, y_ref, buf, rowb, wb, sem):
    i = pl.program_id(0) * pl.num_programs(1) + pl.program_id(1)
    tm = h_ref.shape[1]
    nblk = nblk_ref[i]
    tail = pl.multiple_of((nblk - 1) * SORT_BLOCK, SORT_BLOCK)
    buf[pl.ds(tail, SORT_BLOCK), :] = jnp.zeros((SORT_BLOCK, buf.shape[1]), BF16)

    def piece(s, d):
        return pltpu.make_async_copy(os_ref.at[pl.ds(pl.multiple_of(d * RUN, RUN), RUN), :],
                                     buf.at[pl.ds(pl.multiple_of(s * RUN, RUN), RUN), :], sem)

    started = _run_pieces(i, nrun_ref, src_ref, dst_ref, piece)
    h = h_ref[0].astype(BF16)
    shared = _dot((_silu(_dot(h, wsg_ref[...])) * _dot(h, wsu_ref[...])).astype(BF16), wsd_ref[...])

    def drain(c, carry):
        piece(0, 0).wait()
        return carry

    w = wt_ref[0]
    rt = rowt_ref[0]
    for n in range(TOP_K):
        rowb[n] = jnp.broadcast_to(rt[:, n:n + 1], (tm, SORT_BLOCK))
        wb[n] = jnp.broadcast_to(w[:, n:n + 1], (tm, SORT_BLOCK))
    lax.fori_loop(0, started, drain, 0)
    ciota = lax.broadcasted_iota(jnp.int32, (tm, SORT_BLOCK), 1)

    def block(b, acc):
        r0 = pl.multiple_of(b * SORT_BLOCK, SORT_BLOCK)
        c = jnp.where(ciota == rowb[0] - r0, wb[0], 0.0)
        for n in range(1, TOP_K):
            c = c + jnp.where(ciota == rowb[n] - r0, wb[n], 0.0)
        return acc + _dot(c.astype(BF16), buf[pl.ds(r0, SORT_BLOCK), :])

    routed = lax.fori_loop(0, nblk, block, jnp.zeros((tm, buf.shape[1]), F32))
    x2 = x1_ref[0] + g2_ref[0] * (routed + shared)
    y_ref[0] = _rmsnorm(x2, nf_ref[...])


def _combine(os, rowt, wt, tables, h2, x1, g2, normf_g, wsg, wsu, wsd, tm):
    b, t, d = x1.shape
    nt = t // tm
    per_tok = g2.shape[1] != 1
    mrows = tm if per_tok else 1
    mod_map = (lambda bi, i, *_: (bi, i, 0)) if per_tok else (lambda bi, i, *_: (bi, 0, 0))
    tok = lambda n: pl.BlockSpec((1, tm, n), lambda bi, i, *_: (bi, i, 0))
    const = lambda a: pl.BlockSpec(a.shape, lambda bi, i, *_: (0, 0))
    grid_spec = pltpu.PrefetchScalarGridSpec(
        num_scalar_prefetch=4,
        grid=(b, nt),
        in_specs=[tok(LANES), tok(LANES), tok(d), tok(d), pl.BlockSpec((1, mrows, d), mod_map),
                  pl.BlockSpec((1, d), lambda bi, i, *_: (0, 0)), const(wsg), const(wsu), const(wsd),
                  pl.BlockSpec(memory_space=pl.ANY)],
        out_specs=tok(d),
        scratch_shapes=[pltpu.VMEM((_sorted_capacity(tm), d), BF16), pltpu.VMEM((TOP_K, tm, SORT_BLOCK), jnp.int32),
                        pltpu.VMEM((TOP_K, tm, SORT_BLOCK), F32), pltpu.SemaphoreType.DMA(())])
    return pl.pallas_call(
        _combine_kernel,
        grid_spec=grid_spec,
        out_shape=jax.ShapeDtypeStruct((b, t, d), F32),
        compiler_params=_cparams(("arbitrary", "arbitrary")),
        name="combine",
    )(*tables, rowt.reshape(b, t, LANES), wt.reshape(b, t, LANES), h2, x1, g2, normf_g.reshape(1, d), wsg, wsu, wsd, os)


def _moe(h2, logits, x1, g2, p, w, cfg):
    bx, tx, d = x1.shape
    m = bx * tx
    te, tm = cfg["te"], cfg["tm_route"]
    row, rowt, wt, cnt = _route(logits.reshape(m, LANES), w["e_bias_perm"], tm)
    runs = cnt[:, :, 0].astype(jnp.int32)
    src = jnp.cumsum(runs, axis=1) - runs
    nblk = (jnp.sum(runs, axis=1) * RUN + SORT_BLOCK - 1) // SORT_BLOCK
    per_expert = jnp.sum(runs, axis=0)
    tiles_e = (per_expert * RUN + te - 1) // te
    ends = jnp.cumsum(tiles_e)
    dst = ((ends - tiles_e) * (te // RUN))[None, :] + jnp.cumsum(runs, axis=0) - runs
    n_tiles = (m * TOP_K + (m // tm) * N_EXPERTS * (RUN - 1)) // te + N_EXPERTS
    n_used = ends[-1:].astype(jnp.int32)
    tile_ids = jnp.minimum(jnp.arange(n_tiles, dtype=jnp.int32), n_used[0] - 1)
    tile_row = jnp.minimum(jnp.sum(ends[None, :] <= tile_ids[:, None], axis=1), N_EXPERTS - 1)
    tile_expert = ((tile_row % N_GROUPS) * GROUP_SIZE + tile_row // N_GROUPS).astype(jnp.int32)
    flat = lambda a: a.reshape(-1).astype(jnp.int32)
    tables = (flat(runs), flat(src), flat(dst), flat(nblk))
    xs = _sort_rows(h2.reshape(m, d), row, tables, n_tiles * te, tm)
    os = _expert_tiles(xs, tile_expert, n_used, p["w_exp_gate"], p["w_exp_up"], p["w_exp_down"], te)
    return _combine(os, rowt, wt, tables, h2, x1, g2, p["normf_g"], w["sh_gate"], w["sh_up"], w["sh_down"], tm)


def _layer(x, mod, shift_prev, wkv0, attend, p, w, cfg):
    b, t, d = x.shape
    bx, tx = cfg["rows"]
    tm = cfg["tm"]
    xr = x.reshape(bx, tx, d)
    if bx == b:
        part = lambda i: mod[:, i:i + 1, :]
    else:
        part = lambda i: jnp.repeat(mod[:, i, :], t, axis=0).reshape(bx, tx, d)
    sh1, sc1, g1, sh2, sc2, g2 = (part(i) for i in range(6))
    h_rows = 1 if bx == b else tx
    proj = functools.partial(_in_proj, xr, sh1, sc1, p["norm1_g"], tm=tm, h_rows=h_rows)
    pr, h_keep = proj(w["in_r"], None, tn=cfg["tn_r"], epilogue="none", out_dtype=F32)
    qkv, _ = proj(w["in_qkv"], None, tn=512, epilogue="none", out_dtype=F32)
    logf_pad, _ = proj(w["in_f"], w["b_f_pad"], tn=LANES, epilogue="log_sigmoid", out_dtype=F32)
    gates, _ = proj(w["in_g"], None, tn=512, epilogue="sigmoid", out_dtype=BF16)
    shift_new = h_keep[:, 0, :] if bx == b else h_keep.reshape(b, t, d)[:, -1, :]
    logf = logf_pad.reshape(b, t, LANES)[:, :, :N_HEADS]
    qkv = qkv.reshape(b, t, 3 * WIDTH)

    prev = _dense(shift_prev, w["in_r_f32"], jnp.zeros((R_COLS,), F32), act=False, tn=R_COLS // 2)
    r, dec, k2, v, kk, kka, g, bonus = _rwkv_prep(pr.reshape(b, t, R_COLS), prev, p, cfg["tt"])
    y_scan, wkv_new = _wkv_scan(r, dec, k2, v, kk, kka, wkv0, cfg["tc"])

    y_f = attend(qkv, logf)

    rs = lambda a: a.reshape(bx, tx, a.shape[-1])
    x1, h2, logits = _merge(rs(y_scan), rs(bonus), rs(g), rs(y_f), gates, xr, g1, sh2, sc2, p, w["router_t"], tm)
    y = _moe(h2, logits, x1, g2, p, w, cfg)
    k_out = qkv[:, :, WIDTH:2 * WIDTH].reshape(b, t, N_HEADS, HEAD_DIM)
    v_out = qkv[:, :, 2 * WIDTH:].reshape(b, t, N_HEADS, HEAD_DIM)
    return y.reshape(b, t, d), k_out, v_out, logf, wkv_new, shift_new


def kernel(x_prompt, x_sample, c_prompt, c_sample, cache_k, cache_v, cache_logf, page_table, state_wkv, state_shift, w_ada, b_ada, norm1_g, w_in, mu_shift, w0, w_w2, a0, w_a2, w_g2, k_k, k_a, r_k, lnx_w, lnx_b, b_f, w_br_r, w_br_f, w_out, norm2_g, w_router, e_bias, w_exp_gate, w_exp_up, w_exp_down, w_sh_gate, w_sh_up, w_sh_down, normf_g):
    p = dict(norm1_g=norm1_g, mu_shift=mu_shift, w0=w0, w_w2=w_w2, a0=a0, w_a2=w_a2, w_g2=w_g2, k_k=k_k, k_a=k_a,
             r_k=r_k, lnx_w=lnx_w, lnx_b=lnx_b, w_br_r=w_br_r, w_br_f=w_br_f, w_out=w_out, norm2_g=norm2_g,
             normf_g=normf_g, w_exp_gate=w_exp_gate, w_exp_up=w_exp_up, w_exp_down=w_exp_down)
    bp, tp, d = x_prompt.shape
    bs, ts, _ = x_sample.shape
    off_f = R_COLS + 3 * WIDTH
    off_g = off_f + N_HEADS
    perm = lambda a: a.reshape(a.shape[:-1] + (N_GROUPS, GROUP_SIZE)).swapaxes(-1, -2).reshape(a.shape)
    router = jnp.pad(perm(w_router), ((0, 0), (0, LANES - N_EXPERTS)))
    r_hi = router.astype(BF16)
    w = dict(
        in_r=w_in[:, :R_COLS].astype(BF16), in_r_f32=w_in[:, :R_COLS],
        in_qkv=w_in[:, R_COLS:off_f].astype(BF16),
        in_f=jnp.pad(w_in[:, off_f:off_g], ((0, 0), (0, LANES - N_HEADS))).astype(BF16),
        in_g=w_in[:, off_g:].astype(BF16),
        b_f_pad=jnp.pad(b_f, (0, LANES - N_HEADS)).reshape(1, LANES),
        router_t=jnp.stack([r_hi, (router - r_hi.astype(F32)).astype(BF16)]),
        e_bias_perm=perm(e_bias),
        sh_gate=w_sh_gate.astype(BF16), sh_up=w_sh_up.astype(BF16), sh_down=w_sh_down.astype(BF16),
    )
    mod = _dense(jnp.concatenate([c_prompt, c_sample], axis=0), w_ada, b_ada, act=True).reshape(bp + bs, 6, d)

    def attend_prompt(qkv, logf):
        lf = jnp.pad(logf, ((0, 0), (0, 0), (0, LANES - N_HEADS)))
        return _fox_prompt(qkv, _cumsum_time(lf, 256), min(512, tp))

    def attend_sample(qkv, logf):
        return _fox_sample(qkv[:, :, :WIDTH], qkv[:, :, WIDTH:2 * WIDTH], qkv[:, :, 2 * WIDTH:], logf,
                           cache_k, cache_v, cache_logf, page_table, min(16, page_table.shape[1]))

    cfg_p = dict(rows=(bp, tp), tm=min(512, tp), tn_r=R_COLS // 2, tt=min(256, tp), tc=16,
                 tm_route=min(512, tp), te=512)
    cfg_s = dict(rows=(1, bs * ts), tm=bs * ts, tn_r=R_COLS // 2, tt=ts, tc=ts, tm_route=bs * ts, te=256)
    yp, kp, vp, lfp, wkvp, shp = _layer(x_prompt, mod[:bp], jnp.zeros((bp, d), F32),
                                        jnp.zeros((bp, N_HEADS, HEAD_DIM, HEAD_DIM), F32), attend_prompt, p, w, cfg_p)
    ys, ks, vs, lfs, wkvs, shs = _layer(x_sample, mod[bp:], state_shift, state_wkv, attend_sample, p, w, cfg_s)
    return (yp, ys, kp, vp, lfp, wkvp, shp, ks, vs, lfs, wkvs, shs)
```

```python
import functools

import jax
import jax.numpy as jnp
from jax import lax
from jax.experimental import pallas as pl
from jax.experimental.pallas import tpu as pltpu

F32 = jnp.float32
BF16 = jnp.bfloat16

HEAD_DIM = 64
N_HEADS = 8
WIDTH = N_HEADS * HEAD_DIM
DECAY_LORA = 64
AAA_LORA = 64
GATE_LORA = 128
R_COLS = 3 * WIDTH + DECAY_LORA + AAA_LORA + GATE_LORA
LNX_EPS = 64e-5
NORM_EPS = 1e-6
N_EXPERTS = 64
N_GROUPS = 8
GROUP_SIZE = N_EXPERTS // N_GROUPS
TOPK_GROUPS = 4
TOP_K = 6
ROUTED_SCALE = 2.5
LANES = 128
RUN = 16
SORT_BLOCK = 256
VMEM_LIMIT = 56 * 1024 * 1024


def _cparams(sem):
    return pltpu.CompilerParams(dimension_semantics=sem, vmem_limit_bytes=VMEM_LIMIT)


def _dot(a, b):
    return jnp.dot(a, b, preferred_element_type=F32)


def _dot_nt(a, b):
    return lax.dot_general(a, b, (((1,), (1,)), ((), ())), preferred_element_type=F32)


def _split2(x):
    hi = x.astype(BF16)
    lo = (x - hi.astype(F32)).astype(BF16)
    return hi, lo


def _split3(x):
    hi = x.astype(BF16)
    r = x - hi.astype(F32)
    mid = r.astype(BF16)
    lo = (r - mid.astype(F32)).astype(BF16)
    return hi, mid, lo


def _dot_x3(x, m):
    hi, mid, lo = _split3(x)
    return _dot(hi, m) + _dot(mid, m) + _dot(lo, m)


def _dot_3x(m, x):
    hi, mid, lo = _split3(x)
    return _dot(m, hi) + _dot(m, mid) + _dot(m, lo)


def _dot_hp(x, w):
    xh, xm, xl = _split3(x)
    wh, wl = _split2(w)
    return _dot(xh, wh) + (_dot(xh, wl) + _dot(xm, wh)) + (_dot(xm, wl) + _dot(xl, wh))


def _sigmoid(x):
    return 1.0 / (1.0 + jnp.exp(-x))


def _softplus(x):
    return jnp.maximum(x, 0.0) + jnp.log1p(jnp.exp(-jnp.abs(x)))


def _silu(x):
    return x * _sigmoid(x)


def _rmsnorm(x, g):
    return x * lax.rsqrt(jnp.mean(x * x, axis=-1, keepdims=True) + NORM_EPS) * g


def _dense_kernel(x_ref, w_ref, b_ref, o_ref, *, act):
    x = x_ref[...]
    if act:
        x = _silu(x)
    o_ref[...] = _dot_hp(x, w_ref[...]) + b_ref[...]


def _dense(x, w, b, act, tn=512):
    m, k = x.shape
    n = w.shape[1]
    assert n % tn == 0
    return pl.pallas_call(
        functools.partial(_dense_kernel, act=act),
        grid=(n // tn,),
        in_specs=[pl.BlockSpec((m, k), lambda j: (0, 0)),
                  pl.BlockSpec((k, tn), lambda j: (0, j)),
                  pl.BlockSpec((1, tn), lambda j: (0, j))],
        out_specs=pl.BlockSpec((m, tn), lambda j: (0, j)),
        out_shape=jax.ShapeDtypeStruct((m, n), F32),
        compiler_params=_cparams(("parallel",)),
        name="dense",
    )(x, w, b.reshape(1, n))


def _in_proj_kernel(x_ref, sh_ref, sc_ref, g_ref, w_ref, b_ref, o_ref, h_ref, h_scr, *, epilogue, h_rows):
    @pl.when(pl.program_id(2) == 0)
    def _():
        h = _rmsnorm(x_ref[0], g_ref[...]) * (1.0 + sc_ref[0]) + sh_ref[0]
        h_scr[...] = h.astype(BF16)
        h_ref[0] = h[h.shape[0] - h_rows:, :]

    acc = _dot(h_scr[...], w_ref[...])
    if epilogue == "sigmoid":
        acc = _sigmoid(acc)
    elif epilogue == "log_sigmoid":
        acc = -_softplus(-(acc + b_ref[...]))
    o_ref[...] = acc.astype(o_ref.dtype).reshape(o_ref.shape)


def _in_proj(x, sh, sc, g, w, bias, *, tm, tn, epilogue, out_dtype, h_rows, split=False):
    b, t, d = x.shape
    n = w.shape[1]
    assert t % tm == 0 and n % tn == 0
    if split:
        o_spec = pl.BlockSpec((1, 1, tm, tn), lambda bi, i, j: (j, bi, i, 0))
        o_shape = jax.ShapeDtypeStruct((n // tn, b, t, tn), out_dtype)
    else:
        o_spec = pl.BlockSpec((1, tm, tn), lambda bi, i, j: (bi, i, j))
        o_shape = jax.ShapeDtypeStruct((b, t, n), out_dtype)
    per_tok = sh.shape[1] != 1
    mrows = tm if per_tok else 1
    mod_map = (lambda bi, i, j: (bi, i, 0)) if per_tok else (lambda bi, i, j: (bi, 0, 0))
    if bias is None:
        bias = jnp.zeros((1, n), F32)
    out, h = pl.pallas_call(
        functools.partial(_in_proj_kernel, epilogue=epilogue, h_rows=h_rows),
        grid=(b, t // tm, n // tn),
        in_specs=[pl.BlockSpec((1, tm, d), lambda bi, i, j: (bi, i, 0)),
                  pl.BlockSpec((1, mrows, d), mod_map),
                  pl.BlockSpec((1, mrows, d), mod_map),
                  pl.BlockSpec((1, d), lambda bi, i, j: (0, 0)),
                  pl.BlockSpec((d, tn), lambda bi, i, j: (0, j)),
                  pl.BlockSpec((1, tn), lambda bi, i, j: (0, j))],
        out_specs=[o_spec, pl.BlockSpec((1, h_rows, d), lambda bi, i, j: (bi, 0, 0))],
        out_shape=[o_shape, jax.ShapeDtypeStruct((b, h_rows, d), F32)],
        scratch_shapes=[pltpu.VMEM((tm, d), BF16)],
        compiler_params=_cparams(("parallel", "arbitrary", "arbitrary")),
        name="in_proj_" + epilogue,
    )(x, sh, sc, g.reshape(1, d), w, bias)
    return out, h


def _prep_kernel(pr_ref, prev_ref, mu_ref, w0_ref, ww2_ref, a0_ref, wa2_ref, wg2_ref, kk_ref, ka_ref, rk_ref,
                 seg_ref, r_o, w_o, k_o, v_o, kk_o, kka_o, g_o, bonus_o, carry):
    @pl.when(pl.program_id(1) == 0)
    def _():
        carry[...] = prev_ref[0]

    pr = pr_ref[0]
    tt = pr.shape[0]
    first = lax.broadcasted_iota(jnp.int32, pr.shape, 0) == 0
    pprev = jnp.where(first, carry[...], pltpu.roll(pr, 1, axis=0))
    carry[...] = pr[tt - 1:tt, :]
    pm = pr + (pprev - pr) * mu_ref[...]
    r = pm[:, 0:WIDTH]
    k = pm[:, WIDTH:2 * WIDTH]
    v = pm[:, 2 * WIDTH:3 * WIDTH]
    lo = pm[:, 3 * WIDTH:3 * WIDTH + LANES]
    glo = pm[:, 3 * WIDTH + LANES:3 * WIDTH + 2 * LANES]
    seg = seg_ref[...]
    w_log = -_softplus(-(w0_ref[...] + _dot(jnp.tanh(lo).astype(BF16), ww2_ref[...]))) - 0.5
    decay = jnp.exp(-jnp.exp(w_log))
    a = _sigmoid(a0_ref[...] + _dot(lo.astype(BF16), wa2_ref[...]))
    g = _dot(_sigmoid(glo).astype(BF16), wg2_ref[...])
    kk = k * kk_ref[...]
    kk = kk / jnp.maximum(jnp.sqrt(_dot_x3(kk * kk, seg)), 1e-12)
    k2 = k * (1.0 + (a - 1.0) * ka_ref[...])
    for d in range(r_o.shape[0]):
        r_o[d, 0] = r
        w_o[d, 0] = decay
        k_o[d, 0] = k2
        kk_o[d, 0] = kk
        kka_o[d, 0] = kk * a
    v_o[0] = v
    g_o[0] = g
    bonus_o[0] = _dot_x3(r * k2 * rk_ref[...], seg) * v


def _seg_ones(width, seg):
    i = jnp.arange(width) // seg
    return (i[:, None] == i[None, :]).astype(BF16)


def _lane_dup(b):
    return max(1, LANES // (b * N_HEADS))


def _rwkv_prep(pr, prev, p, tt):
    b, t, _ = pr.shape
    dup = _lane_dup(b)
    zpad = jnp.zeros((LANES - DECAY_LORA, WIDTH), F32)
    ww2 = jnp.concatenate([p["w_w2"], zpad], axis=0).astype(BF16)
    wa2 = jnp.concatenate([zpad, p["w_a2"]], axis=0).astype(BF16)
    row = lambda a: a.reshape(1, -1)
    consts = [row(p["mu_shift"]), row(p["w0"]), ww2, row(p["a0"]), wa2, p["w_g2"].astype(BF16),
              row(p["k_k"]), row(p["k_a"]), row(p["r_k"]), _seg_ones(WIDTH, HEAD_DIM)]
    const_specs = [pl.BlockSpec(c.shape, lambda bi, i: (0, 0)) for c in consts]
    o_spec = pl.BlockSpec((1, tt, WIDTH), lambda bi, i: (bi, i, 0))
    k_spec = pl.BlockSpec((dup, 1, tt, WIDTH), lambda bi, i: (0, bi, i, 0))
    o_shape = jax.ShapeDtypeStruct((b, t, WIDTH), F32)
    k_shape = jax.ShapeDtypeStruct((dup, b, t, WIDTH), F32)
    r, dec, k2, v, kk, kka, g, bonus = pl.pallas_call(
        _prep_kernel,
        grid=(b, t // tt),
        in_specs=[pl.BlockSpec((1, tt, R_COLS), lambda bi, i: (bi, i, 0)),
                  pl.BlockSpec((1, 1, R_COLS), lambda bi, i: (bi, 0, 0))] + const_specs,
        out_specs=[k_spec, k_spec, k_spec, o_spec, k_spec, k_spec, o_spec, o_spec],
        out_shape=[k_shape, k_shape, k_shape, o_shape, k_shape, k_shape, o_shape, o_shape],
        scratch_shapes=[pltpu.VMEM((1, R_COLS), F32)],
        compiler_params=_cparams(("parallel", "arbitrary")),
        name="rwkv_prep",
    )(pr, prev.reshape(b, 1, R_COLS), *consts)
    return r, dec, k2, v, kk, kka, g, bonus


def _scan_kernel(kk_ref, w_ref, k_ref, kka_ref, r_ref, v_ref, s0_ref, y_ref, s_ref, *, steps, n_slabs):
    @pl.when(pl.program_id(0) == 0)
    def _():
        s_ref[...] = s0_ref[...]

    def step(t, carry):
        kk = kk_ref[t]
        w = w_ref[t]
        k = k_ref[t]
        kka = kka_ref[t]
        r = r_ref[t]
        rw = r * w
        c1 = jnp.sum(r * kka, axis=0, keepdims=True)
        c2 = jnp.sum(r * k, axis=0, keepdims=True)
        for n in range(n_slabs):
            s = s_ref[n]
            sa = -jnp.sum(s * kk, axis=0, keepdims=True)
            yp = jnp.sum(s * rw, axis=0, keepdims=True)
            vn = v_ref[t, pl.ds(n, 1), :]
            s_ref[n] = s * w + sa * kka + vn * k
            y_ref[t, pl.ds(n, 1), :] = yp + c1 * sa + c2 * vn
        return carry

    lax.fori_loop(0, steps, step, 0)


def _wkv_scan(r, w, k, v, kk, kka, s0, tc):
    b, t, _ = v.shape
    bh = b * N_HEADS
    dup = _lane_dup(b)
    lanes = dup * bh
    ni = HEAD_DIM // dup

    def key_layout(x):
        return x.reshape(dup, b, t, N_HEADS, HEAD_DIM).transpose(2, 4, 0, 1, 3).reshape(t, HEAD_DIM, lanes)

    v_l = v.reshape(b, t, N_HEADS, ni, dup).transpose(1, 3, 4, 0, 2).reshape(t, ni, lanes)
    s_l = s0.reshape(b, N_HEADS, ni, dup, HEAD_DIM).transpose(2, 4, 3, 0, 1).reshape(ni, HEAD_DIM, lanes)
    key_spec = pl.BlockSpec((tc, HEAD_DIM, lanes), lambda c: (c, 0, 0))
    val_spec = pl.BlockSpec((tc, ni, lanes), lambda c: (c, 0, 0))
    st_spec = pl.BlockSpec((ni, HEAD_DIM, lanes), lambda c: (0, 0, 0))
    y_l, s_out = pl.pallas_call(
        functools.partial(_scan_kernel, steps=tc, n_slabs=ni),
        grid=(t // tc,),
        in_specs=[key_spec] * 5 + [val_spec, st_spec],
        out_specs=[val_spec, st_spec],
        out_shape=[jax.ShapeDtypeStruct((t, ni, lanes), F32),
                   jax.ShapeDtypeStruct((ni, HEAD_DIM, lanes), F32)],
        compiler_params=_cparams(("arbitrary",)),
        name="wkv_scan",
    )(key_layout(kk), key_layout(w), key_layout(k), key_layout(kka), key_layout(r), v_l, s_l)
    y = y_l.reshape(t, ni, dup, b, N_HEADS).transpose(3, 0, 4, 1, 2).reshape(b, t, WIDTH)
    s_new = s_out.reshape(ni, HEAD_DIM, dup, b, N_HEADS).transpose(3, 4, 0, 2, 1).reshape(
        b, N_HEADS, HEAD_DIM, HEAD_DIM)
    return y, s_new


def _cumsum_kernel(x_ref, o_ref, carry):
    @pl.when(pl.program_id(1) == 0)
    def _():
        carry[...] = jnp.zeros_like(carry)

    x = x_ref[0]
    tb = x.shape[0]
    tril = (lax.broadcasted_iota(jnp.int32, (tb, tb), 1) <= lax.broadcasted_iota(jnp.int32, (tb, tb), 0)).astype(BF16)
    c = _dot_3x(tril, x) + carry[...]
    o_ref[0] = c
    carry[...] = c[tb - 1:tb, :]


def _cumsum_time(x, tb):
    b, t, n = x.shape
    spec = pl.BlockSpec((1, tb, n), lambda bi, i: (bi, i, 0))
    return pl.pallas_call(
        _cumsum_kernel, grid=(b, t // tb), in_specs=[spec], out_specs=spec,
        out_shape=jax.ShapeDtypeStruct((b, t, n), F32),
        scratch_shapes=[pltpu.VMEM((1, n), F32)],
        compiler_params=_cparams(("parallel", "arbitrary")),
        name="cumsum_time",
    )(x)


def _bias_lanes(x, col, lane, base, own, key_side):
    hi, mid, lo = _split3(col)
    first, second = (base + 3, base) if key_side else (base, base + 3)
    out = jnp.where(own, x, 0.0)
    out = jnp.where((lane >= second) & (lane < second + 3), 1.0, out)
    out = jnp.where(lane == first, hi.astype(F32), out)
    out = jnp.where(lane == first + 1, mid.astype(F32), out)
    return jnp.where(lane == first + 2, lo.astype(F32), out)


def _fox_prompt_kernel(q_ref, k_ref, v_ref, cq_ref, ck_ref, o_ref, kaug, vb, *, tq):
    i = pl.program_id(2)
    t = k_ref.shape[2]
    pair = pl.program_id(1)

    @pl.when(i == 0)
    def _():
        lane_k = lax.broadcasted_iota(jnp.int32, (t, LANES), 1)
        vb[...] = v_ref[0, 0].astype(BF16)
        k = k_ref[0, 0]
        for hh in range(2):
            own = (lane_k >= hh * HEAD_DIM) & (lane_k < (hh + 1) * HEAD_DIM)
            ck = jnp.sum(jnp.where(lane_k == 2 * pair + hh, ck_ref[0], 0.0), axis=-1, keepdims=True)
            kaug[hh] = _bias_lanes(k, -ck, lane_k, (1 - hh) * HEAD_DIM, own, True).astype(BF16)

    lane = lax.broadcasted_iota(jnp.int32, (tq, LANES), 1)
    q = q_ref[0, 0] * (HEAD_DIM ** -0.5)
    qa = []
    for hh in range(2):
        own = (lane >= hh * HEAD_DIM) & (lane < (hh + 1) * HEAD_DIM)
        cq = jnp.sum(jnp.where(lane == 2 * pair + hh, cq_ref[0], 0.0), axis=-1, keepdims=True)
        qa.append(_bias_lanes(q, cq, lane, (1 - hh) * HEAD_DIM, own, False).astype(BF16))
    causal = lax.broadcasted_iota(jnp.int32, (tq, tq), 1) <= lax.broadcasted_iota(jnp.int32, (tq, tq), 0)

    def block(j, carry, masked):
        start = pl.multiple_of(j * tq, tq)
        v_blk = vb[pl.ds(start, tq), :]
        out = []
        for hh in range(2):
            m, l, acc = carry[hh]
            s = _dot_nt(qa[hh], kaug[hh, pl.ds(start, tq), :])
            if masked:
                s = jnp.where(causal, s, -jnp.inf)
            m_new = jnp.maximum(m, jnp.max(s, axis=-1, keepdims=True))
            p = jnp.exp(s - m_new)
            alpha = jnp.exp(m - m_new)
            out.append((m_new, alpha * l + jnp.sum(p, axis=-1, keepdims=True),
                        alpha * acc + _dot(p.astype(BF16), v_blk)))
        return tuple(out)

    one = (jnp.full((tq, 1), -jnp.inf, F32), jnp.zeros((tq, 1), F32), jnp.zeros((tq, LANES), F32))
    carry = lax.fori_loop(0, i, lambda j, c: block(j, c, False), (one, one))
    (_, l0, a0), (_, l1, a1) = block(i, carry, True)
    o_ref[0] = jnp.where(lane < HEAD_DIM, a0 / l0, a1 / l1).astype(o_ref.dtype)


def _fox_prompt(qkv, cum, tq):
    _, b, t, _ = qkv.shape
    npair = WIDTH // LANES
    return pl.pallas_call(
        functools.partial(_fox_prompt_kernel, tq=tq),
        grid=(b, npair, t // tq),
        in_specs=[pl.BlockSpec((1, 1, tq, LANES), lambda bi, p, i: (0, bi, i, p)),
                  pl.BlockSpec((1, 1, t, LANES), lambda bi, p, i: (1, bi, 0, p)),
                  pl.BlockSpec((1, 1, t, LANES), lambda bi, p, i: (2, bi, 0, p)),
                  pl.BlockSpec((1, tq, LANES), lambda bi, p, i: (bi, i, 0)),
                  pl.BlockSpec((1, t, LANES), lambda bi, p, i: (bi, 0, 0))],
        out_specs=pl.BlockSpec((1, tq, LANES), lambda bi, p, i: (bi, i, p)),
        out_shape=jax.ShapeDtypeStruct((b, t, WIDTH), BF16),
        scratch_shapes=[pltpu.VMEM((2, t, LANES), BF16), pltpu.VMEM((t, LANES), BF16)],
        compiler_params=_cparams(("parallel", "parallel", "arbitrary")),
        name="fox_prompt",
    )(qkv, qkv, qkv, cum, cum)


def _page_sums_kernel(lf_ref, rev_ref, tot_ref):
    n = lf_ref.shape[1]
    a = lax.broadcasted_iota(jnp.int32, (n, n), 0)
    c = lax.broadcasted_iota(jnp.int32, (n, n), 1)
    hi, mid, lo = _split3(lf_ref[...])
    later = (a > c).astype(BF16)
    rev_ref[...] = _dot(hi, later) + _dot(mid, later) + _dot(lo, later)
    every = jnp.ones((n, n), BF16)
    tot_ref[...] = _dot(hi, every) + _dot(mid, every) + _dot(lo, every)


def _page_sums(lf_rows, rows):
    n_rows, n = lf_rows.shape
    spec = pl.BlockSpec((rows, n), lambda i: (i, 0))
    return pl.pallas_call(
        _page_sums_kernel, grid=(n_rows // rows,), in_specs=[spec], out_specs=[spec, spec],
        out_shape=[jax.ShapeDtypeStruct((n_rows, n), F32)] * 2,
        compiler_params=_cparams(("parallel",)),
        name="page_sums",
    )(lf_rows)


def _fox_sample_kernel(pt_ref, q_ref, kn_ref, vn_ref, lfn_ref, *rest, pages_per_step):
    npp = pages_per_step
    k_refs = rest[:npp]
    v_refs = rest[npp:2 * npp]
    rev_refs = rest[2 * npp:3 * npp]
    tot_refs = rest[3 * npp:4 * npp]
    o_ref = rest[4 * npp]
    qrep, m_s, l_s, acc_s, suf_s, cn_s = rest[4 * npp + 1:]
    step = pl.program_id(1)
    nq = q_ref.shape[1]
    rows = N_HEADS * nq
    page = kn_ref.shape[1]
    row_head = lax.broadcasted_iota(jnp.int32, (rows, WIDTH), 0) // nq
    lane_head = lax.broadcasted_iota(jnp.int32, (rows, WIDTH), 1) // HEAD_DIM

    def rep_heads(x):
        return jnp.concatenate([jnp.broadcast_to(x[h:h + 1, :], (nq, x.shape[1])) for h in range(N_HEADS)], axis=0)

    def update(s_list, pv):
        m_old = m_s[...]
        m_new = functools.reduce(jnp.maximum, [jnp.max(s, axis=-1, keepdims=True) for s in s_list] + [m_old])
        p_list = [jnp.exp(s - m_new) for s in s_list]
        alpha = jnp.exp(m_old - m_new)
        l_s[...] = alpha * l_s[...] + functools.reduce(jnp.add, [jnp.sum(p, axis=-1, keepdims=True) for p in p_list])
        acc_s[...] = alpha * acc_s[...] + functools.reduce(jnp.add, [pv(u, p.astype(BF16)) for u, p in enumerate(p_list)])
        m_s[...] = m_new

    @pl.when(step == 0)
    def _():
        q = q_ref[0] * (HEAD_DIM ** -0.5)
        qrep[...] = jnp.where(row_head == lane_head, jnp.concatenate([q] * N_HEADS, axis=0), 0.0).astype(BF16)
        key_i = lax.broadcasted_iota(jnp.int32, (page, page), 0)
        key_j = lax.broadcasted_iota(jnp.int32, (page, page), 1)
        cn_row = _dot_x3(rep_heads(lfn_ref[0]), (key_i <= key_j).astype(BF16))
        rq = lax.broadcasted_iota(jnp.int32, (rows, page), 0) % nq
        kc = lax.broadcasted_iota(jnp.int32, (rows, page), 1)
        cn_col = jnp.sum(jnp.where(kc == rq, cn_row, 0.0), axis=-1, keepdims=True)
        cn_s[...] = cn_col
        suf_s[...] = jnp.zeros_like(suf_s)
        m_s[...] = jnp.full_like(m_s, -jnp.inf)
        l_s[...] = jnp.zeros_like(l_s)
        acc_s[...] = jnp.zeros_like(acc_s)
        s = _dot_nt(qrep[...], kn_ref[0].astype(BF16)) + cn_col - cn_row
        vn = vn_ref[0].astype(BF16)
        update([jnp.where(kc <= rq, s, -jnp.inf)], lambda u, p: _dot(p, vn))

    q_all = qrep[...]
    cn = cn_s[...]
    suf = suf_s[...]
    s_list = []
    for u in range(npp):
        kt = k_refs[u][0].reshape(WIDTH, page).astype(BF16)
        s_list.append(_dot(q_all, kt) + ((cn + suf) + rep_heads(rev_refs[u][0])))
        suf = suf + rep_heads(tot_refs[u][0])
    update(s_list, lambda u, p: _dot_nt(p, v_refs[u][0].reshape(WIDTH, page).astype(BF16)))
    suf_s[...] = suf

    @pl.when(step == pl.num_programs(1) - 1)
    def _():
        o_sel = jnp.where(row_head == lane_head, acc_s[...] / l_s[...], 0.0)
        out = o_sel[0:nq]
        for h in range(1, N_HEADS):
            out = out + o_sel[h * nq:(h + 1) * nq]
        o_ref[0] = out.astype(o_ref.dtype)


def _fox_sample(q, k_new, v_new, logf_new, cache_k, cache_v, cache_logf, page_table, pages_per_step):
    b, tn, _ = q.shape
    n_pool, page = cache_k.shape[:2]
    n_pages = page_table.shape[1]
    npp = pages_per_step
    assert n_pages % npp == 0 and tn <= page
    rows = N_HEADS * tn
    ck = cache_k.transpose(0, 2, 3, 1)
    cv = cache_v.transpose(0, 2, 3, 1)
    clf = cache_logf.transpose(0, 2, 1).reshape(n_pool * N_HEADS, page)
    sum_rows = 2048 if clf.shape[0] % 2048 == 0 else clf.shape[0]
    rev, tot = (a.reshape(n_pool, N_HEADS, page) for a in _page_sums(clf, sum_rows))
    pad_rows = lambda x: jnp.pad(x, ((0, 0), (0, page - tn), (0, 0)))
    lfn = jnp.pad(logf_new.transpose(0, 2, 1), ((0, 0), (0, 0), (0, page - tn)))

    def page_map(u, nd):
        return lambda bi, s, pt: (pt[bi, n_pages - 1 - (s * npp + u)],) + (0,) * nd

    tok_spec = lambda r: pl.BlockSpec((1, r, WIDTH), lambda bi, s, pt: (bi, 0, 0))
    in_specs = ([tok_spec(tn), tok_spec(page), tok_spec(page),
                 pl.BlockSpec((1, N_HEADS, page), lambda bi, s, pt: (bi, 0, 0))]
                + [pl.BlockSpec((1, N_HEADS, HEAD_DIM, page), page_map(u, 3)) for u in range(npp)] * 2
                + [pl.BlockSpec((1, N_HEADS, page), page_map(u, 2)) for u in range(npp)] * 2)
    grid_spec = pltpu.PrefetchScalarGridSpec(
        num_scalar_prefetch=1,
        grid=(b, n_pages // npp),
        in_specs=in_specs,
        out_specs=tok_spec(tn),
        scratch_shapes=[pltpu.VMEM((rows, WIDTH), BF16), pltpu.VMEM((rows, 1), F32), pltpu.VMEM((rows, 1), F32),
                        pltpu.VMEM((rows, WIDTH), F32), pltpu.VMEM((rows, page), F32), pltpu.VMEM((rows, 1), F32)])
    return pl.pallas_call(
        functools.partial(_fox_sample_kernel, pages_per_step=npp),
        grid_spec=grid_spec,
        out_shape=jax.ShapeDtypeStruct((b, tn, WIDTH), BF16),
        compiler_params=_cparams(("parallel", "arbitrary")),
        name="fox_sample",
    )(page_table, q, pad_rows(k_new), pad_rows(v_new), lfn, *([ck] * npp), *([cv] * npp), *([rev] * npp),
      *([tot] * npp))


def _merge_kernel(y_ref, bonus_ref, g_ref, yf_ref, gates_ref, x_ref, g1_ref, sh2_ref, sc2_ref,
                  lnw_ref, lnb_ref, seg_ref, wr_ref, wf_ref, wo_ref, n2_ref, wrt_ref,
                  x1_ref, h2_ref, lg_ref):
    seg = seg_ref[...]
    y = y_ref[0]
    mu = _dot_x3(y, seg) * (1.0 / HEAD_DIM)
    d = y - mu
    var = _dot_x3(d * d, seg) * (1.0 / HEAD_DIM)
    yn = d * lax.rsqrt(var + LNX_EPS) * lnw_ref[...] + lnb_ref[...]
    yr = ((yn + bonus_ref[0]) * g_ref[0]).astype(BF16)
    gates = gates_ref[0].astype(F32)
    d_model = x_ref.shape[2]
    merged = gates[:, :d_model] * _dot(yr, wr_ref[...]) + gates[:, d_model:] * _dot(yf_ref[0], wf_ref[...])
    x1 = x_ref[0] + g1_ref[0] * _dot(merged.astype(BF16), wo_ref[...])
    x1_ref[0] = x1
    h2 = _rmsnorm(x1, n2_ref[...]) * (1.0 + sc2_ref[0]) + sh2_ref[0]
    h2_ref[0] = h2
    hh, hl = _split2(h2)
    wrt = wrt_ref[...]
    lg_ref[0] = _dot(hh, wrt[0]) + (_dot(hh, wrt[1]) + _dot(hl, wrt[0]))


def _merge(y, bonus, g, yf, gates, x, g1, sh2, sc2, p, w_router_t, tm):
    b, t, d = x.shape
    per_tok = g1.shape[1] != 1
    mrows = tm if per_tok else 1
    mod_map = (lambda bi, i: (bi, i, 0)) if per_tok else (lambda bi, i: (bi, 0, 0))
    tok = lambda n: pl.BlockSpec((1, tm, n), lambda bi, i: (bi, i, 0))
    mod = pl.BlockSpec((1, mrows, d), mod_map)
    row = lambda a: a.reshape(1, -1)
    consts = [row(p["lnx_w"]), row(p["lnx_b"]), _seg_ones(WIDTH, HEAD_DIM), p["w_br_r"].astype(BF16),
              p["w_br_f"].astype(BF16), p["w_out"].astype(BF16), row(p["norm2_g"]), w_router_t]
    const_specs = [pl.BlockSpec(c.shape, (lambda bi, i: (0, 0)) if c.ndim == 2 else (lambda bi, i: (0, 0, 0)))
                   for c in consts]
    return pl.pallas_call(
        _merge_kernel,
        grid=(b, t // tm),
        in_specs=[tok(WIDTH), tok(WIDTH), tok(WIDTH), tok(WIDTH), tok(2 * d), tok(d), mod, mod, mod] + const_specs,
        out_specs=[tok(d), tok(d), tok(LANES)],
        out_shape=[jax.ShapeDtypeStruct((b, t, d), F32), jax.ShapeDtypeStruct((b, t, d), F32),
                   jax.ShapeDtypeStruct((b, t, LANES), F32)],
        compiler_params=_cparams(("parallel", "parallel")),
        name="merge",
    )(y, bonus, g, yf, gates, x, g1, sh2, sc2, *consts)


def _route_kernel(lg_ref, bias_ref, row_ref, rowt_ref, wt_ref, cnt_ref):
    lt = lg_ref[...].T
    tm = lt.shape[1]
    score = _sigmoid(lt[:N_EXPERTS])
    biased = score + bias_ref[...][:N_EXPERTS]
    slab = [biased[k * N_GROUPS:(k + 1) * N_GROUPS] for k in range(GROUP_SIZE)]
    neg = jnp.full((N_GROUPS, tm), -jnp.inf, F32)
    m1 = functools.reduce(jnp.maximum, slab)
    taken = jnp.zeros((N_GROUPS, tm), jnp.bool_)
    m2 = neg
    for k in range(GROUP_SIZE):
        is_first = (slab[k] == m1) & jnp.logical_not(taken)
        taken = taken | is_first
        m2 = jnp.maximum(m2, jnp.where(is_first, neg, slab[k]))
    gs = m1 + m2
    g_iota = lax.broadcasted_iota(jnp.int32, (N_GROUPS, tm), 0)
    cnt = jnp.zeros((N_GROUPS, tm), jnp.int32)
    for g2 in range(N_GROUPS):
        other = gs[g2:g2 + 1, :]
        beats = (other > gs) | ((g_iota > g2) & (other == gs))
        cnt = cnt + beats.astype(jnp.int32)
    g_sel = cnt < TOPK_GROUPS
    cand = [jnp.where(g_sel, slab[k], neg) for k in range(GROUP_SIZE)]
    rank = [jnp.zeros((N_GROUPS, tm), jnp.int32) for _ in range(GROUP_SIZE)]
    for k2 in range(GROUP_SIZE):
        for g2 in range(N_GROUPS):
            other = cand[k2][g2:g2 + 1, :]
            for k in range(GROUP_SIZE):
                first = (g_iota >= g2) if k2 < k else (g_iota > g2)
                beats = (other > cand[k]) | (first & (other == cand[k]))
                rank[k] = rank[k] + beats.astype(jnp.int32)
    sel = [rank[k] < TOP_K for k in range(GROUP_SIZE)]
    sc = [score[k * N_GROUPS:(k + 1) * N_GROUPS] for k in range(GROUP_SIZE)]
    picked = [jnp.where(sel[k], sc[k], 0.0) for k in range(GROUP_SIZE)]
    total = jnp.sum(functools.reduce(jnp.add, picked), axis=0, keepdims=True)
    gate = jnp.concatenate([pk / total * ROUTED_SCALE for pk in picked], axis=0)
    chosen = jnp.concatenate([s.astype(F32) for s in sel], axis=0)
    chosen_b = chosen.astype(BF16)
    ri = lax.broadcasted_iota(jnp.int32, (N_EXPERTS, N_EXPERTS), 0)
    ci = lax.broadcasted_iota(jnp.int32, (N_EXPERTS, N_EXPERTS), 1)
    lower = (ci < ri).astype(BF16)
    ordinal = _dot(lower, chosen_b)
    ta = lax.broadcasted_iota(jnp.int32, (tm, tm), 0)
    tc = lax.broadcasted_iota(jnp.int32, (tm, tm), 1)
    rank_tok = _dot(chosen_b, (ta < tc).astype(BF16))
    count = jnp.sum(chosen, axis=1, keepdims=True)
    runs = jnp.broadcast_to(jnp.floor((count + (RUN - 1)) * (1.0 / RUN)), (N_EXPERTS, LANES))
    cnt_ref[0] = runs
    start = _dot(lower, runs.astype(BF16))[:, 0:1] * float(RUN)
    tile_row = start + rank_tok
    r_rows, w_rows = [], []
    for n in range(TOP_K):
        hit = (chosen > 0.0) & (ordinal == float(n))
        r_rows.append(jnp.sum(jnp.where(hit, tile_row, 0.0), axis=0, keepdims=True))
        w_rows.append(jnp.sum(jnp.where(hit, gate, 0.0), axis=0, keepdims=True))
    zrow = jnp.zeros((1, tm), F32)
    pad8 = lambda rows: jnp.concatenate(rows + [zrow] * (8 - TOP_K), axis=0)
    zpad = jnp.zeros((LANES - 8, tm), F32)
    row_ref[...] = pad8(r_rows).astype(jnp.int32)
    rowt_ref[...] = jnp.concatenate([pad8(r_rows), zpad], axis=0).T.astype(jnp.int32)
    wt_ref[...] = jnp.concatenate([pad8(w_rows), zpad], axis=0).T


def _route(logits, e_bias_perm, tm):
    m = logits.shape[0]
    bias = jnp.pad(e_bias_perm, (0, LANES - N_EXPERTS)).reshape(LANES, 1)
    tok_spec = pl.BlockSpec((tm, LANES), lambda i: (i, 0))
    return pl.pallas_call(
        _route_kernel,
        grid=(m // tm,),
        in_specs=[tok_spec, pl.BlockSpec((LANES, 1), lambda i: (0, 0))],
        out_specs=[pl.BlockSpec((8, tm), lambda i: (0, i)), tok_spec, tok_spec,
                   pl.BlockSpec((1, N_EXPERTS, LANES), lambda i: (i, 0, 0))],
        out_shape=[jax.ShapeDtypeStruct((8, m), jnp.int32), jax.ShapeDtypeStruct((m, LANES), jnp.int32),
                   jax.ShapeDtypeStruct((m, LANES), F32), jax.ShapeDtypeStruct((m // tm, N_EXPERTS, LANES), F32)],
        compiler_params=_cparams(("parallel",)),
        name="route",
    )(logits, bias)


def _sorted_capacity(tm):
    rows = TOP_K * tm + N_EXPERTS * (RUN - 1)
    return -(-rows // SORT_BLOCK) * SORT_BLOCK


def _run_pieces(i, nrun_ref, src_ref, dst_ref, piece):
    def per_expert(e, total):
        j = i * N_EXPERTS + e
        n, s0, d0 = nrun_ref[j], src_ref[j], dst_ref[j]

        def one(c, carry):
            piece(s0 + c, d0 + c).start()
            return carry

        lax.fori_loop(0, n, one, 0)
        return total + n

    return lax.fori_loop(0, N_EXPERTS, per_expert, 0)


def _sort_kernel(nrun_ref, src_ref, dst_ref, nblk_ref, row_ref, h_ref, xs_in_ref, xs_ref, buf, sem):
    del xs_in_ref
    i = pl.program_id(0)
    tm = h_ref.shape[0]
    hb = h_ref[...].astype(BF16)
    rows = row_ref[...]
    riota = lax.broadcasted_iota(jnp.int32, (SORT_BLOCK, tm), 0)

    def block(b, carry):
        r0 = pl.multiple_of(b * SORT_BLOCK, SORT_BLOCK)
        hit = riota + r0 == rows[0:1, :]
        for n in range(1, TOP_K):
            hit = hit | (riota + r0 == rows[n:n + 1, :])
        buf[pl.ds(r0, SORT_BLOCK), :] = _dot(hit.astype(BF16), hb).astype(BF16)
        return carry

    lax.fori_loop(0, nblk_ref[i], block, 0)

    def piece(s, d):
        return pltpu.make_async_copy(buf.at[pl.ds(pl.multiple_of(s * RUN, RUN), RUN), :],
                                     xs_ref.at[pl.ds(pl.multiple_of(d * RUN, RUN), RUN), :], sem)

    started = _run_pieces(i, nrun_ref, src_ref, dst_ref, piece)

    def drain(c, carry):
        piece(0, 0).wait()
        return carry

    lax.fori_loop(0, started, drain, 0)


def _sort_rows(h2, row, tables, n_rows, tm):
    m, d = h2.shape
    grid_spec = pltpu.PrefetchScalarGridSpec(
        num_scalar_prefetch=4,
        grid=(m // tm,),
        in_specs=[pl.BlockSpec((8, tm), lambda i, *_: (0, i)), pl.BlockSpec((tm, d), lambda i, *_: (i, 0)),
                  pl.BlockSpec(memory_space=pl.ANY)],
        out_specs=pl.BlockSpec(memory_space=pl.ANY),
        scratch_shapes=[pltpu.VMEM((_sorted_capacity(tm), d), BF16), pltpu.SemaphoreType.DMA(())])
    return pl.pallas_call(
        _sort_kernel,
        grid_spec=grid_spec,
        out_shape=jax.ShapeDtypeStruct((n_rows, d), BF16),
        input_output_aliases={6: 0},
        compiler_params=_cparams(("arbitrary",)),
        name="sort_rows",
    )(*tables, row, h2, jnp.zeros((n_rows, d), BF16))


def _expert_kernel(te_ref, nu_ref, x_ref, wg_ref, wu_ref, wd_ref, o_ref, wgb, wub, wdb):
    i = pl.program_id(0)
    changed = (i == 0) | (te_ref[i] != te_ref[jnp.maximum(i - 1, 0)])

    @pl.when(changed)
    def _():
        wgb[...] = wg_ref[0].astype(BF16)
        wub[...] = wu_ref[0].astype(BF16)
        wdb[...] = wd_ref[0].astype(BF16)

    @pl.when(i < nu_ref[0])
    def _():
        x = x_ref[...]
        act = _silu(_dot(x, wgb[...])) * _dot(x, wub[...])
        o_ref[...] = _dot(act.astype(BF16), wdb[...]).astype(o_ref.dtype)

    @pl.when(i >= nu_ref[0])
    def _():
        o_ref[...] = jnp.zeros_like(o_ref)


def _expert_tiles(xs, tile_expert, n_used, wg, wu, wd, te):
    n_rows, d = xs.shape
    ff = wg.shape[2]
    last = lambda i, nu: jnp.minimum(i, nu[0] - 1)
    grid_spec = pltpu.PrefetchScalarGridSpec(
        num_scalar_prefetch=2,
        grid=(n_rows // te,),
        in_specs=[pl.BlockSpec((te, d), lambda i, tx, nu: (last(i, nu), 0)),
                  pl.BlockSpec((1, d, ff), lambda i, tx, nu: (tx[i], 0, 0)),
                  pl.BlockSpec((1, d, ff), lambda i, tx, nu: (tx[i], 0, 0)),
                  pl.BlockSpec((1, ff, d), lambda i, tx, nu: (tx[i], 0, 0))],
        out_specs=pl.BlockSpec((te, d), lambda i, tx, nu: (i, 0)),
        scratch_shapes=[pltpu.VMEM((d, ff), BF16), pltpu.VMEM((d, ff), BF16), pltpu.VMEM((ff, d), BF16)])
    return pl.pallas_call(
        _expert_kernel,
        grid_spec=grid_spec,
        out_shape=jax.ShapeDtypeStruct((n_rows, d), BF16),
        compiler_params=_cparams(("arbitrary",)),
        name="expert_tiles",
    )(tile_expert, n_used, xs, wg, wu, wd)


def _combine_kernel(nrun_ref, src_ref, dst_ref, nblk_ref, rowt_ref, wt_ref, h_ref, x1_ref, g2_ref, nf_ref,
                    wsg_ref, wsu_ref, wsd_ref, os_ref, y_ref, buf, rowb, wb, sem):
    i = pl.program_id(0) * pl.num_programs(1) + pl.program_id(1)
    tm = h_ref.shape[1]
    nblk = nblk_ref[i]
    tail = pl.multiple_of((nblk - 1) * SORT_BLOCK, SORT_BLOCK)
    buf[pl.ds(tail, SORT_BLOCK), :] = jnp.zeros((SORT_BLOCK, buf.shape[1]), BF16)

    def piece(s, d):
        return pltpu.make_async_copy(os_ref.at[pl.ds(pl.multiple_of(d * RUN, RUN), RUN), :],
                                     buf.at[pl.ds(pl.multiple_of(s * RUN, RUN), RUN), :], sem)

    started = _run_pieces(i, nrun_ref, src_ref, dst_ref, piece)
    h = h_ref[0].astype(BF16)
    shared = _dot((_silu(_dot(h, wsg_ref[...])) * _dot(h, wsu_ref[...])).astype(BF16), wsd_ref[...])

    def drain(c, carry):
        piece(0, 0).wait()
        return carry

    w = wt_ref[0]
    rt = rowt_ref[0]
    for n in range(TOP_K):
        rowb[n] = jnp.broadcast_to(rt[:, n:n + 1], (tm, SORT_BLOCK))
        wb[n] = jnp.broadcast_to(w[:, n:n + 1], (tm, SORT_BLOCK))
    lax.fori_loop(0, started, drain, 0)
    ciota = lax.broadcasted_iota(jnp.int32, (tm, SORT_BLOCK), 1)

    def block(b, acc):
        r0 = pl.multiple_of(b * SORT_BLOCK, SORT_BLOCK)
        c = jnp.where(ciota == rowb[0] - r0, wb[0], 0.0)
        for n in range(1, TOP_K):
            c = c + jnp.where(ciota == rowb[n] - r0, wb[n], 0.0)
        return acc + _dot(c.astype(BF16), buf[pl.ds(r0, SORT_BLOCK), :])

    routed = lax.fori_loop(0, nblk, block, jnp.zeros((tm, buf.shape[1]), F32))
    x2 = x1_ref[0] + g2_ref[0] * (routed + shared)
    y_ref[0] = _rmsnorm(x2, nf_ref[...])


def _combine(os, rowt, wt, tables, h2, x1, g2, normf_g, wsg, wsu, wsd, tm):
    b, t, d = x1.shape
    nt = t // tm
    per_tok = g2.shape[1] != 1
    mrows = tm if per_tok else 1
    mod_map = (lambda bi, i, *_: (bi, i, 0)) if per_tok else (lambda bi, i, *_: (bi, 0, 0))
    tok = lambda n: pl.BlockSpec((1, tm, n), lambda bi, i, *_: (bi, i, 0))
    const = lambda a: pl.BlockSpec(a.shape, lambda bi, i, *_: (0, 0))
    grid_spec = pltpu.PrefetchScalarGridSpec(
        num_scalar_prefetch=4,
        grid=(b, nt),
        in_specs=[tok(LANES), tok(LANES), tok(d), tok(d), pl.BlockSpec((1, mrows, d), mod_map),
                  pl.BlockSpec((1, d), lambda bi, i, *_: (0, 0)), const(wsg), const(wsu), const(wsd),
                  pl.BlockSpec(memory_space=pl.ANY)],
        out_specs=tok(d),
        scratch_shapes=[pltpu.VMEM((_sorted_capacity(tm), d), BF16), pltpu.VMEM((TOP_K, tm, SORT_BLOCK), jnp.int32),
                        pltpu.VMEM((TOP_K, tm, SORT_BLOCK), F32), pltpu.SemaphoreType.DMA(())])
    return pl.pallas_call(
        _combine_kernel,
        grid_spec=grid_spec,
        out_shape=jax.ShapeDtypeStruct((b, t, d), F32),
        compiler_params=_cparams(("arbitrary", "arbitrary")),
        name="combine",
    )(*tables, rowt.reshape(b, t, LANES), wt.reshape(b, t, LANES), h2, x1, g2, normf_g.reshape(1, d), wsg, wsu, wsd, os)


def _moe(h2, logits, x1, g2, p, w, cfg):
    bx, tx, d = x1.shape
    m = bx * tx
    te, tm = cfg["te"], cfg["tm_route"]
    row, rowt, wt, cnt = _route(logits.reshape(m, LANES), w["e_bias_perm"], tm)
    runs = cnt[:, :, 0].astype(jnp.int32)
    src = jnp.cumsum(runs, axis=1) - runs
    nblk = (jnp.sum(runs, axis=1) * RUN + SORT_BLOCK - 1) // SORT_BLOCK
    per_expert = jnp.sum(runs, axis=0)
    tiles_e = (per_expert * RUN + te - 1) // te
    ends = jnp.cumsum(tiles_e)
    dst = ((ends - tiles_e) * (te // RUN))[None, :] + jnp.cumsum(runs, axis=0) - runs
    n_tiles = (m * TOP_K + (m // tm) * N_EXPERTS * (RUN - 1)) // te + N_EXPERTS
    n_used = ends[-1:].astype(jnp.int32)
    tile_ids = jnp.minimum(jnp.arange(n_tiles, dtype=jnp.int32), n_used[0] - 1)
    tile_row = jnp.minimum(jnp.sum(ends[None, :] <= tile_ids[:, None], axis=1), N_EXPERTS - 1)
    tile_expert = ((tile_row % N_GROUPS) * GROUP_SIZE + tile_row // N_GROUPS).astype(jnp.int32)
    flat = lambda a: a.reshape(-1).astype(jnp.int32)
    tables = (flat(runs), flat(src), flat(dst), flat(nblk))
    xs = _sort_rows(h2.reshape(m, d), row, tables, n_tiles * te, tm)
    os = _expert_tiles(xs, tile_expert, n_used, p["w_exp_gate"], p["w_exp_up"], p["w_exp_down"], te)
    return _combine(os, rowt, wt, tables, h2, x1, g2, p["normf_g"], w["sh_gate"], w["sh_up"], w["sh_down"], tm)


def _layer(x, mod, shift_prev, wkv0, attend, p, w, cfg):
    b, t, d = x.shape
    bx, tx = cfg["rows"]
    tm = cfg["tm"]
    xr = x.reshape(bx, tx, d)
    if bx == b:
        part = lambda i: mod[:, i:i + 1, :]
    else:
        part = lambda i: jnp.repeat(mod[:, i, :], t, axis=0).reshape(bx, tx, d)
    sh1, sc1, g1, sh2, sc2, g2 = (part(i) for i in range(6))
    h_rows = 1 if bx == b else tx
    proj = functools.partial(_in_proj, xr, sh1, sc1, p["norm1_g"], tm=cfg["tm_in"], h_rows=h_rows)
    pr, h_keep = proj(w["in_r"], None, tn=cfg["tn_r"], epilogue="none", out_dtype=F32)
    qkv, _ = proj(w["in_qkv"], None, tn=WIDTH, epilogue="none", out_dtype=F32, split=True)
    logf_pad, _ = proj(w["in_f"], w["b_f_pad"], tn=LANES, epilogue="log_sigmoid", out_dtype=F32)
    gates, _ = proj(w["in_g"], None, tn=512, epilogue="sigmoid", out_dtype=BF16)
    shift_new = h_keep[:, 0, :] if bx == b else h_keep.reshape(b, t, d)[:, -1, :]
    logf = logf_pad.reshape(b, t, LANES)[:, :, :N_HEADS]
    qkv = qkv.reshape(3, b, t, WIDTH)

    prev = _dense(shift_prev, w["in_r_f32"], jnp.zeros((R_COLS,), F32), act=False, tn=R_COLS // 2)
    r, dec, k2, v, kk, kka, g, bonus = _rwkv_prep(pr.reshape(b, t, R_COLS), prev, p, cfg["tt"])
    y_scan, wkv_new = _wkv_scan(r, dec, k2, v, kk, kka, wkv0, cfg["tc"])

    y_f = attend(qkv, logf)

    rs = lambda a: a.reshape(bx, tx, a.shape[-1])
    x1, h2, logits = _merge(rs(y_scan), rs(bonus), rs(g), rs(y_f), gates, xr, g1, sh2, sc2, p, w["router_t"], tm)
    y = _moe(h2, logits, x1, g2, p, w, cfg)
    k_out = qkv[1].reshape(b, t, N_HEADS, HEAD_DIM)
    v_out = qkv[2].reshape(b, t, N_HEADS, HEAD_DIM)
    return y.reshape(b, t, d), k_out, v_out, logf, wkv_new, shift_new


def kernel(x_prompt, x_sample, c_prompt, c_sample, cache_k, cache_v, cache_logf, page_table, state_wkv, state_shift, w_ada, b_ada, norm1_g, w_in, mu_shift, w0, w_w2, a0, w_a2, w_g2, k_k, k_a, r_k, lnx_w, lnx_b, b_f, w_br_r, w_br_f, w_out, norm2_g, w_router, e_bias, w_exp_gate, w_exp_up, w_exp_down, w_sh_gate, w_sh_up, w_sh_down, normf_g):
    p = dict(norm1_g=norm1_g, mu_shift=mu_shift, w0=w0, w_w2=w_w2, a0=a0, w_a2=w_a2, w_g2=w_g2, k_k=k_k, k_a=k_a,
             r_k=r_k, lnx_w=lnx_w, lnx_b=lnx_b, w_br_r=w_br_r, w_br_f=w_br_f, w_out=w_out, norm2_g=norm2_g,
             normf_g=normf_g, w_exp_gate=w_exp_gate, w_exp_up=w_exp_up, w_exp_down=w_exp_down)
    bp, tp, d = x_prompt.shape
    bs, ts, _ = x_sample.shape
    off_f = R_COLS + 3 * WIDTH
    off_g = off_f + N_HEADS
    perm = lambda a: a.reshape(a.shape[:-1] + (N_GROUPS, GROUP_SIZE)).swapaxes(-1, -2).reshape(a.shape)
    router = jnp.pad(perm(w_router), ((0, 0), (0, LANES - N_EXPERTS)))
    r_hi = router.astype(BF16)
    w = dict(
        in_r=w_in[:, :R_COLS].astype(BF16), in_r_f32=w_in[:, :R_COLS],
        in_qkv=w_in[:, R_COLS:off_f].astype(BF16),
        in_f=jnp.pad(w_in[:, off_f:off_g], ((0, 0), (0, LANES - N_HEADS))).astype(BF16),
        in_g=w_in[:, off_g:].astype(BF16),
        b_f_pad=jnp.pad(b_f, (0, LANES - N_HEADS)).reshape(1, LANES),
        router_t=jnp.stack([r_hi, (router - r_hi.astype(F32)).astype(BF16)]),
        e_bias_perm=perm(e_bias),
        sh_gate=w_sh_gate.astype(BF16), sh_up=w_sh_up.astype(BF16), sh_down=w_sh_down.astype(BF16),
    )
    mod = _dense(jnp.concatenate([c_prompt, c_sample], axis=0), w_ada, b_ada, act=True).reshape(bp + bs, 6, d)

    def attend_prompt(qkv, logf):
        lf = jnp.pad(logf, ((0, 0), (0, 0), (0, LANES - N_HEADS)))
        return _fox_prompt(qkv, _cumsum_time(lf, 256), min(512, tp))

    def attend_sample(qkv, logf):
        return _fox_sample(qkv[0], qkv[1], qkv[2], logf,
                           cache_k, cache_v, cache_logf, page_table, min(16, page_table.shape[1]))

    cfg_p = dict(rows=(bp, tp), tm=min(512, tp), tm_in=min(1024, tp), tn_r=R_COLS // 2, tt=min(256, tp), tc=16,
                 tm_route=min(512, tp), te=512)
    cfg_s = dict(rows=(1, bs * ts), tm=bs * ts, tm_in=bs * ts, tn_r=R_COLS // 2, tt=ts, tc=ts, tm_route=bs * ts,
                 te=256)
    yp, kp, vp, lfp, wkvp, shp = _layer(x_prompt, mod[:bp], jnp.zeros((bp, d), F32),
                                        jnp.zeros((bp, N_HEADS, HEAD_DIM, HEAD_DIM), F32), attend_prompt, p, w, cfg_p)
    ys, ks, vs, lfs, wkvs, shs = _layer(x_sample, mod[bp:], state_shift, state_wkv, attend_sample, p, w, cfg_s)
    return (yp, ys, kp, vp, lfp, wkvp, shp, ks, vs, lfs, wkvs, shs)
```

```python
import functools

import jax
import jax.numpy as jnp
from jax import lax
from jax.experimental import pallas as pl
from jax.experimental.pallas import tpu as pltpu

F32 = jnp.float32
BF16 = jnp.bfloat16

HEAD_DIM = 64
N_HEADS = 8
WIDTH = N_HEADS * HEAD_DIM
DECAY_LORA = 64
AAA_LORA = 64
GATE_LORA = 128
R_COLS = 3 * WIDTH + DECAY_LORA + AAA_LORA + GATE_LORA
LNX_EPS = 64e-5
NORM_EPS = 1e-6
N_EXPERTS = 64
N_GROUPS = 8
GROUP_SIZE = N_EXPERTS // N_GROUPS
TOPK_GROUPS = 4
TOP_K = 6
ROUTED_SCALE = 2.5
LANES = 128
RUN = 16
SORT_BLOCK = 512
VMEM_LIMIT = 56 * 1024 * 1024


def _cparams(sem):
    return pltpu.CompilerParams(dimension_semantics=sem, vmem_limit_bytes=VMEM_LIMIT)


def _dot(a, b):
    return jnp.dot(a, b, preferred_element_type=F32)


def _dot_nt(a, b):
    return lax.dot_general(a, b, (((1,), (1,)), ((), ())), preferred_element_type=F32)


def _split2(x):
    hi = x.astype(BF16)
    lo = (x - hi.astype(F32)).astype(BF16)
    return hi, lo


def _split3(x):
    hi = x.astype(BF16)
    r = x - hi.astype(F32)
    mid = r.astype(BF16)
    lo = (r - mid.astype(F32)).astype(BF16)
    return hi, mid, lo


def _dot_x3(x, m):
    hi, mid, lo = _split3(x)
    return _dot(hi, m) + _dot(mid, m) + _dot(lo, m)


def _dot_3x(m, x):
    hi, mid, lo = _split3(x)
    return _dot(m, hi) + _dot(m, mid) + _dot(m, lo)


def _dot_hp(x, w):
    xh, xm, xl = _split3(x)
    wh, wl = _split2(w)
    return _dot(xh, wh) + (_dot(xh, wl) + _dot(xm, wh)) + (_dot(xm, wl) + _dot(xl, wh))


def _sigmoid(x):
    return 1.0 / (1.0 + jnp.exp(-x))


def _softplus(x):
    return jnp.maximum(x, 0.0) + jnp.log1p(jnp.exp(-jnp.abs(x)))


def _silu(x):
    return x * _sigmoid(x)


def _rmsnorm(x, g):
    return x * lax.rsqrt(jnp.mean(x * x, axis=-1, keepdims=True) + NORM_EPS) * g


def _dense_kernel(x_ref, w_ref, b_ref, o_ref, *, act):
    x = x_ref[...]
    if act:
        x = _silu(x)
    o_ref[...] = _dot_hp(x, w_ref[...]) + b_ref[...]


def _dense(x, w, b, act, tn=512):
    m, k = x.shape
    n = w.shape[1]
    assert n % tn == 0
    return pl.pallas_call(
        functools.partial(_dense_kernel, act=act),
        grid=(n // tn,),
        in_specs=[pl.BlockSpec((m, k), lambda j: (0, 0)),
                  pl.BlockSpec((k, tn), lambda j: (0, j)),
                  pl.BlockSpec((1, tn), lambda j: (0, j))],
        out_specs=pl.BlockSpec((m, tn), lambda j: (0, j)),
        out_shape=jax.ShapeDtypeStruct((m, n), F32),
        compiler_params=_cparams(("parallel",)),
        name="dense",
    )(x, w, b.reshape(1, n))


def _in_proj_kernel(x_ref, sh_ref, sc_ref, g_ref, w_ref, b_ref, o_ref, h_ref, h_scr, *, epilogue, h_rows):
    @pl.when(pl.program_id(2) == 0)
    def _():
        h = _rmsnorm(x_ref[0], g_ref[...]) * (1.0 + sc_ref[0]) + sh_ref[0]
        h_scr[...] = h.astype(BF16)
        h_ref[0] = h[h.shape[0] - h_rows:, :]

    acc = _dot(h_scr[...], w_ref[...])
    if epilogue == "sigmoid":
        acc = _sigmoid(acc)
    elif epilogue == "log_sigmoid":
        acc = -_softplus(-(acc + b_ref[...]))
    o_ref[...] = acc.astype(o_ref.dtype).reshape(o_ref.shape)


def _in_proj(x, sh, sc, g, w, bias, *, tm, tn, epilogue, out_dtype, h_rows, split=False):
    b, t, d = x.shape
    n = w.shape[1]
    assert t % tm == 0 and n % tn == 0
    if split:
        o_spec = pl.BlockSpec((1, 1, tm, tn), lambda bi, i, j: (j, bi, i, 0))
        o_shape = jax.ShapeDtypeStruct((n // tn, b, t, tn), out_dtype)
    else:
        o_spec = pl.BlockSpec((1, tm, tn), lambda bi, i, j: (bi, i, j))
        o_shape = jax.ShapeDtypeStruct((b, t, n), out_dtype)
    per_tok = sh.shape[1] != 1
    mrows = tm if per_tok else 1
    mod_map = (lambda bi, i, j: (bi, i, 0)) if per_tok else (lambda bi, i, j: (bi, 0, 0))
    if bias is None:
        bias = jnp.zeros((1, n), F32)
    out, h = pl.pallas_call(
        functools.partial(_in_proj_kernel, epilogue=epilogue, h_rows=h_rows),
        grid=(b, t // tm, n // tn),
        in_specs=[pl.BlockSpec((1, tm, d), lambda bi, i, j: (bi, i, 0)),
                  pl.BlockSpec((1, mrows, d), mod_map),
                  pl.BlockSpec((1, mrows, d), mod_map),
                  pl.BlockSpec((1, d), lambda bi, i, j: (0, 0)),
                  pl.BlockSpec((d, tn), lambda bi, i, j: (0, j)),
                  pl.BlockSpec((1, tn), lambda bi, i, j: (0, j))],
        out_specs=[o_spec, pl.BlockSpec((1, h_rows, d), lambda bi, i, j: (bi, 0, 0))],
        out_shape=[o_shape, jax.ShapeDtypeStruct((b, h_rows, d), F32)],
        scratch_shapes=[pltpu.VMEM((tm, d), BF16)],
        compiler_params=_cparams(("parallel", "arbitrary", "arbitrary")),
        name="in_proj_" + epilogue,
    )(x, sh, sc, g.reshape(1, d), w, bias)
    return out, h


def _prep_kernel(pr_ref, prev_ref, mu_ref, w0_ref, ww2_ref, a0_ref, wa2_ref, wg2_ref, kk_ref, ka_ref, rk_ref,
                 seg_ref, r_o, w_o, k_o, v_o, kk_o, kka_o, g_o, bonus_o, carry):
    @pl.when(pl.program_id(1) == 0)
    def _():
        carry[...] = prev_ref[0]

    pr = pr_ref[0]
    tt = pr.shape[0]
    first = lax.broadcasted_iota(jnp.int32, pr.shape, 0) == 0
    pprev = jnp.where(first, carry[...], pltpu.roll(pr, 1, axis=0))
    carry[...] = pr[tt - 1:tt, :]
    pm = pr + (pprev - pr) * mu_ref[...]
    r = pm[:, 0:WIDTH]
    k = pm[:, WIDTH:2 * WIDTH]
    v = pm[:, 2 * WIDTH:3 * WIDTH]
    lo = pm[:, 3 * WIDTH:3 * WIDTH + LANES]
    glo = pm[:, 3 * WIDTH + LANES:3 * WIDTH + 2 * LANES]
    seg = seg_ref[...]
    w_log = -_softplus(-(w0_ref[...] + _dot(jnp.tanh(lo).astype(BF16), ww2_ref[...]))) - 0.5
    decay = jnp.exp(-jnp.exp(w_log))
    a = _sigmoid(a0_ref[...] + _dot(lo.astype(BF16), wa2_ref[...]))
    g = _dot(_sigmoid(glo).astype(BF16), wg2_ref[...])
    kk = k * kk_ref[...]
    kk = kk / jnp.maximum(jnp.sqrt(_dot_x3(kk * kk, seg)), 1e-12)
    k2 = k * (1.0 + (a - 1.0) * ka_ref[...])
    for d in range(r_o.shape[0]):
        r_o[d, 0] = r
        w_o[d, 0] = decay
        k_o[d, 0] = k2
        kk_o[d, 0] = kk
        kka_o[d, 0] = kk * a
    v_o[0] = v
    g_o[0] = g
    bonus_o[0] = _dot_x3(r * k2 * rk_ref[...], seg) * v


def _seg_ones(width, seg):
    i = jnp.arange(width) // seg
    return (i[:, None] == i[None, :]).astype(BF16)


def _lane_dup(b):
    return max(1, LANES // (b * N_HEADS))


def _rwkv_prep(pr, prev, p, tt):
    b, t, _ = pr.shape
    dup = _lane_dup(b)
    zpad = jnp.zeros((LANES - DECAY_LORA, WIDTH), F32)
    ww2 = jnp.concatenate([p["w_w2"], zpad], axis=0).astype(BF16)
    wa2 = jnp.concatenate([zpad, p["w_a2"]], axis=0).astype(BF16)
    row = lambda a: a.reshape(1, -1)
    consts = [row(p["mu_shift"]), row(p["w0"]), ww2, row(p["a0"]), wa2, p["w_g2"].astype(BF16),
              row(p["k_k"]), row(p["k_a"]), row(p["r_k"]), _seg_ones(WIDTH, HEAD_DIM)]
    const_specs = [pl.BlockSpec(c.shape, lambda bi, i: (0, 0)) for c in consts]
    o_spec = pl.BlockSpec((1, tt, WIDTH), lambda bi, i: (bi, i, 0))
    k_spec = pl.BlockSpec((dup, 1, tt, WIDTH), lambda bi, i: (0, bi, i, 0))
    o_shape = jax.ShapeDtypeStruct((b, t, WIDTH), F32)
    k_shape = jax.ShapeDtypeStruct((dup, b, t, WIDTH), F32)
    r, dec, k2, v, kk, kka, g, bonus = pl.pallas_call(
        _prep_kernel,
        grid=(b, t // tt),
        in_specs=[pl.BlockSpec((1, tt, R_COLS), lambda bi, i: (bi, i, 0)),
                  pl.BlockSpec((1, 1, R_COLS), lambda bi, i: (bi, 0, 0))] + const_specs,
        out_specs=[k_spec, k_spec, k_spec, o_spec, k_spec, k_spec, o_spec, o_spec],
        out_shape=[k_shape, k_shape, k_shape, o_shape, k_shape, k_shape, o_shape, o_shape],
        scratch_shapes=[pltpu.VMEM((1, R_COLS), F32)],
        compiler_params=_cparams(("parallel", "arbitrary")),
        name="rwkv_prep",
    )(pr, prev.reshape(b, 1, R_COLS), *consts)
    return r, dec, k2, v, kk, kka, g, bonus


def _scan_kernel(kk_ref, w_ref, k_ref, kka_ref, r_ref, v_ref, s0_ref, y_ref, s_ref, *, steps, n_slabs):
    @pl.when(pl.program_id(0) == 0)
    def _():
        s_ref[...] = s0_ref[...]

    def step(t, carry):
        kk = kk_ref[t]
        w = w_ref[t]
        k = k_ref[t]
        kka = kka_ref[t]
        r = r_ref[t]
        rw = r * w
        c1 = jnp.sum(r * kka, axis=0, keepdims=True)
        c2 = jnp.sum(r * k, axis=0, keepdims=True)
        for n in range(n_slabs):
            s = s_ref[n]
            sa = -jnp.sum(s * kk, axis=0, keepdims=True)
            yp = jnp.sum(s * rw, axis=0, keepdims=True)
            vn = v_ref[t, pl.ds(n, 1), :]
            s_ref[n] = s * w + sa * kka + vn * k
            y_ref[t, pl.ds(n, 1), :] = yp + c1 * sa + c2 * vn
        return carry

    lax.fori_loop(0, steps, step, 0)


def _wkv_scan(r, w, k, v, kk, kka, s0, tc):
    b, t, _ = v.shape
    bh = b * N_HEADS
    dup = _lane_dup(b)
    lanes = dup * bh
    ni = HEAD_DIM // dup

    def key_layout(x):
        return x.reshape(dup, b, t, N_HEADS, HEAD_DIM).transpose(2, 4, 0, 1, 3).reshape(t, HEAD_DIM, lanes)

    v_l = v.reshape(b, t, N_HEADS, ni, dup).transpose(1, 3, 4, 0, 2).reshape(t, ni, lanes)
    s_l = s0.reshape(b, N_HEADS, ni, dup, HEAD_DIM).transpose(2, 4, 3, 0, 1).reshape(ni, HEAD_DIM, lanes)
    key_spec = pl.BlockSpec((tc, HEAD_DIM, lanes), lambda c: (c, 0, 0))
    val_spec = pl.BlockSpec((tc, ni, lanes), lambda c: (c, 0, 0))
    st_spec = pl.BlockSpec((ni, HEAD_DIM, lanes), lambda c: (0, 0, 0))
    y_l, s_out = pl.pallas_call(
        functools.partial(_scan_kernel, steps=tc, n_slabs=ni),
        grid=(t // tc,),
        in_specs=[key_spec] * 5 + [val_spec, st_spec],
        out_specs=[val_spec, st_spec],
        out_shape=[jax.ShapeDtypeStruct((t, ni, lanes), F32),
                   jax.ShapeDtypeStruct((ni, HEAD_DIM, lanes), F32)],
        compiler_params=_cparams(("arbitrary",)),
        name="wkv_scan",
    )(key_layout(kk), key_layout(w), key_layout(k), key_layout(kka), key_layout(r), v_l, s_l)
    y = y_l.reshape(t, ni, dup, b, N_HEADS).transpose(3, 0, 4, 1, 2).reshape(b, t, WIDTH)
    s_new = s_out.reshape(ni, HEAD_DIM, dup, b, N_HEADS).transpose(3, 4, 0, 2, 1).reshape(
        b, N_HEADS, HEAD_DIM, HEAD_DIM)
    return y, s_new


def _cumsum_kernel(x_ref, o_ref, carry):
    @pl.when(pl.program_id(1) == 0)
    def _():
        carry[...] = jnp.zeros_like(carry)

    x = x_ref[0]
    tb = x.shape[0]
    tril = (lax.broadcasted_iota(jnp.int32, (tb, tb), 1) <= lax.broadcasted_iota(jnp.int32, (tb, tb), 0)).astype(BF16)
    c = _dot_3x(tril, x) + carry[...]
    o_ref[0] = c
    carry[...] = c[tb - 1:tb, :]


def _cumsum_time(x, tb):
    b, t, n = x.shape
    spec = pl.BlockSpec((1, tb, n), lambda bi, i: (bi, i, 0))
    return pl.pallas_call(
        _cumsum_kernel, grid=(b, t // tb), in_specs=[spec], out_specs=spec,
        out_shape=jax.ShapeDtypeStruct((b, t, n), F32),
        scratch_shapes=[pltpu.VMEM((1, n), F32)],
        compiler_params=_cparams(("parallel", "arbitrary")),
        name="cumsum_time",
    )(x)


def _bias_lanes(x, col, lane, base, own, key_side):
    hi, mid, lo = _split3(col)
    first, second = (base + 3, base) if key_side else (base, base + 3)
    out = jnp.where(own, x, 0.0)
    out = jnp.where((lane >= second) & (lane < second + 3), 1.0, out)
    out = jnp.where(lane == first, hi.astype(F32), out)
    out = jnp.where(lane == first + 1, mid.astype(F32), out)
    return jnp.where(lane == first + 2, lo.astype(F32), out)


def _fox_prompt_kernel(q_ref, k_ref, v_ref, cq_ref, ck_ref, o_ref, kaug, vb, *, tq):
    i = pl.program_id(2)
    t = k_ref.shape[2]
    pair = pl.program_id(1)

    @pl.when(i == 0)
    def _():
        lane_k = lax.broadcasted_iota(jnp.int32, (t, LANES), 1)
        vb[...] = v_ref[0, 0].astype(BF16)
        k = k_ref[0, 0]
        for hh in range(2):
            own = (lane_k >= hh * HEAD_DIM) & (lane_k < (hh + 1) * HEAD_DIM)
            ck = jnp.sum(jnp.where(lane_k == 2 * pair + hh, ck_ref[0], 0.0), axis=-1, keepdims=True)
            kaug[hh] = _bias_lanes(k, -ck, lane_k, (1 - hh) * HEAD_DIM, own, True).astype(BF16)

    lane = lax.broadcasted_iota(jnp.int32, (tq, LANES), 1)
    q = q_ref[0, 0] * (HEAD_DIM ** -0.5)
    qa = []
    for hh in range(2):
        own = (lane >= hh * HEAD_DIM) & (lane < (hh + 1) * HEAD_DIM)
        cq = jnp.sum(jnp.where(lane == 2 * pair + hh, cq_ref[0], 0.0), axis=-1, keepdims=True)
        qa.append(_bias_lanes(q, cq, lane, (1 - hh) * HEAD_DIM, own, False).astype(BF16))
    causal = lax.broadcasted_iota(jnp.int32, (tq, tq), 1) <= lax.broadcasted_iota(jnp.int32, (tq, tq), 0)

    def block(j, carry, masked):
        start = pl.multiple_of(j * tq, tq)
        v_blk = vb[pl.ds(start, tq), :]
        out = []
        for hh in range(2):
            m, l, acc = carry[hh]
            s = _dot_nt(qa[hh], kaug[hh, pl.ds(start, tq), :])
            if masked:
                s = jnp.where(causal, s, -jnp.inf)
            m_new = jnp.maximum(m, jnp.max(s, axis=-1, keepdims=True))
            p = jnp.exp(s - m_new)
            alpha = jnp.exp(m - m_new)
            out.append((m_new, alpha * l + jnp.sum(p, axis=-1, keepdims=True),
                        alpha * acc + _dot(p.astype(BF16), v_blk)))
        return tuple(out)

    one = (jnp.full((tq, 1), -jnp.inf, F32), jnp.zeros((tq, 1), F32), jnp.zeros((tq, LANES), F32))
    carry = lax.fori_loop(0, i, lambda j, c: block(j, c, False), (one, one))
    (_, l0, a0), (_, l1, a1) = block(i, carry, True)
    o_ref[0] = jnp.where(lane < HEAD_DIM, a0 / l0, a1 / l1).astype(o_ref.dtype)


def _fox_prompt(qkv, cum, tq):
    _, b, t, _ = qkv.shape
    npair = WIDTH // LANES
    return pl.pallas_call(
        functools.partial(_fox_prompt_kernel, tq=tq),
        grid=(b, npair, t // tq),
        in_specs=[pl.BlockSpec((1, 1, tq, LANES), lambda bi, p, i: (0, bi, i, p)),
                  pl.BlockSpec((1, 1, t, LANES), lambda bi, p, i: (1, bi, 0, p)),
                  pl.BlockSpec((1, 1, t, LANES), lambda bi, p, i: (2, bi, 0, p)),
                  pl.BlockSpec((1, tq, LANES), lambda bi, p, i: (bi, i, 0)),
                  pl.BlockSpec((1, t, LANES), lambda bi, p, i: (bi, 0, 0))],
        out_specs=pl.BlockSpec((1, tq, LANES), lambda bi, p, i: (bi, i, p)),
        out_shape=jax.ShapeDtypeStruct((b, t, WIDTH), BF16),
        scratch_shapes=[pltpu.VMEM((2, t, LANES), BF16), pltpu.VMEM((t, LANES), BF16)],
        compiler_params=_cparams(("parallel", "parallel", "arbitrary")),
        name="fox_prompt",
    )(qkv, qkv, qkv, cum, cum)


def _page_sums_kernel(lf_ref, rev_ref, tot_ref):
    n = lf_ref.shape[1]
    a = lax.broadcasted_iota(jnp.int32, (n, n), 0)
    c = lax.broadcasted_iota(jnp.int32, (n, n), 1)
    hi, mid, lo = _split3(lf_ref[...])
    later = (a > c).astype(BF16)
    rev_ref[...] = _dot(hi, later) + _dot(mid, later) + _dot(lo, later)
    every = jnp.ones((n, n), BF16)
    tot_ref[...] = _dot(hi, every) + _dot(mid, every) + _dot(lo, every)


def _page_sums(lf_rows, rows):
    n_rows, n = lf_rows.shape
    spec = pl.BlockSpec((rows, n), lambda i: (i, 0))
    return pl.pallas_call(
        _page_sums_kernel, grid=(n_rows // rows,), in_specs=[spec], out_specs=[spec, spec],
        out_shape=[jax.ShapeDtypeStruct((n_rows, n), F32)] * 2,
        compiler_params=_cparams(("parallel",)),
        name="page_sums",
    )(lf_rows)


def _fox_sample_kernel(pt_ref, q_ref, kn_ref, vn_ref, lfn_ref, *rest, pages_per_step):
    npp = pages_per_step
    k_refs = rest[:npp]
    v_refs = rest[npp:2 * npp]
    rev_refs = rest[2 * npp:3 * npp]
    tot_refs = rest[3 * npp:4 * npp]
    o_ref = rest[4 * npp]
    qrep, m_s, l_s, acc_s, suf_s, cn_s = rest[4 * npp + 1:]
    step = pl.program_id(1)
    nq = q_ref.shape[1]
    rows = N_HEADS * nq
    page = kn_ref.shape[1]
    row_head = lax.broadcasted_iota(jnp.int32, (rows, WIDTH), 0) // nq
    lane_head = lax.broadcasted_iota(jnp.int32, (rows, WIDTH), 1) // HEAD_DIM

    def rep_heads(x):
        return jnp.concatenate([jnp.broadcast_to(x[h:h + 1, :], (nq, x.shape[1])) for h in range(N_HEADS)], axis=0)

    def update(s_list, pv):
        m_old = m_s[...]
        m_new = functools.reduce(jnp.maximum, [jnp.max(s, axis=-1, keepdims=True) for s in s_list] + [m_old])
        p_list = [jnp.exp(s - m_new) for s in s_list]
        alpha = jnp.exp(m_old - m_new)
        l_s[...] = alpha * l_s[...] + functools.reduce(jnp.add, [jnp.sum(p, axis=-1, keepdims=True) for p in p_list])
        acc_s[...] = alpha * acc_s[...] + functools.reduce(jnp.add, [pv(u, p.astype(BF16)) for u, p in enumerate(p_list)])
        m_s[...] = m_new

    @pl.when(step == 0)
    def _():
        q = q_ref[0] * (HEAD_DIM ** -0.5)
        qrep[...] = jnp.where(row_head == lane_head, jnp.concatenate([q] * N_HEADS, axis=0), 0.0).astype(BF16)
        key_i = lax.broadcasted_iota(jnp.int32, (page, page), 0)
        key_j = lax.broadcasted_iota(jnp.int32, (page, page), 1)
        cn_row = _dot_x3(rep_heads(lfn_ref[0]), (key_i <= key_j).astype(BF16))
        rq = lax.broadcasted_iota(jnp.int32, (rows, page), 0) % nq
        kc = lax.broadcasted_iota(jnp.int32, (rows, page), 1)
        cn_col = jnp.sum(jnp.where(kc == rq, cn_row, 0.0), axis=-1, keepdims=True)
        cn_s[...] = cn_col
        suf_s[...] = jnp.zeros_like(suf_s)
        m_s[...] = jnp.full_like(m_s, -jnp.inf)
        l_s[...] = jnp.zeros_like(l_s)
        acc_s[...] = jnp.zeros_like(acc_s)
        s = _dot_nt(qrep[...], kn_ref[0].astype(BF16)) + cn_col - cn_row
        vn = vn_ref[0].astype(BF16)
        update([jnp.where(kc <= rq, s, -jnp.inf)], lambda u, p: _dot(p, vn))

    q_all = qrep[...]
    cn = cn_s[...]
    suf = suf_s[...]
    s_list = []
    for u in range(npp):
        kt = k_refs[u][0].reshape(WIDTH, page).astype(BF16)
        s_list.append(_dot(q_all, kt) + ((cn + suf) + rep_heads(rev_refs[u][0])))
        suf = suf + rep_heads(tot_refs[u][0])
    update(s_list, lambda u, p: _dot_nt(p, v_refs[u][0].reshape(WIDTH, page).astype(BF16)))
    suf_s[...] = suf

    @pl.when(step == pl.num_programs(1) - 1)
    def _():
        o_sel = jnp.where(row_head == lane_head, acc_s[...] / l_s[...], 0.0)
        out = o_sel[0:nq]
        for h in range(1, N_HEADS):
            out = out + o_sel[h * nq:(h + 1) * nq]
        o_ref[0] = out.astype(o_ref.dtype)


def _fox_sample(q, k_new, v_new, logf_new, cache_k, cache_v, cache_logf, page_table, pages_per_step):
    b, tn, _ = q.shape
    n_pool, page = cache_k.shape[:2]
    n_pages = page_table.shape[1]
    npp = pages_per_step
    assert n_pages % npp == 0 and tn <= page
    rows = N_HEADS * tn
    ck = cache_k.transpose(0, 2, 3, 1)
    cv = cache_v.transpose(0, 2, 3, 1)
    clf = cache_logf.transpose(0, 2, 1).reshape(n_pool * N_HEADS, page)
    sum_rows = 2048 if clf.shape[0] % 2048 == 0 else clf.shape[0]
    rev, tot = (a.reshape(n_pool, N_HEADS, page) for a in _page_sums(clf, sum_rows))
    pad_rows = lambda x: jnp.pad(x, ((0, 0), (0, page - tn), (0, 0)))
    lfn = jnp.pad(logf_new.transpose(0, 2, 1), ((0, 0), (0, 0), (0, page - tn)))

    def page_map(u, nd):
        return lambda bi, s, pt: (pt[bi, n_pages - 1 - (s * npp + u)],) + (0,) * nd

    tok_spec = lambda r: pl.BlockSpec((1, r, WIDTH), lambda bi, s, pt: (bi, 0, 0))
    in_specs = ([tok_spec(tn), tok_spec(page), tok_spec(page),
                 pl.BlockSpec((1, N_HEADS, page), lambda bi, s, pt: (bi, 0, 0))]
                + [pl.BlockSpec((1, N_HEADS, HEAD_DIM, page), page_map(u, 3)) for u in range(npp)] * 2
                + [pl.BlockSpec((1, N_HEADS, page), page_map(u, 2)) for u in range(npp)] * 2)
    grid_spec = pltpu.PrefetchScalarGridSpec(
        num_scalar_prefetch=1,
        grid=(b, n_pages // npp),
        in_specs=in_specs,
        out_specs=tok_spec(tn),
        scratch_shapes=[pltpu.VMEM((rows, WIDTH), BF16), pltpu.VMEM((rows, 1), F32), pltpu.VMEM((rows, 1), F32),
                        pltpu.VMEM((rows, WIDTH), F32), pltpu.VMEM((rows, page), F32), pltpu.VMEM((rows, 1), F32)])
    return pl.pallas_call(
        functools.partial(_fox_sample_kernel, pages_per_step=npp),
        grid_spec=grid_spec,
        out_shape=jax.ShapeDtypeStruct((b, tn, WIDTH), BF16),
        compiler_params=_cparams(("parallel", "arbitrary")),
        name="fox_sample",
    )(page_table, q, pad_rows(k_new), pad_rows(v_new), lfn, *([ck] * npp), *([cv] * npp), *([rev] * npp),
      *([tot] * npp))


def _merge_kernel(y_ref, bonus_ref, g_ref, yf_ref, gates_ref, x_ref, g1_ref, sh2_ref, sc2_ref,
                  lnw_ref, lnb_ref, seg_ref, wr_ref, wf_ref, wo_ref, n2_ref, wrt_ref,
                  x1_ref, h2_ref, lg_ref):
    seg = seg_ref[...]
    y = y_ref[0]
    mu = _dot_x3(y, seg) * (1.0 / HEAD_DIM)
    d = y - mu
    var = _dot_x3(d * d, seg) * (1.0 / HEAD_DIM)
    yn = d * lax.rsqrt(var + LNX_EPS) * lnw_ref[...] + lnb_ref[...]
    yr = ((yn + bonus_ref[0]) * g_ref[0]).astype(BF16)
    gates = gates_ref[0].astype(F32)
    d_model = x_ref.shape[2]
    merged = gates[:, :d_model] * _dot(yr, wr_ref[...]) + gates[:, d_model:] * _dot(yf_ref[0], wf_ref[...])
    x1 = x_ref[0] + g1_ref[0] * _dot(merged.astype(BF16), wo_ref[...])
    x1_ref[0] = x1
    h2 = _rmsnorm(x1, n2_ref[...]) * (1.0 + sc2_ref[0]) + sh2_ref[0]
    h2_ref[0] = h2
    hh, hl = _split2(h2)
    wrt = wrt_ref[...]
    lg_ref[0] = _dot(hh, wrt[0]) + (_dot(hh, wrt[1]) + _dot(hl, wrt[0]))


def _merge(y, bonus, g, yf, gates, x, g1, sh2, sc2, p, w_router_t, tm):
    b, t, d = x.shape
    per_tok = g1.shape[1] != 1
    mrows = tm if per_tok else 1
    mod_map = (lambda bi, i: (bi, i, 0)) if per_tok else (lambda bi, i: (bi, 0, 0))
    tok = lambda n: pl.BlockSpec((1, tm, n), lambda bi, i: (bi, i, 0))
    mod = pl.BlockSpec((1, mrows, d), mod_map)
    row = lambda a: a.reshape(1, -1)
    consts = [row(p["lnx_w"]), row(p["lnx_b"]), _seg_ones(WIDTH, HEAD_DIM), p["w_br_r"].astype(BF16),
              p["w_br_f"].astype(BF16), p["w_out"].astype(BF16), row(p["norm2_g"]), w_router_t]
    const_specs = [pl.BlockSpec(c.shape, (lambda bi, i: (0, 0)) if c.ndim == 2 else (lambda bi, i: (0, 0, 0)))
                   for c in consts]
    return pl.pallas_call(
        _merge_kernel,
        grid=(b, t // tm),
        in_specs=[tok(WIDTH), tok(WIDTH), tok(WIDTH), tok(WIDTH), tok(2 * d), tok(d), mod, mod, mod] + const_specs,
        out_specs=[tok(d), tok(d), tok(LANES)],
        out_shape=[jax.ShapeDtypeStruct((b, t, d), F32), jax.ShapeDtypeStruct((b, t, d), F32),
                   jax.ShapeDtypeStruct((b, t, LANES), F32)],
        compiler_params=_cparams(("parallel", "parallel")),
        name="merge",
    )(y, bonus, g, yf, gates, x, g1, sh2, sc2, *consts)


def _route_kernel(lg_ref, bias_ref, row_ref, rowt_ref, wt_ref, cnt_ref):
    lt = lg_ref[...].T
    tm = lt.shape[1]
    score = _sigmoid(lt[:N_EXPERTS])
    biased = score + bias_ref[...][:N_EXPERTS]
    slab = [biased[k * N_GROUPS:(k + 1) * N_GROUPS] for k in range(GROUP_SIZE)]
    neg = jnp.full((N_GROUPS, tm), -jnp.inf, F32)
    m1 = functools.reduce(jnp.maximum, slab)
    taken = jnp.zeros((N_GROUPS, tm), jnp.bool_)
    m2 = neg
    for k in range(GROUP_SIZE):
        is_first = (slab[k] == m1) & jnp.logical_not(taken)
        taken = taken | is_first
        m2 = jnp.maximum(m2, jnp.where(is_first, neg, slab[k]))
    gs = m1 + m2
    g_iota = lax.broadcasted_iota(jnp.int32, (N_GROUPS, tm), 0)
    cnt = jnp.zeros((N_GROUPS, tm), jnp.int32)
    for g2 in range(N_GROUPS):
        other = gs[g2:g2 + 1, :]
        beats = (other > gs) | ((g_iota > g2) & (other == gs))
        cnt = cnt + beats.astype(jnp.int32)
    g_sel = cnt < TOPK_GROUPS
    cand = [jnp.where(g_sel, slab[k], neg) for k in range(GROUP_SIZE)]
    rank = [jnp.zeros((N_GROUPS, tm), jnp.int32) for _ in range(GROUP_SIZE)]
    for k2 in range(GROUP_SIZE):
        for g2 in range(N_GROUPS):
            other = cand[k2][g2:g2 + 1, :]
            for k in range(GROUP_SIZE):
                first = (g_iota >= g2) if k2 < k else (g_iota > g2)
                beats = (other > cand[k]) | (first & (other == cand[k]))
                rank[k] = rank[k] + beats.astype(jnp.int32)
    sel = [rank[k] < TOP_K for k in range(GROUP_SIZE)]
    sc = [score[k * N_GROUPS:(k + 1) * N_GROUPS] for k in range(GROUP_SIZE)]
    picked = [jnp.where(sel[k], sc[k], 0.0) for k in range(GROUP_SIZE)]
    total = jnp.sum(functools.reduce(jnp.add, picked), axis=0, keepdims=True)
    gate = jnp.concatenate([pk / total * ROUTED_SCALE for pk in picked], axis=0)
    chosen = jnp.concatenate([s.astype(F32) for s in sel], axis=0)
    chosen_b = chosen.astype(BF16)
    ri = lax.broadcasted_iota(jnp.int32, (N_EXPERTS, N_EXPERTS), 0)
    ci = lax.broadcasted_iota(jnp.int32, (N_EXPERTS, N_EXPERTS), 1)
    lower = (ci < ri).astype(BF16)
    ordinal = _dot(lower, chosen_b)
    ta = lax.broadcasted_iota(jnp.int32, (tm, tm), 0)
    tc = lax.broadcasted_iota(jnp.int32, (tm, tm), 1)
    rank_tok = _dot(chosen_b, (ta < tc).astype(BF16))
    count = jnp.sum(chosen, axis=1, keepdims=True)
    runs = jnp.broadcast_to(jnp.floor((count + (RUN - 1)) * (1.0 / RUN)), (N_EXPERTS, LANES))
    cnt_ref[0] = runs
    start = _dot(lower, runs.astype(BF16))[:, 0:1] * float(RUN)
    tile_row = start + rank_tok
    r_rows, w_rows = [], []
    for n in range(TOP_K):
        hit = (chosen > 0.0) & (ordinal == float(n))
        r_rows.append(jnp.sum(jnp.where(hit, tile_row, 0.0), axis=0, keepdims=True))
        w_rows.append(jnp.sum(jnp.where(hit, gate, 0.0), axis=0, keepdims=True))
    zrow = jnp.zeros((1, tm), F32)
    pad8 = lambda rows: jnp.concatenate(rows + [zrow] * (8 - TOP_K), axis=0)
    zpad = jnp.zeros((LANES - 8, tm), F32)
    row_ref[...] = pad8(r_rows).astype(jnp.int32)
    rowt_ref[...] = jnp.concatenate([pad8(r_rows), zpad], axis=0).T.astype(jnp.int32)
    wt_ref[...] = jnp.concatenate([pad8(w_rows), zpad], axis=0).T


def _route(logits, e_bias_perm, tm):
    m = logits.shape[0]
    bias = jnp.pad(e_bias_perm, (0, LANES - N_EXPERTS)).reshape(LANES, 1)
    tok_spec = pl.BlockSpec((tm, LANES), lambda i: (i, 0))
    return pl.pallas_call(
        _route_kernel,
        grid=(m // tm,),
        in_specs=[tok_spec, pl.BlockSpec((LANES, 1), lambda i: (0, 0))],
        out_specs=[pl.BlockSpec((8, tm), lambda i: (0, i)), tok_spec, tok_spec,
                   pl.BlockSpec((1, N_EXPERTS, LANES), lambda i: (i, 0, 0))],
        out_shape=[jax.ShapeDtypeStruct((8, m), jnp.int32), jax.ShapeDtypeStruct((m, LANES), jnp.int32),
                   jax.ShapeDtypeStruct((m, LANES), F32), jax.ShapeDtypeStruct((m // tm, N_EXPERTS, LANES), F32)],
        compiler_params=_cparams(("parallel",)),
        name="route",
    )(logits, bias)


def _sorted_capacity(tm):
    rows = TOP_K * tm + N_EXPERTS * (RUN - 1)
    return -(-rows // SORT_BLOCK) * SORT_BLOCK


def _run_pieces(i, nrun_ref, src_ref, dst_ref, piece):
    def per_expert(e, total):
        j = i * N_EXPERTS + e
        n, s0, d0 = nrun_ref[j], src_ref[j], dst_ref[j]

        def one(c, carry):
            piece(s0 + c, d0 + c).start()
            return carry

        lax.fori_loop(0, n, one, 0)
        return total + n

    return lax.fori_loop(0, N_EXPERTS, per_expert, 0)


def _sort_kernel(nrun_ref, src_ref, dst_ref, nblk_ref, row_ref, h_ref, xs_in_ref, xs_ref, buf, sem):
    del xs_in_ref
    i = pl.program_id(0)
    tm = h_ref.shape[0]
    hb = h_ref[...].astype(BF16)
    rows = row_ref[...]
    riota = lax.broadcasted_iota(jnp.int32, (SORT_BLOCK, tm), 0)

    def block(b, carry):
        r0 = pl.multiple_of(b * SORT_BLOCK, SORT_BLOCK)
        hit = riota + r0 == rows[0:1, :]
        for n in range(1, TOP_K):
            hit = hit | (riota + r0 == rows[n:n + 1, :])
        buf[pl.ds(r0, SORT_BLOCK), :] = _dot(hit.astype(BF16), hb).astype(BF16)
        return carry

    lax.fori_loop(0, nblk_ref[i], block, 0)

    def piece(s, d):
        return pltpu.make_async_copy(buf.at[pl.ds(pl.multiple_of(s * RUN, RUN), RUN), :],
                                     xs_ref.at[pl.ds(pl.multiple_of(d * RUN, RUN), RUN), :], sem)

    started = _run_pieces(i, nrun_ref, src_ref, dst_ref, piece)

    def drain(c, carry):
        piece(0, 0).wait()
        return carry

    lax.fori_loop(0, started, drain, 0)


def _sort_rows(h2, row, tables, n_rows, tm):
    m, d = h2.shape
    grid_spec = pltpu.PrefetchScalarGridSpec(
        num_scalar_prefetch=4,
        grid=(m // tm,),
        in_specs=[pl.BlockSpec((8, tm), lambda i, *_: (0, i)), pl.BlockSpec((tm, d), lambda i, *_: (i, 0)),
                  pl.BlockSpec(memory_space=pl.ANY)],
        out_specs=pl.BlockSpec(memory_space=pl.ANY),
        scratch_shapes=[pltpu.VMEM((_sorted_capacity(tm), d), BF16), pltpu.SemaphoreType.DMA(())])
    return pl.pallas_call(
        _sort_kernel,
        grid_spec=grid_spec,
        out_shape=jax.ShapeDtypeStruct((n_rows, d), BF16),
        input_output_aliases={6: 0},
        compiler_params=_cparams(("arbitrary",)),
        name="sort_rows",
    )(*tables, row, h2, jnp.zeros((n_rows, d), BF16))


def _expert_kernel(te_ref, nu_ref, x_ref, wg_ref, wu_ref, wd_ref, o_ref, wgb, wub, wdb):
    i = pl.program_id(0)
    changed = (i == 0) | (te_ref[i] != te_ref[jnp.maximum(i - 1, 0)])

    @pl.when(changed)
    def _():
        wgb[...] = wg_ref[0].astype(BF16)
        wub[...] = wu_ref[0].astype(BF16)
        wdb[...] = wd_ref[0].astype(BF16)

    @pl.when(i < nu_ref[0])
    def _():
        x = x_ref[...]
        act = _silu(_dot(x, wgb[...])) * _dot(x, wub[...])
        o_ref[...] = _dot(act.astype(BF16), wdb[...]).astype(o_ref.dtype)

    @pl.when(i >= nu_ref[0])
    def _():
        o_ref[...] = jnp.zeros_like(o_ref)


def _expert_tiles(xs, tile_expert, n_used, wg, wu, wd, te):
    n_rows, d = xs.shape
    ff = wg.shape[2]
    last = lambda i, nu: jnp.minimum(i, nu[0] - 1)
    grid_spec = pltpu.PrefetchScalarGridSpec(
        num_scalar_prefetch=2,
        grid=(n_rows // te,),
        in_specs=[pl.BlockSpec((te, d), lambda i, tx, nu: (last(i, nu), 0)),
                  pl.BlockSpec((1, d, ff), lambda i, tx, nu: (tx[i], 0, 0)),
                  pl.BlockSpec((1, d, ff), lambda i, tx, nu: (tx[i], 0, 0)),
                  pl.BlockSpec((1, ff, d), lambda i, tx, nu: (tx[i], 0, 0))],
        out_specs=pl.BlockSpec((te, d), lambda i, tx, nu: (i, 0)),
        scratch_shapes=[pltpu.VMEM((d, ff), BF16), pltpu.VMEM((d, ff), BF16), pltpu.VMEM((ff, d), BF16)])
    return pl.pallas_call(
        _expert_kernel,
        grid_spec=grid_spec,
        out_shape=jax.ShapeDtypeStruct((n_rows, d), BF16),
        compiler_params=_cparams(("arbitrary",)),
        name="expert_tiles",
    )(tile_expert, n_used, xs, wg, wu, wd)


def _combine_kernel(nrun_ref, src_ref, dst_ref, nblk_ref, rowt_ref, wt_ref, h_ref, x1_ref, g2_ref, nf_ref,
                    wsg_ref, wsu_ref, wsd_ref, os_ref, y_ref, buf, rowb, wb, sem):
    i = pl.program_id(0) * pl.num_programs(1) + pl.program_id(1)
    tm = h_ref.shape[1]
    nblk = nblk_ref[i]
    tail = pl.multiple_of((nblk - 1) * SORT_BLOCK, SORT_BLOCK)
    buf[pl.ds(tail, SORT_BLOCK), :] = jnp.zeros((SORT_BLOCK, buf.shape[1]), BF16)

    def piece(s, d):
        return pltpu.make_async_copy(os_ref.at[pl.ds(pl.multiple_of(d * RUN, RUN), RUN), :],
                                     buf.at[pl.ds(pl.multiple_of(s * RUN, RUN), RUN), :], sem)

    started = _run_pieces(i, nrun_ref, src_ref, dst_ref, piece)
    h = h_ref[0].astype(BF16)
    shared = _dot((_silu(_dot(h, wsg_ref[...])) * _dot(h, wsu_ref[...])).astype(BF16), wsd_ref[...])

    def drain(c, carry):
        piece(0, 0).wait()
        return carry

    w = wt_ref[0]
    rt = rowt_ref[0]
    for n in range(TOP_K):
        rowb[n] = jnp.broadcast_to(rt[:, n:n + 1], (tm, SORT_BLOCK))
        wb[n] = jnp.broadcast_to(w[:, n:n + 1], (tm, SORT_BLOCK))
    lax.fori_loop(0, started, drain, 0)
    ciota = lax.broadcasted_iota(jnp.int32, (tm, SORT_BLOCK), 1)

    def block(b, acc):
        r0 = pl.multiple_of(b * SORT_BLOCK, SORT_BLOCK)
        c = jnp.where(ciota == rowb[0] - r0, wb[0], 0.0)
        for n in range(1, TOP_K):
            c = c + jnp.where(ciota == rowb[n] - r0, wb[n], 0.0)
        return acc + _dot(c.astype(BF16), buf[pl.ds(r0, SORT_BLOCK), :])

    routed = lax.fori_loop(0, nblk, block, jnp.zeros((tm, buf.shape[1]), F32))
    x2 = x1_ref[0] + g2_ref[0] * (routed + shared)
    y_ref[0] = _rmsnorm(x2, nf_ref[...])


def _combine(os, rowt, wt, tables, h2, x1, g2, normf_g, wsg, wsu, wsd, tm):
    b, t, d = x1.shape
    nt = t // tm
    per_tok = g2.shape[1] != 1
    mrows = tm if per_tok else 1
    mod_map = (lambda bi, i, *_: (bi, i, 0)) if per_tok else (lambda bi, i, *_: (bi, 0, 0))
    tok = lambda n: pl.BlockSpec((1, tm, n), lambda bi, i, *_: (bi, i, 0))
    const = lambda a: pl.BlockSpec(a.shape, lambda bi, i, *_: (0, 0))
    grid_spec = pltpu.PrefetchScalarGridSpec(
        num_scalar_prefetch=4,
        grid=(b, nt),
        in_specs=[tok(LANES), tok(LANES), tok(d), tok(d), pl.BlockSpec((1, mrows, d), mod_map),
                  pl.BlockSpec((1, d), lambda bi, i, *_: (0, 0)), const(wsg), const(wsu), const(wsd),
                  pl.BlockSpec(memory_space=pl.ANY)],
        out_specs=tok(d),
        scratch_shapes=[pltpu.VMEM((_sorted_capacity(tm), d), BF16), pltpu.VMEM((TOP_K, tm, SORT_BLOCK), jnp.int32),
                        pltpu.VMEM((TOP_K, tm, SORT_BLOCK), F32), pltpu.SemaphoreType.DMA(())])
    return pl.pallas_call(
        _combine_kernel,
        grid_spec=grid_spec,
        out_shape=jax.ShapeDtypeStruct((b, t, d), F32),
        compiler_params=_cparams(("arbitrary", "arbitrary")),
        name="combine",
    )(*tables, rowt.reshape(b, t, LANES), wt.reshape(b, t, LANES), h2, x1, g2, normf_g.reshape(1, d), wsg, wsu, wsd, os)


def _moe(h2, logits, x1, g2, p, w, cfg):
    bx, tx, d = x1.shape
    m = bx * tx
    te, tm = cfg["te"], cfg["tm_route"]
    row, rowt, wt, cnt = _route(logits.reshape(m, LANES), w["e_bias_perm"], tm)
    runs = cnt[:, :, 0].astype(jnp.int32)
    src = jnp.cumsum(runs, axis=1) - runs
    nblk = (jnp.sum(runs, axis=1) * RUN + SORT_BLOCK - 1) // SORT_BLOCK
    per_expert = jnp.sum(runs, axis=0)
    tiles_e = (per_expert * RUN + te - 1) // te
    ends = jnp.cumsum(tiles_e)
    dst = ((ends - tiles_e) * (te // RUN))[None, :] + jnp.cumsum(runs, axis=0) - runs
    n_tiles = (m * TOP_K + (m // tm) * N_EXPERTS * (RUN - 1)) // te + N_EXPERTS
    n_used = ends[-1:].astype(jnp.int32)
    tile_ids = jnp.minimum(jnp.arange(n_tiles, dtype=jnp.int32), n_used[0] - 1)
    tile_row = jnp.minimum(jnp.sum(ends[None, :] <= tile_ids[:, None], axis=1), N_EXPERTS - 1)
    tile_expert = ((tile_row % N_GROUPS) * GROUP_SIZE + tile_row // N_GROUPS).astype(jnp.int32)
    flat = lambda a: a.reshape(-1).astype(jnp.int32)
    tables = (flat(runs), flat(src), flat(dst), flat(nblk))
    xs = _sort_rows(h2.reshape(m, d), row, tables, n_tiles * te, tm)
    os = _expert_tiles(xs, tile_expert, n_used, p["w_exp_gate"], p["w_exp_up"], p["w_exp_down"], te)
    return _combine(os, rowt, wt, tables, h2, x1, g2, p["normf_g"], w["sh_gate"], w["sh_up"], w["sh_down"], tm)


def _layer(x, mod, shift_prev, wkv0, attend, p, w, cfg):
    b, t, d = x.shape
    bx, tx = cfg["rows"]
    tm = cfg["tm"]
    xr = x.reshape(bx, tx, d)
    if bx == b:
        part = lambda i: mod[:, i:i + 1, :]
    else:
        part = lambda i: jnp.repeat(mod[:, i, :], t, axis=0).reshape(bx, tx, d)
    sh1, sc1, g1, sh2, sc2, g2 = (part(i) for i in range(6))
    h_rows = 1 if bx == b else tx
    proj = functools.partial(_in_proj, xr, sh1, sc1, p["norm1_g"], tm=cfg["tm_in"], h_rows=h_rows)
    pr, h_keep = proj(w["in_r"], None, tn=cfg["tn_r"], epilogue="none", out_dtype=F32)
    qkv, _ = proj(w["in_qkv"], None, tn=WIDTH, epilogue="none", out_dtype=F32, split=True)
    logf_pad, _ = proj(w["in_f"], w["b_f_pad"], tn=LANES, epilogue="log_sigmoid", out_dtype=F32)
    gates, _ = proj(w["in_g"], None, tn=512, epilogue="sigmoid", out_dtype=BF16)
    shift_new = h_keep[:, 0, :] if bx == b else h_keep.reshape(b, t, d)[:, -1, :]
    logf = logf_pad.reshape(b, t, LANES)[:, :, :N_HEADS]
    qkv = qkv.reshape(3, b, t, WIDTH)

    prev = _dense(shift_prev, w["in_r_f32"], jnp.zeros((R_COLS,), F32), act=False, tn=R_COLS // 2)
    r, dec, k2, v, kk, kka, g, bonus = _rwkv_prep(pr.reshape(b, t, R_COLS), prev, p, cfg["tt"])
    y_scan, wkv_new = _wkv_scan(r, dec, k2, v, kk, kka, wkv0, cfg["tc"])

    y_f = attend(qkv, logf)

    rs = lambda a: a.reshape(bx, tx, a.shape[-1])
    x1, h2, logits = _merge(rs(y_scan), rs(bonus), rs(g), rs(y_f), gates, xr, g1, sh2, sc2, p, w["router_t"], tm)
    y = _moe(h2, logits, x1, g2, p, w, cfg)
    k_out = qkv[1].reshape(b, t, N_HEADS, HEAD_DIM)
    v_out = qkv[2].reshape(b, t, N_HEADS, HEAD_DIM)
    return y.reshape(b, t, d), k_out, v_out, logf, wkv_new, shift_new


def kernel(x_prompt, x_sample, c_prompt, c_sample, cache_k, cache_v, cache_logf, page_table, state_wkv, state_shift, w_ada, b_ada, norm1_g, w_in, mu_shift, w0, w_w2, a0, w_a2, w_g2, k_k, k_a, r_k, lnx_w, lnx_b, b_f, w_br_r, w_br_f, w_out, norm2_g, w_router, e_bias, w_exp_gate, w_exp_up, w_exp_down, w_sh_gate, w_sh_up, w_sh_down, normf_g):
    p = dict(norm1_g=norm1_g, mu_shift=mu_shift, w0=w0, w_w2=w_w2, a0=a0, w_a2=w_a2, w_g2=w_g2, k_k=k_k, k_a=k_a,
             r_k=r_k, lnx_w=lnx_w, lnx_b=lnx_b, w_br_r=w_br_r, w_br_f=w_br_f, w_out=w_out, norm2_g=norm2_g,
             normf_g=normf_g, w_exp_gate=w_exp_gate, w_exp_up=w_exp_up, w_exp_down=w_exp_down)
    bp, tp, d = x_prompt.shape
    bs, ts, _ = x_sample.shape
    off_f = R_COLS + 3 * WIDTH
    off_g = off_f + N_HEADS
    perm = lambda a: a.reshape(a.shape[:-1] + (N_GROUPS, GROUP_SIZE)).swapaxes(-1, -2).reshape(a.shape)
    router = jnp.pad(perm(w_router), ((0, 0), (0, LANES - N_EXPERTS)))
    r_hi = router.astype(BF16)
    w = dict(
        in_r=w_in[:, :R_COLS].astype(BF16), in_r_f32=w_in[:, :R_COLS],
        in_qkv=w_in[:, R_COLS:off_f].astype(BF16),
        in_f=jnp.pad(w_in[:, off_f:off_g], ((0, 0), (0, LANES - N_HEADS))).astype(BF16),
        in_g=w_in[:, off_g:].astype(BF16),
        b_f_pad=jnp.pad(b_f, (0, LANES - N_HEADS)).reshape(1, LANES),
        router_t=jnp.stack([r_hi, (router - r_hi.astype(F32)).astype(BF16)]),
        e_bias_perm=perm(e_bias),
        sh_gate=w_sh_gate.astype(BF16), sh_up=w_sh_up.astype(BF16), sh_down=w_sh_down.astype(BF16),
    )
    mod = _dense(jnp.concatenate([c_prompt, c_sample], axis=0), w_ada, b_ada, act=True).reshape(bp + bs, 6, d)

    def attend_prompt(qkv, logf):
        lf = jnp.pad(logf, ((0, 0), (0, 0), (0, LANES - N_HEADS)))
        return _fox_prompt(qkv, _cumsum_time(lf, 256), min(512, tp))

    def attend_sample(qkv, logf):
        return _fox_sample(qkv[0], qkv[1], qkv[2], logf,
                           cache_k, cache_v, cache_logf, page_table, min(16, page_table.shape[1]))

    cfg_p = dict(rows=(bp, tp), tm=min(512, tp), tm_in=min(1024, tp), tn_r=R_COLS // 2, tt=min(512, tp), tc=16,
                 tm_route=min(512, tp), te=512)
    cfg_s = dict(rows=(1, bs * ts), tm=bs * ts, tm_in=bs * ts, tn_r=R_COLS // 2, tt=ts, tc=ts, tm_route=bs * ts,
                 te=256)
    yp, kp, vp, lfp, wkvp, shp = _layer(x_prompt, mod[:bp], jnp.zeros((bp, d), F32),
                                        jnp.zeros((bp, N_HEADS, HEAD_DIM, HEAD_DIM), F32), attend_prompt, p, w, cfg_p)
    ys, ks, vs, lfs, wkvs, shs = _layer(x_sample, mod[bp:], state_shift, state_wkv, attend_sample, p, w, cfg_s)
    return (yp, ys, kp, vp, lfp, wkvp, shp, ks, vs, lfs, wkvs, shs)
```

```python
import functools

import jax
import jax.numpy as jnp
from jax import lax
from jax.experimental import pallas as pl
from jax.experimental.pallas import tpu as pltpu

F32 = jnp.float32
BF16 = jnp.bfloat16

HEAD_DIM = 64
N_HEADS = 8
WIDTH = N_HEADS * HEAD_DIM
DECAY_LORA = 64
AAA_LORA = 64
GATE_LORA = 128
R_COLS = 3 * WIDTH + DECAY_LORA + AAA_LORA + GATE_LORA
LNX_EPS = 64e-5
NORM_EPS = 1e-6
N_EXPERTS = 64
N_GROUPS = 8
GROUP_SIZE = N_EXPERTS // N_GROUPS
TOPK_GROUPS = 4
TOP_K = 6
ROUTED_SCALE = 2.5
LANES = 128
RUN = 16
SORT_BLOCK = 512
VMEM_LIMIT = 56 * 1024 * 1024


def _cparams(sem):
    return pltpu.CompilerParams(dimension_semantics=sem, vmem_limit_bytes=VMEM_LIMIT)


def _dot(a, b):
    return jnp.dot(a, b, preferred_element_type=F32)


def _dot_nt(a, b):
    return lax.dot_general(a, b, (((1,), (1,)), ((), ())), preferred_element_type=F32)


def _split2(x):
    hi = x.astype(BF16)
    lo = (x - hi.astype(F32)).astype(BF16)
    return hi, lo


def _split3(x):
    hi = x.astype(BF16)
    r = x - hi.astype(F32)
    mid = r.astype(BF16)
    lo = (r - mid.astype(F32)).astype(BF16)
    return hi, mid, lo


def _dot_x3(x, m):
    hi, mid, lo = _split3(x)
    return _dot(hi, m) + _dot(mid, m) + _dot(lo, m)


def _dot_3x(m, x):
    hi, mid, lo = _split3(x)
    return _dot(m, hi) + _dot(m, mid) + _dot(m, lo)


def _dot_hp(x, w):
    xh, xm, xl = _split3(x)
    wh, wl = _split2(w)
    return _dot(xh, wh) + (_dot(xh, wl) + _dot(xm, wh)) + (_dot(xm, wl) + _dot(xl, wh))


def _sigmoid(x):
    return 1.0 / (1.0 + jnp.exp(-x))


def _softplus(x):
    return jnp.maximum(x, 0.0) + jnp.log1p(jnp.exp(-jnp.abs(x)))


def _silu(x):
    return x * _sigmoid(x)


def _rmsnorm(x, g):
    return x * lax.rsqrt(jnp.mean(x * x, axis=-1, keepdims=True) + NORM_EPS) * g


def _dense_kernel(x_ref, w_ref, b_ref, o_ref, *, act):
    x = x_ref[...]
    if act:
        x = _silu(x)
    o_ref[...] = _dot_hp(x, w_ref[...]) + b_ref[...]


def _dense(x, w, b, act, tn=512):
    m, k = x.shape
    n = w.shape[1]
    assert n % tn == 0
    return pl.pallas_call(
        functools.partial(_dense_kernel, act=act),
        grid=(n // tn,),
        in_specs=[pl.BlockSpec((m, k), lambda j: (0, 0)),
                  pl.BlockSpec((k, tn), lambda j: (0, j)),
                  pl.BlockSpec((1, tn), lambda j: (0, j))],
        out_specs=pl.BlockSpec((m, tn), lambda j: (0, j)),
        out_shape=jax.ShapeDtypeStruct((m, n), F32),
        compiler_params=_cparams(("parallel",)),
        name="dense",
    )(x, w, b.reshape(1, n))


def _in_proj_kernel(x_ref, sh_ref, sc_ref, g_ref, w_ref, b_ref, o_ref, h_ref, h_scr, *, epilogue, h_rows):
    @pl.when(pl.program_id(2) == 0)
    def _():
        h = _rmsnorm(x_ref[0], g_ref[...]) * (1.0 + sc_ref[0]) + sh_ref[0]
        h_scr[...] = h.astype(BF16)
        h_ref[0] = h[h.shape[0] - h_rows:, :]

    acc = _dot(h_scr[...], w_ref[...])
    if epilogue == "sigmoid":
        acc = _sigmoid(acc)
    elif epilogue == "log_sigmoid":
        acc = -_softplus(-(acc + b_ref[...]))
    o_ref[...] = acc.astype(o_ref.dtype).reshape(o_ref.shape)


def _in_proj(x, sh, sc, g, w, bias, *, tm, tn, epilogue, out_dtype, h_rows, split=False):
    b, t, d = x.shape
    n = w.shape[1]
    assert t % tm == 0 and n % tn == 0
    if split:
        o_spec = pl.BlockSpec((1, 1, tm, tn), lambda bi, i, j: (j, bi, i, 0))
        o_shape = jax.ShapeDtypeStruct((n // tn, b, t, tn), out_dtype)
    else:
        o_spec = pl.BlockSpec((1, tm, tn), lambda bi, i, j: (bi, i, j))
        o_shape = jax.ShapeDtypeStruct((b, t, n), out_dtype)
    per_tok = sh.shape[1] != 1
    mrows = tm if per_tok else 1
    mod_map = (lambda bi, i, j: (bi, i, 0)) if per_tok else (lambda bi, i, j: (bi, 0, 0))
    if bias is None:
        bias = jnp.zeros((1, n), F32)
    out, h = pl.pallas_call(
        functools.partial(_in_proj_kernel, epilogue=epilogue, h_rows=h_rows),
        grid=(b, t // tm, n // tn),
        in_specs=[pl.BlockSpec((1, tm, d), lambda bi, i, j: (bi, i, 0)),
                  pl.BlockSpec((1, mrows, d), mod_map),
                  pl.BlockSpec((1, mrows, d), mod_map),
                  pl.BlockSpec((1, d), lambda bi, i, j: (0, 0)),
                  pl.BlockSpec((d, tn), lambda bi, i, j: (0, j)),
                  pl.BlockSpec((1, tn), lambda bi, i, j: (0, j))],
        out_specs=[o_spec, pl.BlockSpec((1, h_rows, d), lambda bi, i, j: (bi, 0, 0))],
        out_shape=[o_shape, jax.ShapeDtypeStruct((b, h_rows, d), F32)],
        scratch_shapes=[pltpu.VMEM((tm, d), BF16)],
        compiler_params=_cparams(("parallel", "arbitrary", "arbitrary")),
        name="in_proj_" + epilogue,
    )(x, sh, sc, g.reshape(1, d), w, bias)
    return out, h


def _prep_kernel(pr_ref, prev_ref, mu_ref, w0_ref, ww2_ref, a0_ref, wa2_ref, wg2_ref, kk_ref, ka_ref, rk_ref,
                 seg_ref, r_o, w_o, k_o, v_o, kk_o, kka_o, g_o, bonus_o, carry):
    @pl.when(pl.program_id(1) == 0)
    def _():
        carry[...] = prev_ref[0]

    pr = pr_ref[0]
    tt = pr.shape[0]
    first = lax.broadcasted_iota(jnp.int32, pr.shape, 0) == 0
    pprev = jnp.where(first, carry[...], pltpu.roll(pr, 1, axis=0))
    carry[...] = pr[tt - 1:tt, :]
    pm = pr + (pprev - pr) * mu_ref[...]
    r = pm[:, 0:WIDTH]
    k = pm[:, WIDTH:2 * WIDTH]
    v = pm[:, 2 * WIDTH:3 * WIDTH]
    lo = pm[:, 3 * WIDTH:3 * WIDTH + LANES]
    glo = pm[:, 3 * WIDTH + LANES:3 * WIDTH + 2 * LANES]
    seg = seg_ref[...]
    w_log = -_softplus(-(w0_ref[...] + _dot(jnp.tanh(lo).astype(BF16), ww2_ref[...]))) - 0.5
    decay = jnp.exp(-jnp.exp(w_log))
    a = _sigmoid(a0_ref[...] + _dot(lo.astype(BF16), wa2_ref[...]))
    g = _dot(_sigmoid(glo).astype(BF16), wg2_ref[...])
    kk = k * kk_ref[...]
    kk = kk / jnp.maximum(jnp.sqrt(_dot_x3(kk * kk, seg)), 1e-12)
    k2 = k * (1.0 + (a - 1.0) * ka_ref[...])
    for d in range(r_o.shape[0]):
        r_o[d, 0] = r
        w_o[d, 0] = decay
        k_o[d, 0] = k2
        kk_o[d, 0] = kk
        kka_o[d, 0] = kk * a
    v_o[0] = v
    g_o[0] = g
    bonus_o[0] = _dot_x3(r * k2 * rk_ref[...], seg) * v


def _seg_ones(width, seg):
    i = jnp.arange(width) // seg
    return (i[:, None] == i[None, :]).astype(BF16)


def _lane_dup(b):
    return max(1, LANES // (b * N_HEADS))


def _rwkv_prep(pr, prev, p, tt):
    b, t, _ = pr.shape
    dup = _lane_dup(b)
    zpad = jnp.zeros((LANES - DECAY_LORA, WIDTH), F32)
    ww2 = jnp.concatenate([p["w_w2"], zpad], axis=0).astype(BF16)
    wa2 = jnp.concatenate([zpad, p["w_a2"]], axis=0).astype(BF16)
    row = lambda a: a.reshape(1, -1)
    consts = [row(p["mu_shift"]), row(p["w0"]), ww2, row(p["a0"]), wa2, p["w_g2"].astype(BF16),
              row(p["k_k"]), row(p["k_a"]), row(p["r_k"]), _seg_ones(WIDTH, HEAD_DIM)]
    const_specs = [pl.BlockSpec(c.shape, lambda bi, i: (0, 0)) for c in consts]
    o_spec = pl.BlockSpec((1, tt, WIDTH), lambda bi, i: (bi, i, 0))
    k_spec = pl.BlockSpec((dup, 1, tt, WIDTH), lambda bi, i: (0, bi, i, 0))
    o_shape = jax.ShapeDtypeStruct((b, t, WIDTH), F32)
    k_shape = jax.ShapeDtypeStruct((dup, b, t, WIDTH), F32)
    r, dec, k2, v, kk, kka, g, bonus = pl.pallas_call(
        _prep_kernel,
        grid=(b, t // tt),
        in_specs=[pl.BlockSpec((1, tt, R_COLS), lambda bi, i: (bi, i, 0)),
                  pl.BlockSpec((1, 1, R_COLS), lambda bi, i: (bi, 0, 0))] + const_specs,
        out_specs=[k_spec, k_spec, k_spec, o_spec, k_spec, k_spec, o_spec, o_spec],
        out_shape=[k_shape, k_shape, k_shape, o_shape, k_shape, k_shape, o_shape, o_shape],
        scratch_shapes=[pltpu.VMEM((1, R_COLS), F32)],
        compiler_params=_cparams(("parallel", "arbitrary")),
        name="rwkv_prep",
    )(pr, prev.reshape(b, 1, R_COLS), *consts)
    return r, dec, k2, v, kk, kka, g, bonus


def _scan_kernel(kk_ref, w_ref, k_ref, kka_ref, r_ref, v_ref, s0_ref, y_ref, s_ref, *, steps, n_slabs):
    @pl.when(pl.program_id(0) == 0)
    def _():
        s_ref[...] = s0_ref[...]

    def step(t, carry):
        kk = kk_ref[t]
        w = w_ref[t]
        k = k_ref[t]
        kka = kka_ref[t]
        r = r_ref[t]
        rw = r * w
        c1 = jnp.sum(r * kka, axis=0, keepdims=True)
        c2 = jnp.sum(r * k, axis=0, keepdims=True)
        for n in range(n_slabs):
            s = s_ref[n]
            sa = -jnp.sum(s * kk, axis=0, keepdims=True)
            yp = jnp.sum(s * rw, axis=0, keepdims=True)
            vn = v_ref[t, pl.ds(n, 1), :]
            s_ref[n] = s * w + sa * kka + vn * k
            y_ref[t, pl.ds(n, 1), :] = yp + c1 * sa + c2 * vn
        return carry

    lax.fori_loop(0, steps, step, 0)


def _wkv_scan(r, w, k, v, kk, kka, s0, tc):
    b, t, _ = v.shape
    bh = b * N_HEADS
    dup = _lane_dup(b)
    lanes = dup * bh
    ni = HEAD_DIM // dup

    def key_layout(x):
        return x.reshape(dup, b, t, N_HEADS, HEAD_DIM).transpose(2, 4, 0, 1, 3).reshape(t, HEAD_DIM, lanes)

    v_l = v.reshape(b, t, N_HEADS, ni, dup).transpose(1, 3, 4, 0, 2).reshape(t, ni, lanes)
    s_l = s0.reshape(b, N_HEADS, ni, dup, HEAD_DIM).transpose(2, 4, 3, 0, 1).reshape(ni, HEAD_DIM, lanes)
    key_spec = pl.BlockSpec((tc, HEAD_DIM, lanes), lambda c: (c, 0, 0))
    val_spec = pl.BlockSpec((tc, ni, lanes), lambda c: (c, 0, 0))
    st_spec = pl.BlockSpec((ni, HEAD_DIM, lanes), lambda c: (0, 0, 0))
    y_l, s_out = pl.pallas_call(
        functools.partial(_scan_kernel, steps=tc, n_slabs=ni),
        grid=(t // tc,),
        in_specs=[key_spec] * 5 + [val_spec, st_spec],
        out_specs=[val_spec, st_spec],
        out_shape=[jax.ShapeDtypeStruct((t, ni, lanes), F32),
                   jax.ShapeDtypeStruct((ni, HEAD_DIM, lanes), F32)],
        compiler_params=_cparams(("arbitrary",)),
        name="wkv_scan",
    )(key_layout(kk), key_layout(w), key_layout(k), key_layout(kka), key_layout(r), v_l, s_l)
    y = y_l.reshape(t, ni, dup, b, N_HEADS).transpose(3, 0, 4, 1, 2).reshape(b, t, WIDTH)
    s_new = s_out.reshape(ni, HEAD_DIM, dup, b, N_HEADS).transpose(3, 4, 0, 2, 1).reshape(
        b, N_HEADS, HEAD_DIM, HEAD_DIM)
    return y, s_new


def _cumsum_kernel(x_ref, o_ref, carry):
    @pl.when(pl.program_id(1) == 0)
    def _():
        carry[...] = jnp.zeros_like(carry)

    x = x_ref[0]
    tb = x.shape[0]
    tril = (lax.broadcasted_iota(jnp.int32, (tb, tb), 1) <= lax.broadcasted_iota(jnp.int32, (tb, tb), 0)).astype(BF16)
    c = _dot_3x(tril, x) + carry[...]
    o_ref[0] = c
    carry[...] = c[tb - 1:tb, :]


def _cumsum_time(x, tb):
    b, t, n = x.shape
    spec = pl.BlockSpec((1, tb, n), lambda bi, i: (bi, i, 0))
    return pl.pallas_call(
        _cumsum_kernel, grid=(b, t // tb), in_specs=[spec], out_specs=spec,
        out_shape=jax.ShapeDtypeStruct((b, t, n), F32),
        scratch_shapes=[pltpu.VMEM((1, n), F32)],
        compiler_params=_cparams(("parallel", "arbitrary")),
        name="cumsum_time",
    )(x)


def _bias_lanes(x, col, lane, base, own, key_side):
    hi, mid, lo = _split3(col)
    first, second = (base + 3, base) if key_side else (base, base + 3)
    out = jnp.where(own, x, 0.0)
    out = jnp.where((lane >= second) & (lane < second + 3), 1.0, out)
    out = jnp.where(lane == first, hi.astype(F32), out)
    out = jnp.where(lane == first + 1, mid.astype(F32), out)
    return jnp.where(lane == first + 2, lo.astype(F32), out)


def _fox_prompt_kernel(q_ref, k_ref, v_ref, cq_ref, ck_ref, o_ref, kaug, vb, *, tq):
    i = pl.program_id(2)
    t = k_ref.shape[2]
    pair = pl.program_id(1)

    @pl.when(i == 0)
    def _():
        lane_k = lax.broadcasted_iota(jnp.int32, (t, LANES), 1)
        vb[...] = v_ref[0, 0].astype(BF16)
        k = k_ref[0, 0]
        for hh in range(2):
            own = (lane_k >= hh * HEAD_DIM) & (lane_k < (hh + 1) * HEAD_DIM)
            ck = jnp.sum(jnp.where(lane_k == 2 * pair + hh, ck_ref[0], 0.0), axis=-1, keepdims=True)
            kaug[hh] = _bias_lanes(k, -ck, lane_k, (1 - hh) * HEAD_DIM, own, True).astype(BF16)

    lane = lax.broadcasted_iota(jnp.int32, (tq, LANES), 1)
    q = q_ref[0, 0] * (HEAD_DIM ** -0.5)
    qa = []
    for hh in range(2):
        own = (lane >= hh * HEAD_DIM) & (lane < (hh + 1) * HEAD_DIM)
        cq = jnp.sum(jnp.where(lane == 2 * pair + hh, cq_ref[0], 0.0), axis=-1, keepdims=True)
        qa.append(_bias_lanes(q, cq, lane, (1 - hh) * HEAD_DIM, own, False).astype(BF16))
    causal = lax.broadcasted_iota(jnp.int32, (tq, tq), 1) <= lax.broadcasted_iota(jnp.int32, (tq, tq), 0)

    def block(j, carry, masked):
        start = pl.multiple_of(j * tq, tq)
        v_blk = vb[pl.ds(start, tq), :]
        out = []
        for hh in range(2):
            m, l, acc = carry[hh]
            s = _dot_nt(qa[hh], kaug[hh, pl.ds(start, tq), :])
            if masked:
                s = jnp.where(causal, s, -jnp.inf)
            m_new = jnp.maximum(m, jnp.max(s, axis=-1, keepdims=True))
            p = jnp.exp(s - m_new)
            alpha = jnp.exp(m - m_new)
            out.append((m_new, alpha * l + jnp.sum(p, axis=-1, keepdims=True),
                        alpha * acc + _dot(p.astype(BF16), v_blk)))
        return tuple(out)

    one = (jnp.full((tq, 1), -jnp.inf, F32), jnp.zeros((tq, 1), F32), jnp.zeros((tq, LANES), F32))
    carry = lax.fori_loop(0, i, lambda j, c: block(j, c, False), (one, one))
    (_, l0, a0), (_, l1, a1) = block(i, carry, True)
    o_ref[0] = jnp.where(lane < HEAD_DIM, a0 / l0, a1 / l1).astype(o_ref.dtype)


def _fox_prompt(qkv, cum, tq):
    _, b, t, _ = qkv.shape
    npair = WIDTH // LANES
    return pl.pallas_call(
        functools.partial(_fox_prompt_kernel, tq=tq),
        grid=(b, npair, t // tq),
        in_specs=[pl.BlockSpec((1, 1, tq, LANES), lambda bi, p, i: (0, bi, i, p)),
                  pl.BlockSpec((1, 1, t, LANES), lambda bi, p, i: (1, bi, 0, p)),
                  pl.BlockSpec((1, 1, t, LANES), lambda bi, p, i: (2, bi, 0, p)),
                  pl.BlockSpec((1, tq, LANES), lambda bi, p, i: (bi, i, 0)),
                  pl.BlockSpec((1, t, LANES), lambda bi, p, i: (bi, 0, 0))],
        out_specs=pl.BlockSpec((1, tq, LANES), lambda bi, p, i: (bi, i, p)),
        out_shape=jax.ShapeDtypeStruct((b, t, WIDTH), BF16),
        scratch_shapes=[pltpu.VMEM((2, t, LANES), BF16), pltpu.VMEM((t, LANES), BF16)],
        compiler_params=_cparams(("parallel", "parallel", "arbitrary")),
        name="fox_prompt",
    )(qkv, qkv, qkv, cum, cum)


def _page_sums_kernel(lf_ref, rev_ref, tot_ref):
    n = lf_ref.shape[1]
    a = lax.broadcasted_iota(jnp.int32, (n, n), 0)
    c = lax.broadcasted_iota(jnp.int32, (n, n), 1)
    hi, mid, lo = _split3(lf_ref[...])
    later = (a > c).astype(BF16)
    rev_ref[...] = _dot(hi, later) + _dot(mid, later) + _dot(lo, later)
    every = jnp.ones((n, n), BF16)
    tot_ref[...] = _dot(hi, every) + _dot(mid, every) + _dot(lo, every)


def _page_sums(lf_rows, rows):
    n_rows, n = lf_rows.shape
    spec = pl.BlockSpec((rows, n), lambda i: (i, 0))
    return pl.pallas_call(
        _page_sums_kernel, grid=(n_rows // rows,), in_specs=[spec], out_specs=[spec, spec],
        out_shape=[jax.ShapeDtypeStruct((n_rows, n), F32)] * 2,
        compiler_params=_cparams(("parallel",)),
        name="page_sums",
    )(lf_rows)


def _fox_sample_kernel(pt_ref, q_ref, kn_ref, vn_ref, lfn_ref, *rest, pages_per_step):
    npp = pages_per_step
    k_refs = rest[:npp]
    v_refs = rest[npp:2 * npp]
    rev_refs = rest[2 * npp:3 * npp]
    tot_refs = rest[3 * npp:4 * npp]
    o_ref = rest[4 * npp]
    qrep, m_s, l_s, acc_s, suf_s, cn_s = rest[4 * npp + 1:]
    step = pl.program_id(1)
    nq = q_ref.shape[1]
    rows = N_HEADS * nq
    page = kn_ref.shape[1]
    row_head = lax.broadcasted_iota(jnp.int32, (rows, WIDTH), 0) // nq
    lane_head = lax.broadcasted_iota(jnp.int32, (rows, WIDTH), 1) // HEAD_DIM

    def rep_heads(x):
        return jnp.concatenate([jnp.broadcast_to(x[h:h + 1, :], (nq, x.shape[1])) for h in range(N_HEADS)], axis=0)

    def update(s_list, pv):
        m_old = m_s[...]
        m_new = functools.reduce(jnp.maximum, [jnp.max(s, axis=-1, keepdims=True) for s in s_list] + [m_old])
        p_list = [jnp.exp(s - m_new) for s in s_list]
        alpha = jnp.exp(m_old - m_new)
        l_s[...] = alpha * l_s[...] + functools.reduce(jnp.add, [jnp.sum(p, axis=-1, keepdims=True) for p in p_list])
        acc_s[...] = alpha * acc_s[...] + functools.reduce(jnp.add, [pv(u, p.astype(BF16)) for u, p in enumerate(p_list)])
        m_s[...] = m_new

    @pl.when(step == 0)
    def _():
        q = q_ref[0] * (HEAD_DIM ** -0.5)
        qrep[...] = jnp.where(row_head == lane_head, jnp.concatenate([q] * N_HEADS, axis=0), 0.0).astype(BF16)
        key_i = lax.broadcasted_iota(jnp.int32, (page, page), 0)
        key_j = lax.broadcasted_iota(jnp.int32, (page, page), 1)
        cn_row = _dot_x3(rep_heads(lfn_ref[0]), (key_i <= key_j).astype(BF16))
        rq = lax.broadcasted_iota(jnp.int32, (rows, page), 0) % nq
        kc = lax.broadcasted_iota(jnp.int32, (rows, page), 1)
        cn_col = jnp.sum(jnp.where(kc == rq, cn_row, 0.0), axis=-1, keepdims=True)
        cn_s[...] = cn_col
        suf_s[...] = jnp.zeros_like(suf_s)
        m_s[...] = jnp.full_like(m_s, -jnp.inf)
        l_s[...] = jnp.zeros_like(l_s)
        acc_s[...] = jnp.zeros_like(acc_s)
        s = _dot_nt(qrep[...], kn_ref[0].astype(BF16)) + cn_col - cn_row
        vn = vn_ref[0].astype(BF16)
        update([jnp.where(kc <= rq, s, -jnp.inf)], lambda u, p: _dot(p, vn))

    q_all = qrep[...]
    cn = cn_s[...]
    suf = suf_s[...]
    s_list = []
    for u in range(npp):
        kt = k_refs[u][0].reshape(WIDTH, page).astype(BF16)
        s_list.append(_dot(q_all, kt) + ((cn + suf) + rep_heads(rev_refs[u][0])))
        suf = suf + rep_heads(tot_refs[u][0])
    update(s_list, lambda u, p: _dot_nt(p, v_refs[u][0].reshape(WIDTH, page).astype(BF16)))
    suf_s[...] = suf

    @pl.when(step == pl.num_programs(1) - 1)
    def _():
        o_sel = jnp.where(row_head == lane_head, acc_s[...] / l_s[...], 0.0)
        out = o_sel[0:nq]
        for h in range(1, N_HEADS):
            out = out + o_sel[h * nq:(h + 1) * nq]
        o_ref[0] = out.astype(o_ref.dtype)


def _fox_sample(q, k_new, v_new, logf_new, cache_k, cache_v, cache_logf, page_table, pages_per_step):
    b, tn, _ = q.shape
    n_pool, page = cache_k.shape[:2]
    n_pages = page_table.shape[1]
    npp = pages_per_step
    assert n_pages % npp == 0 and tn <= page
    rows = N_HEADS * tn
    ck = cache_k.transpose(0, 2, 3, 1)
    cv = cache_v.transpose(0, 2, 3, 1)
    clf = cache_logf.transpose(0, 2, 1).reshape(n_pool * N_HEADS, page)
    sum_rows = 2048 if clf.shape[0] % 2048 == 0 else clf.shape[0]
    rev, tot = (a.reshape(n_pool, N_HEADS, page) for a in _page_sums(clf, sum_rows))
    pad_rows = lambda x: jnp.pad(x, ((0, 0), (0, page - tn), (0, 0)))
    lfn = jnp.pad(logf_new.transpose(0, 2, 1), ((0, 0), (0, 0), (0, page - tn)))

    def page_map(u, nd):
        return lambda bi, s, pt: (pt[bi, n_pages - 1 - (s * npp + u)],) + (0,) * nd

    tok_spec = lambda r: pl.BlockSpec((1, r, WIDTH), lambda bi, s, pt: (bi, 0, 0))
    in_specs = ([tok_spec(tn), tok_spec(page), tok_spec(page),
                 pl.BlockSpec((1, N_HEADS, page), lambda bi, s, pt: (bi, 0, 0))]
                + [pl.BlockSpec((1, N_HEADS, HEAD_DIM, page), page_map(u, 3)) for u in range(npp)] * 2
                + [pl.BlockSpec((1, N_HEADS, page), page_map(u, 2)) for u in range(npp)] * 2)
    grid_spec = pltpu.PrefetchScalarGridSpec(
        num_scalar_prefetch=1,
        grid=(b, n_pages // npp),
        in_specs=in_specs,
        out_specs=tok_spec(tn),
        scratch_shapes=[pltpu.VMEM((rows, WIDTH), BF16), pltpu.VMEM((rows, 1), F32), pltpu.VMEM((rows, 1), F32),
                        pltpu.VMEM((rows, WIDTH), F32), pltpu.VMEM((rows, page), F32), pltpu.VMEM((rows, 1), F32)])
    return pl.pallas_call(
        functools.partial(_fox_sample_kernel, pages_per_step=npp),
        grid_spec=grid_spec,
        out_shape=jax.ShapeDtypeStruct((b, tn, WIDTH), BF16),
        compiler_params=_cparams(("parallel", "arbitrary")),
        name="fox_sample",
    )(page_table, q, pad_rows(k_new), pad_rows(v_new), lfn, *([ck] * npp), *([cv] * npp), *([rev] * npp),
      *([tot] * npp))


def _merge_kernel(y_ref, bonus_ref, g_ref, yf_ref, gates_ref, x_ref, g1_ref, sh2_ref, sc2_ref,
                  lnw_ref, lnb_ref, seg_ref, wr_ref, wf_ref, wo_ref, n2_ref, wrt_ref,
                  x1_ref, h2_ref, lg_ref):
    seg = seg_ref[...]
    y = y_ref[0]
    mu = _dot_x3(y, seg) * (1.0 / HEAD_DIM)
    d = y - mu
    var = _dot_x3(d * d, seg) * (1.0 / HEAD_DIM)
    yn = d * lax.rsqrt(var + LNX_EPS) * lnw_ref[...] + lnb_ref[...]
    yr = ((yn + bonus_ref[0]) * g_ref[0]).astype(BF16)
    gates = gates_ref[0].astype(F32)
    d_model = x_ref.shape[2]
    merged = gates[:, :d_model] * _dot(yr, wr_ref[...]) + gates[:, d_model:] * _dot(yf_ref[0], wf_ref[...])
    x1 = x_ref[0] + g1_ref[0] * _dot(merged.astype(BF16), wo_ref[...])
    x1_ref[0] = x1
    h2 = _rmsnorm(x1, n2_ref[...]) * (1.0 + sc2_ref[0]) + sh2_ref[0]
    h2_ref[0] = h2
    hh, hl = _split2(h2)
    wrt = wrt_ref[...]
    lg_ref[0] = _dot(hh, wrt[0]) + (_dot(hh, wrt[1]) + _dot(hl, wrt[0]))


def _merge(y, bonus, g, yf, gates, x, g1, sh2, sc2, p, w_router_t, tm):
    b, t, d = x.shape
    per_tok = g1.shape[1] != 1
    mrows = tm if per_tok else 1
    mod_map = (lambda bi, i: (bi, i, 0)) if per_tok else (lambda bi, i: (bi, 0, 0))
    tok = lambda n: pl.BlockSpec((1, tm, n), lambda bi, i: (bi, i, 0))
    mod = pl.BlockSpec((1, mrows, d), mod_map)
    row = lambda a: a.reshape(1, -1)
    consts = [row(p["lnx_w"]), row(p["lnx_b"]), _seg_ones(WIDTH, HEAD_DIM), p["w_br_r"].astype(BF16),
              p["w_br_f"].astype(BF16), p["w_out"].astype(BF16), row(p["norm2_g"]), w_router_t]
    const_specs = [pl.BlockSpec(c.shape, (lambda bi, i: (0, 0)) if c.ndim == 2 else (lambda bi, i: (0, 0, 0)))
                   for c in consts]
    return pl.pallas_call(
        _merge_kernel,
        grid=(b, t // tm),
        in_specs=[tok(WIDTH), tok(WIDTH), tok(WIDTH), tok(WIDTH), tok(2 * d), tok(d), mod, mod, mod] + const_specs,
        out_specs=[tok(d), tok(d), tok(LANES)],
        out_shape=[jax.ShapeDtypeStruct((b, t, d), F32), jax.ShapeDtypeStruct((b, t, d), F32),
                   jax.ShapeDtypeStruct((b, t, LANES), F32)],
        compiler_params=_cparams(("parallel", "parallel")),
        name="merge",
    )(y, bonus, g, yf, gates, x, g1, sh2, sc2, *consts)


def _route_kernel(lg_ref, bias_ref, row_ref, rowt_ref, wt_ref, cnt_ref):
    lt = lg_ref[...].T
    tm = lt.shape[1]
    score = _sigmoid(lt[:N_EXPERTS])
    biased = score + bias_ref[...][:N_EXPERTS]
    slab = [biased[k * N_GROUPS:(k + 1) * N_GROUPS] for k in range(GROUP_SIZE)]
    neg = jnp.full((N_GROUPS, tm), -jnp.inf, F32)
    m1 = functools.reduce(jnp.maximum, slab)
    taken = jnp.zeros((N_GROUPS, tm), jnp.bool_)
    m2 = neg
    for k in range(GROUP_SIZE):
        is_first = (slab[k] == m1) & jnp.logical_not(taken)
        taken = taken | is_first
        m2 = jnp.maximum(m2, jnp.where(is_first, neg, slab[k]))
    gs = m1 + m2
    g_iota = lax.broadcasted_iota(jnp.int32, (N_GROUPS, tm), 0)
    cnt = jnp.zeros((N_GROUPS, tm), jnp.int32)
    for g2 in range(N_GROUPS):
        other = gs[g2:g2 + 1, :]
        beats = (other > gs) | ((g_iota > g2) & (other == gs))
        cnt = cnt + beats.astype(jnp.int32)
    g_sel = cnt < TOPK_GROUPS
    cand = [jnp.where(g_sel, slab[k], neg) for k in range(GROUP_SIZE)]
    rank = [jnp.zeros((N_GROUPS, tm), jnp.int32) for _ in range(GROUP_SIZE)]
    for k2 in range(GROUP_SIZE):
        for g2 in range(N_GROUPS):
            other = cand[k2][g2:g2 + 1, :]
            for k in range(GROUP_SIZE):
                first = (g_iota >= g2) if k2 < k else (g_iota > g2)
                beats = (other > cand[k]) | (first & (other == cand[k]))
                rank[k] = rank[k] + beats.astype(jnp.int32)
    sel = [rank[k] < TOP_K for k in range(GROUP_SIZE)]
    sc = [score[k * N_GROUPS:(k + 1) * N_GROUPS] for k in range(GROUP_SIZE)]
    picked = [jnp.where(sel[k], sc[k], 0.0) for k in range(GROUP_SIZE)]
    total = jnp.sum(functools.reduce(jnp.add, picked), axis=0, keepdims=True)
    gate = jnp.concatenate([pk / total * ROUTED_SCALE for pk in picked], axis=0)
    chosen = jnp.concatenate([s.astype(F32) for s in sel], axis=0)
    chosen_b = chosen.astype(BF16)
    ri = lax.broadcasted_iota(jnp.int32, (N_EXPERTS, N_EXPERTS), 0)
    ci = lax.broadcasted_iota(jnp.int32, (N_EXPERTS, N_EXPERTS), 1)
    lower = (ci < ri).astype(BF16)
    ordinal = _dot(lower, chosen_b)
    ta = lax.broadcasted_iota(jnp.int32, (tm, tm), 0)
    tc = lax.broadcasted_iota(jnp.int32, (tm, tm), 1)
    rank_tok = _dot(chosen_b, (ta < tc).astype(BF16))
    count = jnp.sum(chosen, axis=1, keepdims=True)
    runs = jnp.broadcast_to(jnp.floor((count + (RUN - 1)) * (1.0 / RUN)), (N_EXPERTS, LANES))
    cnt_ref[0] = runs
    start = _dot(lower, runs.astype(BF16))[:, 0:1] * float(RUN)
    tile_row = start + rank_tok
    r_rows, w_rows = [], []
    for n in range(TOP_K):
        hit = (chosen > 0.0) & (ordinal == float(n))
        r_rows.append(jnp.sum(jnp.where(hit, tile_row, 0.0), axis=0, keepdims=True))
        w_rows.append(jnp.sum(jnp.where(hit, gate, 0.0), axis=0, keepdims=True))
    zrow = jnp.zeros((1, tm), F32)
    pad8 = lambda rows: jnp.concatenate(rows + [zrow] * (8 - TOP_K), axis=0)
    zpad = jnp.zeros((LANES - 8, tm), F32)
    row_ref[...] = pad8(r_rows).astype(jnp.int32)
    rowt_ref[...] = jnp.concatenate([pad8(r_rows), zpad], axis=0).T.astype(jnp.int32)
    wt_ref[...] = jnp.concatenate([pad8(w_rows), zpad], axis=0).T


def _route(logits, e_bias_perm, tm):
    m = logits.shape[0]
    bias = jnp.pad(e_bias_perm, (0, LANES - N_EXPERTS)).reshape(LANES, 1)
    tok_spec = pl.BlockSpec((tm, LANES), lambda i: (i, 0))
    return pl.pallas_call(
        _route_kernel,
        grid=(m // tm,),
        in_specs=[tok_spec, pl.BlockSpec((LANES, 1), lambda i: (0, 0))],
        out_specs=[pl.BlockSpec((8, tm), lambda i: (0, i)), tok_spec, tok_spec,
                   pl.BlockSpec((1, N_EXPERTS, LANES), lambda i: (i, 0, 0))],
        out_shape=[jax.ShapeDtypeStruct((8, m), jnp.int32), jax.ShapeDtypeStruct((m, LANES), jnp.int32),
                   jax.ShapeDtypeStruct((m, LANES), F32), jax.ShapeDtypeStruct((m // tm, N_EXPERTS, LANES), F32)],
        compiler_params=_cparams(("parallel",)),
        name="route",
    )(logits, bias)


def _sorted_capacity(tm):
    rows = TOP_K * tm + N_EXPERTS * (RUN - 1)
    return -(-rows // SORT_BLOCK) * SORT_BLOCK


def _run_pieces(i, nrun_ref, src_ref, dst_ref, piece):
    def per_expert(e, total):
        j = i * N_EXPERTS + e
        n, s0, d0 = nrun_ref[j], src_ref[j], dst_ref[j]

        def one(c, carry):
            piece(s0 + c, d0 + c).start()
            return carry

        lax.fori_loop(0, n, one, 0)
        return total + n

    return lax.fori_loop(0, N_EXPERTS, per_expert, 0)


def _sort_kernel(nrun_ref, src_ref, dst_ref, nblk_ref, row_ref, h_ref, xs_in_ref, xs_ref, buf, sem):
    del xs_in_ref
    i = pl.program_id(0)
    tm = h_ref.shape[0]
    hb = h_ref[...].astype(BF16)
    rows = row_ref[...]
    riota = lax.broadcasted_iota(jnp.int32, (SORT_BLOCK, tm), 0)

    def block(b, carry):
        r0 = pl.multiple_of(b * SORT_BLOCK, SORT_BLOCK)
        hit = riota + r0 == rows[0:1, :]
        for n in range(1, TOP_K):
            hit = hit | (riota + r0 == rows[n:n + 1, :])
        buf[pl.ds(r0, SORT_BLOCK), :] = _dot(hit.astype(BF16), hb).astype(BF16)
        return carry

    lax.fori_loop(0, nblk_ref[i], block, 0)

    def piece(s, d):
        return pltpu.make_async_copy(buf.at[pl.ds(pl.multiple_of(s * RUN, RUN), RUN), :],
                                     xs_ref.at[pl.ds(pl.multiple_of(d * RUN, RUN), RUN), :], sem)

    started = _run_pieces(i, nrun_ref, src_ref, dst_ref, piece)

    def drain(c, carry):
        piece(0, 0).wait()
        return carry

    lax.fori_loop(0, started, drain, 0)


def _sort_rows(h2, row, tables, n_rows, tm):
    m, d = h2.shape
    grid_spec = pltpu.PrefetchScalarGridSpec(
        num_scalar_prefetch=4,
        grid=(m // tm,),
        in_specs=[pl.BlockSpec((8, tm), lambda i, *_: (0, i)), pl.BlockSpec((tm, d), lambda i, *_: (i, 0)),
                  pl.BlockSpec(memory_space=pl.ANY)],
        out_specs=pl.BlockSpec(memory_space=pl.ANY),
        scratch_shapes=[pltpu.VMEM((_sorted_capacity(tm), d), BF16), pltpu.SemaphoreType.DMA(())])
    return pl.pallas_call(
        _sort_kernel,
        grid_spec=grid_spec,
        out_shape=jax.ShapeDtypeStruct((n_rows, d), BF16),
        input_output_aliases={6: 0},
        compiler_params=_cparams(("arbitrary",)),
        name="sort_rows",
    )(*tables, row, h2, jnp.zeros((n_rows, d), BF16))


def _expert_kernel(te_ref, nu_ref, x_ref, wg_ref, wu_ref, wd_ref, o_ref, wgb, wub, wdb):
    i = pl.program_id(0)
    changed = (i == 0) | (te_ref[i] != te_ref[jnp.maximum(i - 1, 0)])

    @pl.when(changed)
    def _():
        wgb[...] = wg_ref[0].astype(BF16)
        wub[...] = wu_ref[0].astype(BF16)
        wdb[...] = wd_ref[0].astype(BF16)

    @pl.when(i < nu_ref[0])
    def _():
        x = x_ref[...]
        act = _silu(_dot(x, wgb[...])) * _dot(x, wub[...])
        o_ref[...] = _dot(act.astype(BF16), wdb[...]).astype(o_ref.dtype)

    @pl.when(i >= nu_ref[0])
    def _():
        o_ref[...] = jnp.zeros_like(o_ref)


def _expert_tiles(xs, tile_expert, n_used, wg, wu, wd, te):
    n_rows, d = xs.shape
    ff = wg.shape[2]
    last = lambda i, nu: jnp.minimum(i, nu[0] - 1)
    grid_spec = pltpu.PrefetchScalarGridSpec(
        num_scalar_prefetch=2,
        grid=(n_rows // te,),
        in_specs=[pl.BlockSpec((te, d), lambda i, tx, nu: (last(i, nu), 0)),
                  pl.BlockSpec((1, d, ff), lambda i, tx, nu: (tx[i], 0, 0)),
                  pl.BlockSpec((1, d, ff), lambda i, tx, nu: (tx[i], 0, 0)),
                  pl.BlockSpec((1, ff, d), lambda i, tx, nu: (tx[i], 0, 0))],
        out_specs=pl.BlockSpec((te, d), lambda i, tx, nu: (i, 0)),
        scratch_shapes=[pltpu.VMEM((d, ff), BF16), pltpu.VMEM((d, ff), BF16), pltpu.VMEM((ff, d), BF16)])
    return pl.pallas_call(
        _expert_kernel,
        grid_spec=grid_spec,
        out_shape=jax.ShapeDtypeStruct((n_rows, d), BF16),
        compiler_params=_cparams(("arbitrary",)),
        name="expert_tiles",
    )(tile_expert, n_used, xs, wg, wu, wd)


def _combine_kernel(nrun_ref, src_ref, dst_ref, nblk_ref, rowt_ref, wt_ref, h_ref, x1_ref, g2_ref, nf_ref,
                    wsg_ref, wsu_ref, wsd_ref, os_ref, y_ref, buf, rowb, wb, sem):
    i = pl.program_id(0) * pl.num_programs(1) + pl.program_id(1)
    tm = h_ref.shape[1]
    nblk = nblk_ref[i]
    tail = pl.multiple_of((nblk - 1) * SORT_BLOCK, SORT_BLOCK)
    buf[pl.ds(tail, SORT_BLOCK), :] = jnp.zeros((SORT_BLOCK, buf.shape[1]), BF16)

    def piece(s, d):
        return pltpu.make_async_copy(os_ref.at[pl.ds(pl.multiple_of(d * RUN, RUN), RUN), :],
                                     buf.at[pl.ds(pl.multiple_of(s * RUN, RUN), RUN), :], sem)

    started = _run_pieces(i, nrun_ref, src_ref, dst_ref, piece)
    h = h_ref[0].astype(BF16)
    shared = _dot((_silu(_dot(h, wsg_ref[...])) * _dot(h, wsu_ref[...])).astype(BF16), wsd_ref[...])

    def drain(c, carry):
        piece(0, 0).wait()
        return carry

    w = wt_ref[0]
    rt = rowt_ref[0]
    for n in range(TOP_K):
        rowb[n] = jnp.broadcast_to(rt[:, n:n + 1], (tm, SORT_BLOCK))
        wb[n] = jnp.broadcast_to(w[:, n:n + 1], (tm, SORT_BLOCK))
    lax.fori_loop(0, started, drain, 0)
    ciota = lax.broadcasted_iota(jnp.int32, (tm, SORT_BLOCK), 1)

    def block(b, acc):
        r0 = pl.multiple_of(b * SORT_BLOCK, SORT_BLOCK)
        c = jnp.where(ciota == rowb[0] - r0, wb[0], 0.0)
        for n in range(1, TOP_K):
            c = c + jnp.where(ciota == rowb[n] - r0, wb[n], 0.0)
        return acc + _dot(c.astype(BF16), buf[pl.ds(r0, SORT_BLOCK), :])

    routed = lax.fori_loop(0, nblk, block, jnp.zeros((tm, buf.shape[1]), F32))
    x2 = x1_ref[0] + g2_ref[0] * (routed + shared)
    y_ref[0] = _rmsnorm(x2, nf_ref[...])


def _combine(os, rowt, wt, tables, h2, x1, g2, normf_g, wsg, wsu, wsd, tm):
    b, t, d = x1.shape
    nt = t // tm
    per_tok = g2.shape[1] != 1
    mrows = tm if per_tok else 1
    mod_map = (lambda bi, i, *_: (bi, i, 0)) if per_tok else (lambda bi, i, *_: (bi, 0, 0))
    tok = lambda n: pl.BlockSpec((1, tm, n), lambda bi, i, *_: (bi, i, 0))
    const = lambda a: pl.BlockSpec(a.shape, lambda bi, i, *_: (0, 0))
    grid_spec = pltpu.PrefetchScalarGridSpec(
        num_scalar_prefetch=4,
        grid=(b, nt),
        in_specs=[tok(LANES), tok(LANES), tok(d), tok(d), pl.BlockSpec((1, mrows, d), mod_map),
                  pl.BlockSpec((1, d), lambda bi, i, *_: (0, 0)), const(wsg), const(wsu), const(wsd),
                  pl.BlockSpec(memory_space=pl.ANY)],
        out_specs=tok(d),
        scratch_shapes=[pltpu.VMEM((_sorted_capacity(tm), d), BF16), pltpu.VMEM((TOP_K, tm, SORT_BLOCK), jnp.int32),
                        pltpu.VMEM((TOP_K, tm, SORT_BLOCK), F32), pltpu.SemaphoreType.DMA(())])
    return pl.pallas_call(
        _combine_kernel,
        grid_spec=grid_spec,
        out_shape=jax.ShapeDtypeStruct((b, t, d), F32),
        compiler_params=_cparams(("arbitrary", "arbitrary")),
        name="combine",
    )(*tables, rowt.reshape(b, t, LANES), wt.reshape(b, t, LANES), h2, x1, g2, normf_g.reshape(1, d), wsg, wsu, wsd, os)


def _moe(h2, logits, x1, g2, p, w, cfg):
    bx, tx, d = x1.shape
    m = bx * tx
    te, tm = cfg["te"], cfg["tm_route"]
    row, rowt, wt, cnt = _route(logits.reshape(m, LANES), w["e_bias_perm"], tm)
    runs = cnt[:, :, 0].astype(jnp.int32)
    src = jnp.cumsum(runs, axis=1) - runs
    nblk = (jnp.sum(runs, axis=1) * RUN + SORT_BLOCK - 1) // SORT_BLOCK
    per_expert = jnp.sum(runs, axis=0)
    tiles_e = (per_expert * RUN + te - 1) // te
    ends = jnp.cumsum(tiles_e)
    dst = ((ends - tiles_e) * (te // RUN))[None, :] + jnp.cumsum(runs, axis=0) - runs
    n_tiles = (m * TOP_K + (m // tm) * N_EXPERTS * (RUN - 1)) // te + N_EXPERTS
    n_used = ends[-1:].astype(jnp.int32)
    tile_ids = jnp.minimum(jnp.arange(n_tiles, dtype=jnp.int32), n_used[0] - 1)
    tile_row = jnp.minimum(jnp.sum(ends[None, :] <= tile_ids[:, None], axis=1), N_EXPERTS - 1)
    tile_expert = ((tile_row % N_GROUPS) * GROUP_SIZE + tile_row // N_GROUPS).astype(jnp.int32)
    flat = lambda a: a.reshape(-1).astype(jnp.int32)
    tables = (flat(runs), flat(src), flat(dst), flat(nblk))
    xs = _sort_rows(h2.reshape(m, d), row, tables, n_tiles * te, tm)
    os = _expert_tiles(xs, tile_expert, n_used, p["w_exp_gate"], p["w_exp_up"], p["w_exp_down"], te)
    return _combine(os, rowt, wt, tables, h2, x1, g2, p["normf_g"], w["sh_gate"], w["sh_up"], w["sh_down"], tm)


def _layer(x, mod, shift_prev, wkv0, attend, p, w, cfg):
    b, t, d = x.shape
    bx, tx = cfg["rows"]
    tm = cfg["tm"]
    xr = x.reshape(bx, tx, d)
    if bx == b:
        part = lambda i: mod[:, i:i + 1, :]
    else:
        part = lambda i: jnp.repeat(mod[:, i, :], t, axis=0).reshape(bx, tx, d)
    sh1, sc1, g1, sh2, sc2, g2 = (part(i) for i in range(6))
    h_rows = 1 if bx == b else tx
    proj = functools.partial(_in_proj, xr, sh1, sc1, p["norm1_g"], tm=cfg["tm_in"], h_rows=h_rows)
    pr, h_keep = proj(w["in_r"], None, tn=cfg["tn_r"], epilogue="none", out_dtype=F32)
    qkv, _ = proj(w["in_qkv"], None, tn=WIDTH, epilogue="none", out_dtype=F32, split=True)
    logf_pad, _ = proj(w["in_f"], w["b_f_pad"], tn=LANES, epilogue="log_sigmoid", out_dtype=F32)
    gates, _ = proj(w["in_g"], None, tn=1024, epilogue="sigmoid", out_dtype=BF16)
    shift_new = h_keep[:, 0, :] if bx == b else h_keep.reshape(b, t, d)[:, -1, :]
    logf = logf_pad.reshape(b, t, LANES)[:, :, :N_HEADS]
    qkv = qkv.reshape(3, b, t, WIDTH)

    prev = _dense(shift_prev, w["in_r_f32"], jnp.zeros((R_COLS,), F32), act=False, tn=R_COLS // 2)
    r, dec, k2, v, kk, kka, g, bonus = _rwkv_prep(pr.reshape(b, t, R_COLS), prev, p, cfg["tt"])
    y_scan, wkv_new = _wkv_scan(r, dec, k2, v, kk, kka, wkv0, cfg["tc"])

    y_f = attend(qkv, logf)

    rs = lambda a: a.reshape(bx, tx, a.shape[-1])
    x1, h2, logits = _merge(rs(y_scan), rs(bonus), rs(g), rs(y_f), gates, xr, g1, sh2, sc2, p, w["router_t"], tm)
    y = _moe(h2, logits, x1, g2, p, w, cfg)
    k_out = qkv[1].reshape(b, t, N_HEADS, HEAD_DIM)
    v_out = qkv[2].reshape(b, t, N_HEADS, HEAD_DIM)
    return y.reshape(b, t, d), k_out, v_out, logf, wkv_new, shift_new


def kernel(x_prompt, x_sample, c_prompt, c_sample, cache_k, cache_v, cache_logf, page_table, state_wkv, state_shift, w_ada, b_ada, norm1_g, w_in, mu_shift, w0, w_w2, a0, w_a2, w_g2, k_k, k_a, r_k, lnx_w, lnx_b, b_f, w_br_r, w_br_f, w_out, norm2_g, w_router, e_bias, w_exp_gate, w_exp_up, w_exp_down, w_sh_gate, w_sh_up, w_sh_down, normf_g):
    p = dict(norm1_g=norm1_g, mu_shift=mu_shift, w0=w0, w_w2=w_w2, a0=a0, w_a2=w_a2, w_g2=w_g2, k_k=k_k, k_a=k_a,
             r_k=r_k, lnx_w=lnx_w, lnx_b=lnx_b, w_br_r=w_br_r, w_br_f=w_br_f, w_out=w_out, norm2_g=norm2_g,
             normf_g=normf_g, w_exp_gate=w_exp_gate, w_exp_up=w_exp_up, w_exp_down=w_exp_down)
    bp, tp, d = x_prompt.shape
    bs, ts, _ = x_sample.shape
    off_f = R_COLS + 3 * WIDTH
    off_g = off_f + N_HEADS
    perm = lambda a: a.reshape(a.shape[:-1] + (N_GROUPS, GROUP_SIZE)).swapaxes(-1, -2).reshape(a.shape)
    router = jnp.pad(perm(w_router), ((0, 0), (0, LANES - N_EXPERTS)))
    r_hi = router.astype(BF16)
    w = dict(
        in_r=w_in[:, :R_COLS].astype(BF16), in_r_f32=w_in[:, :R_COLS],
        in_qkv=w_in[:, R_COLS:off_f].astype(BF16),
        in_f=jnp.pad(w_in[:, off_f:off_g], ((0, 0), (0, LANES - N_HEADS))).astype(BF16),
        in_g=w_in[:, off_g:].astype(BF16),
        b_f_pad=jnp.pad(b_f, (0, LANES - N_HEADS)).reshape(1, LANES),
        router_t=jnp.stack([r_hi, (router - r_hi.astype(F32)).astype(BF16)]),
        e_bias_perm=perm(e_bias),
        sh_gate=w_sh_gate.astype(BF16), sh_up=w_sh_up.astype(BF16), sh_down=w_sh_down.astype(BF16),
    )
    mod = _dense(jnp.concatenate([c_prompt, c_sample], axis=0), w_ada, b_ada, act=True).reshape(bp + bs, 6, d)

    def attend_prompt(qkv, logf):
        lf = jnp.pad(logf, ((0, 0), (0, 0), (0, LANES - N_HEADS)))
        return _fox_prompt(qkv, _cumsum_time(lf, 256), min(512, tp))

    def attend_sample(qkv, logf):
        return _fox_sample(qkv[0], qkv[1], qkv[2], logf,
                           cache_k, cache_v, cache_logf, page_table, min(16, page_table.shape[1]))

    cfg_p = dict(rows=(bp, tp), tm=min(512, tp), tm_in=min(1024, tp), tn_r=R_COLS // 2, tt=min(512, tp), tc=32,
                 tm_route=min(512, tp), te=512)
    cfg_s = dict(rows=(1, bs * ts), tm=bs * ts, tm_in=bs * ts, tn_r=R_COLS // 2, tt=ts, tc=ts, tm_route=bs * ts,
                 te=256)
    yp, kp, vp, lfp, wkvp, shp = _layer(x_prompt, mod[:bp], jnp.zeros((bp, d), F32),
                                        jnp.zeros((bp, N_HEADS, HEAD_DIM, HEAD_DIM), F32), attend_prompt, p, w, cfg_p)
    ys, ks, vs, lfs, wkvs, shs = _layer(x_sample, mod[bp:], state_shift, state_wkv, attend_sample, p, w, cfg_s)
    return (yp, ys, kp, vp, lfp, wkvp, shp, ks, vs, lfs, wkvs, shs)
```

```python
import functools

import jax
import jax.numpy as jnp
from jax import lax
from jax.experimental import pallas as pl
from jax.experimental.pallas import tpu as pltpu

F32 = jnp.float32
BF16 = jnp.bfloat16

HEAD_DIM = 64
N_HEADS = 8
WIDTH = N_HEADS * HEAD_DIM
DECAY_LORA = 64
AAA_LORA = 64
GATE_LORA = 128
R_COLS = 3 * WIDTH + DECAY_LORA + AAA_LORA + GATE_LORA
LNX_EPS = 64e-5
NORM_EPS = 1e-6
N_EXPERTS = 64
N_GROUPS = 8
GROUP_SIZE = N_EXPERTS // N_GROUPS
TOPK_GROUPS = 4
TOP_K = 6
ROUTED_SCALE = 2.5
LANES = 128
RUN = 16
SORT_BLOCK = 512
VMEM_LIMIT = 56 * 1024 * 1024


def _cparams(sem):
    return pltpu.CompilerParams(dimension_semantics=sem, vmem_limit_bytes=VMEM_LIMIT)


def _dot(a, b):
    return jnp.dot(a, b, preferred_element_type=F32)


def _dot_nt(a, b):
    return lax.dot_general(a, b, (((1,), (1,)), ((), ())), preferred_element_type=F32)


def _split2(x):
    hi = x.astype(BF16)
    lo = (x - hi.astype(F32)).astype(BF16)
    return hi, lo


def _split3(x):
    hi = x.astype(BF16)
    r = x - hi.astype(F32)
    mid = r.astype(BF16)
    lo = (r - mid.astype(F32)).astype(BF16)
    return hi, mid, lo


def _dot_x3(x, m):
    hi, mid, lo = _split3(x)
    return _dot(hi, m) + _dot(mid, m) + _dot(lo, m)


def _dot_3x(m, x):
    hi, mid, lo = _split3(x)
    return _dot(m, hi) + _dot(m, mid) + _dot(m, lo)


def _dot_hp(x, w):
    xh, xm, xl = _split3(x)
    wh, wl = _split2(w)
    return _dot(xh, wh) + (_dot(xh, wl) + _dot(xm, wh)) + (_dot(xm, wl) + _dot(xl, wh))


def _sigmoid(x):
    return 1.0 / (1.0 + jnp.exp(-x))


def _softplus(x):
    return jnp.maximum(x, 0.0) + jnp.log1p(jnp.exp(-jnp.abs(x)))


def _silu(x):
    return x * _sigmoid(x)


def _rmsnorm(x, g):
    return x * lax.rsqrt(jnp.mean(x * x, axis=-1, keepdims=True) + NORM_EPS) * g


def _dense_kernel(x_ref, w_ref, b_ref, o_ref, *, act):
    x = x_ref[...]
    if act:
        x = _silu(x)
    o_ref[...] = _dot_hp(x, w_ref[...]) + b_ref[...]


def _dense(x, w, b, act, tn=512):
    m, k = x.shape
    n = w.shape[1]
    assert n % tn == 0
    return pl.pallas_call(
        functools.partial(_dense_kernel, act=act),
        grid=(n // tn,),
        in_specs=[pl.BlockSpec((m, k), lambda j: (0, 0)),
                  pl.BlockSpec((k, tn), lambda j: (0, j)),
                  pl.BlockSpec((1, tn), lambda j: (0, j))],
        out_specs=pl.BlockSpec((m, tn), lambda j: (0, j)),
        out_shape=jax.ShapeDtypeStruct((m, n), F32),
        compiler_params=_cparams(("parallel",)),
        name="dense",
    )(x, w, b.reshape(1, n))


def _in_proj_kernel(x_ref, sh_ref, sc_ref, g_ref, w_ref, b_ref, o_ref, h_ref, h_scr, *, epilogue, h_rows):
    @pl.when(pl.program_id(2) == 0)
    def _():
        h = _rmsnorm(x_ref[0], g_ref[...]) * (1.0 + sc_ref[0]) + sh_ref[0]
        h_scr[...] = h.astype(BF16)
        h_ref[0] = h[h.shape[0] - h_rows:, :]

    acc = _dot(h_scr[...], w_ref[...])
    if epilogue == "sigmoid":
        acc = _sigmoid(acc)
    elif epilogue == "log_sigmoid":
        acc = -_softplus(-(acc + b_ref[...]))
    o_ref[...] = acc.astype(o_ref.dtype).reshape(o_ref.shape)


def _in_proj(x, sh, sc, g, w, bias, *, tm, tn, epilogue, out_dtype, h_rows, split=False):
    b, t, d = x.shape
    n = w.shape[1]
    assert t % tm == 0 and n % tn == 0
    if split:
        o_spec = pl.BlockSpec((1, 1, tm, tn), lambda bi, i, j: (j, bi, i, 0))
        o_shape = jax.ShapeDtypeStruct((n // tn, b, t, tn), out_dtype)
    else:
        o_spec = pl.BlockSpec((1, tm, tn), lambda bi, i, j: (bi, i, j))
        o_shape = jax.ShapeDtypeStruct((b, t, n), out_dtype)
    per_tok = sh.shape[1] != 1
    mrows = tm if per_tok else 1
    mod_map = (lambda bi, i, j: (bi, i, 0)) if per_tok else (lambda bi, i, j: (bi, 0, 0))
    if bias is None:
        bias = jnp.zeros((1, n), F32)
    out, h = pl.pallas_call(
        functools.partial(_in_proj_kernel, epilogue=epilogue, h_rows=h_rows),
        grid=(b, t // tm, n // tn),
        in_specs=[pl.BlockSpec((1, tm, d), lambda bi, i, j: (bi, i, 0)),
                  pl.BlockSpec((1, mrows, d), mod_map),
                  pl.BlockSpec((1, mrows, d), mod_map),
                  pl.BlockSpec((1, d), lambda bi, i, j: (0, 0)),
                  pl.BlockSpec((d, tn), lambda bi, i, j: (0, j)),
                  pl.BlockSpec((1, tn), lambda bi, i, j: (0, j))],
        out_specs=[o_spec, pl.BlockSpec((1, h_rows, d), lambda bi, i, j: (bi, 0, 0))],
        out_shape=[o_shape, jax.ShapeDtypeStruct((b, h_rows, d), F32)],
        scratch_shapes=[pltpu.VMEM((tm, d), BF16)],
        compiler_params=_cparams(("parallel", "arbitrary", "arbitrary")),
        name="in_proj_" + epilogue,
    )(x, sh, sc, g.reshape(1, d), w, bias)
    return out, h


def _prep_kernel(pr_ref, prev_ref, mu_ref, w0_ref, ww2_ref, a0_ref, wa2_ref, wg2_ref, kk_ref, ka_ref, rk_ref,
                 seg_ref, r_o, w_o, k_o, v_o, kk_o, kka_o, g_o, bonus_o, carry):
    @pl.when(pl.program_id(1) == 0)
    def _():
        carry[...] = prev_ref[0]

    pr = pr_ref[0]
    tt = pr.shape[0]
    first = lax.broadcasted_iota(jnp.int32, pr.shape, 0) == 0
    pprev = jnp.where(first, carry[...], pltpu.roll(pr, 1, axis=0))
    carry[...] = pr[tt - 1:tt, :]
    pm = pr + (pprev - pr) * mu_ref[...]
    r = pm[:, 0:WIDTH]
    k = pm[:, WIDTH:2 * WIDTH]
    v = pm[:, 2 * WIDTH:3 * WIDTH]
    lo = pm[:, 3 * WIDTH:3 * WIDTH + LANES]
    glo = pm[:, 3 * WIDTH + LANES:3 * WIDTH + 2 * LANES]
    seg = seg_ref[...]
    w_log = -_softplus(-(w0_ref[...] + _dot(jnp.tanh(lo).astype(BF16), ww2_ref[...]))) - 0.5
    decay = jnp.exp(-jnp.exp(w_log))
    a = _sigmoid(a0_ref[...] + _dot(lo.astype(BF16), wa2_ref[...]))
    g = _dot(_sigmoid(glo).astype(BF16), wg2_ref[...])
    kk = k * kk_ref[...]
    kk = kk / jnp.maximum(jnp.sqrt(_dot_x3(kk * kk, seg)), 1e-12)
    k2 = k * (1.0 + (a - 1.0) * ka_ref[...])
    for d in range(r_o.shape[0]):
        r_o[d, 0] = r
        w_o[d, 0] = decay
        k_o[d, 0] = k2
        kk_o[d, 0] = kk
        kka_o[d, 0] = kk * a
    v_o[0] = v
    g_o[0] = g
    bonus_o[0] = _dot_x3(r * k2 * rk_ref[...], seg) * v


def _seg_ones(width, seg):
    i = jnp.arange(width) // seg
    return (i[:, None] == i[None, :]).astype(BF16)


def _lane_dup(b):
    return max(1, LANES // (b * N_HEADS))


def _rwkv_prep(pr, prev, p, tt):
    b, t, _ = pr.shape
    dup = _lane_dup(b)
    zpad = jnp.zeros((LANES - DECAY_LORA, WIDTH), F32)
    ww2 = jnp.concatenate([p["w_w2"], zpad], axis=0).astype(BF16)
    wa2 = jnp.concatenate([zpad, p["w_a2"]], axis=0).astype(BF16)
    row = lambda a: a.reshape(1, -1)
    consts = [row(p["mu_shift"]), row(p["w0"]), ww2, row(p["a0"]), wa2, p["w_g2"].astype(BF16),
              row(p["k_k"]), row(p["k_a"]), row(p["r_k"]), _seg_ones(WIDTH, HEAD_DIM)]
    const_specs = [pl.BlockSpec(c.shape, lambda bi, i: (0, 0)) for c in consts]
    o_spec = pl.BlockSpec((1, tt, WIDTH), lambda bi, i: (bi, i, 0))
    k_spec = pl.BlockSpec((dup, 1, tt, WIDTH), lambda bi, i: (0, bi, i, 0))
    o_shape = jax.ShapeDtypeStruct((b, t, WIDTH), F32)
    k_shape = jax.ShapeDtypeStruct((dup, b, t, WIDTH), F32)
    r, dec, k2, v, kk, kka, g, bonus = pl.pallas_call(
        _prep_kernel,
        grid=(b, t // tt),
        in_specs=[pl.BlockSpec((1, tt, R_COLS), lambda bi, i: (bi, i, 0)),
                  pl.BlockSpec((1, 1, R_COLS), lambda bi, i: (bi, 0, 0))] + const_specs,
        out_specs=[k_spec, k_spec, k_spec, o_spec, k_spec, k_spec, o_spec, o_spec],
        out_shape=[k_shape, k_shape, k_shape, o_shape, k_shape, k_shape, o_shape, o_shape],
        scratch_shapes=[pltpu.VMEM((1, R_COLS), F32)],
        compiler_params=_cparams(("parallel", "arbitrary")),
        name="rwkv_prep",
    )(pr, prev.reshape(b, 1, R_COLS), *consts)
    return r, dec, k2, v, kk, kka, g, bonus


def _scan_kernel(kk_ref, w_ref, k_ref, kka_ref, r_ref, v_ref, s0_ref, y_ref, s_ref, *, steps, n_slabs):
    @pl.when(pl.program_id(0) == 0)
    def _():
        s_ref[...] = s0_ref[...]

    def step(t, carry):
        kk = kk_ref[t]
        w = w_ref[t]
        k = k_ref[t]
        kka = kka_ref[t]
        r = r_ref[t]
        rw = r * w
        c1 = jnp.sum(r * kka, axis=0, keepdims=True)
        c2 = jnp.sum(r * k, axis=0, keepdims=True)
        for n in range(n_slabs):
            s = s_ref[n]
            sa = -jnp.sum(s * kk, axis=0, keepdims=True)
            yp = jnp.sum(s * rw, axis=0, keepdims=True)
            vn = v_ref[t, pl.ds(n, 1), :]
            s_ref[n] = s * w + sa * kka + vn * k
            y_ref[t, pl.ds(n, 1), :] = yp + c1 * sa + c2 * vn
        return carry

    lax.fori_loop(0, steps, step, 0)


def _wkv_scan(r, w, k, v, kk, kka, s0, tc):
    b, t, _ = v.shape
    bh = b * N_HEADS
    dup = _lane_dup(b)
    lanes = dup * bh
    ni = HEAD_DIM // dup

    def key_layout(x):
        return x.reshape(dup, b, t, N_HEADS, HEAD_DIM).transpose(2, 4, 0, 1, 3).reshape(t, HEAD_DIM, lanes)

    v_l = v.reshape(b, t, N_HEADS, ni, dup).transpose(1, 3, 4, 0, 2).reshape(t, ni, lanes)
    s_l = s0.reshape(b, N_HEADS, ni, dup, HEAD_DIM).transpose(2, 4, 3, 0, 1).reshape(ni, HEAD_DIM, lanes)
    key_spec = pl.BlockSpec((tc, HEAD_DIM, lanes), lambda c: (c, 0, 0))
    val_spec = pl.BlockSpec((tc, ni, lanes), lambda c: (c, 0, 0))
    st_spec = pl.BlockSpec((ni, HEAD_DIM, lanes), lambda c: (0, 0, 0))
    y_l, s_out = pl.pallas_call(
        functools.partial(_scan_kernel, steps=tc, n_slabs=ni),
        grid=(t // tc,),
        in_specs=[key_spec] * 5 + [val_spec, st_spec],
        out_specs=[val_spec, st_spec],
        out_shape=[jax.ShapeDtypeStruct((t, ni, lanes), F32),
                   jax.ShapeDtypeStruct((ni, HEAD_DIM, lanes), F32)],
        compiler_params=_cparams(("arbitrary",)),
        name="wkv_scan",
    )(key_layout(kk), key_layout(w), key_layout(k), key_layout(kka), key_layout(r), v_l, s_l)
    y = y_l.reshape(t, ni, dup, b, N_HEADS).transpose(3, 0, 4, 1, 2).reshape(b, t, WIDTH)
    s_new = s_out.reshape(ni, HEAD_DIM, dup, b, N_HEADS).transpose(3, 4, 0, 2, 1).reshape(
        b, N_HEADS, HEAD_DIM, HEAD_DIM)
    return y, s_new


def _cumsum_kernel(x_ref, o_ref, carry):
    @pl.when(pl.program_id(1) == 0)
    def _():
        carry[...] = jnp.zeros_like(carry)

    x = x_ref[0]
    tb = x.shape[0]
    tril = (lax.broadcasted_iota(jnp.int32, (tb, tb), 1) <= lax.broadcasted_iota(jnp.int32, (tb, tb), 0)).astype(BF16)
    c = _dot_3x(tril, x) + carry[...]
    o_ref[0] = c
    carry[...] = c[tb - 1:tb, :]


def _cumsum_time(x, tb):
    b, t, n = x.shape
    spec = pl.BlockSpec((1, tb, n), lambda bi, i: (bi, i, 0))
    return pl.pallas_call(
        _cumsum_kernel, grid=(b, t // tb), in_specs=[spec], out_specs=spec,
        out_shape=jax.ShapeDtypeStruct((b, t, n), F32),
        scratch_shapes=[pltpu.VMEM((1, n), F32)],
        compiler_params=_cparams(("parallel", "arbitrary")),
        name="cumsum_time",
    )(x)


def _bias_lanes(x, col, lane, base, own, key_side):
    hi, mid, lo = _split3(col)
    first, second = (base + 3, base) if key_side else (base, base + 3)
    out = jnp.where(own, x, 0.0)
    out = jnp.where((lane >= second) & (lane < second + 3), 1.0, out)
    out = jnp.where(lane == first, hi.astype(F32), out)
    out = jnp.where(lane == first + 1, mid.astype(F32), out)
    return jnp.where(lane == first + 2, lo.astype(F32), out)


def _fox_prompt_kernel(q_ref, k_ref, v_ref, cq_ref, ck_ref, o_ref, kaug, vb, *, tq):
    i = pl.program_id(2)
    t = k_ref.shape[2]
    pair = pl.program_id(1)

    @pl.when(i == 0)
    def _():
        lane_k = lax.broadcasted_iota(jnp.int32, (t, LANES), 1)
        vb[...] = v_ref[0, 0].astype(BF16)
        k = k_ref[0, 0]
        for hh in range(2):
            own = (lane_k >= hh * HEAD_DIM) & (lane_k < (hh + 1) * HEAD_DIM)
            ck = jnp.sum(jnp.where(lane_k == 2 * pair + hh, ck_ref[0], 0.0), axis=-1, keepdims=True)
            kaug[hh] = _bias_lanes(k, -ck, lane_k, (1 - hh) * HEAD_DIM, own, True).astype(BF16)

    lane = lax.broadcasted_iota(jnp.int32, (tq, LANES), 1)
    q = q_ref[0, 0] * (HEAD_DIM ** -0.5)
    qa = []
    for hh in range(2):
        own = (lane >= hh * HEAD_DIM) & (lane < (hh + 1) * HEAD_DIM)
        cq = jnp.sum(jnp.where(lane == 2 * pair + hh, cq_ref[0], 0.0), axis=-1, keepdims=True)
        qa.append(_bias_lanes(q, cq, lane, (1 - hh) * HEAD_DIM, own, False).astype(BF16))
    causal = lax.broadcasted_iota(jnp.int32, (tq, tq), 1) <= lax.broadcasted_iota(jnp.int32, (tq, tq), 0)

    def block(j, carry, masked):
        start = pl.multiple_of(j * tq, tq)
        v_blk = vb[pl.ds(start, tq), :]
        out = []
        for hh in range(2):
            m, l, acc = carry[hh]
            s = _dot_nt(qa[hh], kaug[hh, pl.ds(start, tq), :])
            if masked:
                s = jnp.where(causal, s, -jnp.inf)
            m_new = jnp.maximum(m, jnp.max(s, axis=-1, keepdims=True))
            p = jnp.exp(s - m_new)
            alpha = jnp.exp(m - m_new)
            out.append((m_new, alpha * l + jnp.sum(p, axis=-1, keepdims=True),
                        alpha * acc + _dot(p.astype(BF16), v_blk)))
        return tuple(out)

    one = (jnp.full((tq, 1), -jnp.inf, F32), jnp.zeros((tq, 1), F32), jnp.zeros((tq, LANES), F32))
    carry = lax.fori_loop(0, i, lambda j, c: block(j, c, False), (one, one))
    (_, l0, a0), (_, l1, a1) = block(i, carry, True)
    o_ref[0] = jnp.where(lane < HEAD_DIM, a0 / l0, a1 / l1).astype(o_ref.dtype)


def _fox_prompt(qkv, cum, tq):
    _, b, t, _ = qkv.shape
    npair = WIDTH // LANES
    return pl.pallas_call(
        functools.partial(_fox_prompt_kernel, tq=tq),
        grid=(b, npair, t // tq),
        in_specs=[pl.BlockSpec((1, 1, tq, LANES), lambda bi, p, i: (0, bi, i, p)),
                  pl.BlockSpec((1, 1, t, LANES), lambda bi, p, i: (1, bi, 0, p)),
                  pl.BlockSpec((1, 1, t, LANES), lambda bi, p, i: (2, bi, 0, p)),
                  pl.BlockSpec((1, tq, LANES), lambda bi, p, i: (bi, i, 0)),
                  pl.BlockSpec((1, t, LANES), lambda bi, p, i: (bi, 0, 0))],
        out_specs=pl.BlockSpec((1, tq, LANES), lambda bi, p, i: (bi, i, p)),
        out_shape=jax.ShapeDtypeStruct((b, t, WIDTH), BF16),
        scratch_shapes=[pltpu.VMEM((2, t, LANES), BF16), pltpu.VMEM((t, LANES), BF16)],
        compiler_params=_cparams(("parallel", "parallel", "arbitrary")),
        name="fox_prompt",
    )(qkv, qkv, qkv, cum, cum)


def _page_sums_kernel(lf_ref, rev_ref, tot_ref):
    n = lf_ref.shape[1]
    a = lax.broadcasted_iota(jnp.int32, (n, n), 0)
    c = lax.broadcasted_iota(jnp.int32, (n, n), 1)
    hi, mid, lo = _split3(lf_ref[...])
    later = (a > c).astype(BF16)
    rev_ref[...] = _dot(hi, later) + _dot(mid, later) + _dot(lo, later)
    every = jnp.ones((n, n), BF16)
    tot_ref[...] = _dot(hi, every) + _dot(mid, every) + _dot(lo, every)


def _page_sums(lf_rows, rows):
    n_rows, n = lf_rows.shape
    spec = pl.BlockSpec((rows, n), lambda i: (i, 0))
    return pl.pallas_call(
        _page_sums_kernel, grid=(n_rows // rows,), in_specs=[spec], out_specs=[spec, spec],
        out_shape=[jax.ShapeDtypeStruct((n_rows, n), F32)] * 2,
        compiler_params=_cparams(("parallel",)),
        name="page_sums",
    )(lf_rows)


def _fox_sample_kernel(pt_ref, q_ref, kn_ref, vn_ref, lfn_ref, *rest, pages_per_step):
    npp = pages_per_step
    k_refs = rest[:npp]
    v_refs = rest[npp:2 * npp]
    rev_refs = rest[2 * npp:3 * npp]
    tot_refs = rest[3 * npp:4 * npp]
    o_ref = rest[4 * npp]
    qrep, m_s, l_s, acc_s, suf_s, cn_s = rest[4 * npp + 1:]
    step = pl.program_id(1)
    nq = q_ref.shape[1]
    rows = N_HEADS * nq
    page = kn_ref.shape[1]
    row_head = lax.broadcasted_iota(jnp.int32, (rows, WIDTH), 0) // nq
    lane_head = lax.broadcasted_iota(jnp.int32, (rows, WIDTH), 1) // HEAD_DIM

    def rep_heads(x):
        return jnp.concatenate([jnp.broadcast_to(x[h:h + 1, :], (nq, x.shape[1])) for h in range(N_HEADS)], axis=0)

    def update(s_list, pv):
        m_old = m_s[...]
        m_new = functools.reduce(jnp.maximum, [jnp.max(s, axis=-1, keepdims=True) for s in s_list] + [m_old])
        p_list = [jnp.exp(s - m_new) for s in s_list]
        alpha = jnp.exp(m_old - m_new)
        l_s[...] = alpha * l_s[...] + functools.reduce(jnp.add, [jnp.sum(p, axis=-1, keepdims=True) for p in p_list])
        acc_s[...] = alpha * acc_s[...] + functools.reduce(jnp.add, [pv(u, p.astype(BF16)) for u, p in enumerate(p_list)])
        m_s[...] = m_new

    @pl.when(step == 0)
    def _():
        q = q_ref[0] * (HEAD_DIM ** -0.5)
        qrep[...] = jnp.where(row_head == lane_head, jnp.concatenate([q] * N_HEADS, axis=0), 0.0).astype(BF16)
        key_i = lax.broadcasted_iota(jnp.int32, (page, page), 0)
        key_j = lax.broadcasted_iota(jnp.int32, (page, page), 1)
        cn_row = _dot_x3(rep_heads(lfn_ref[0]), (key_i <= key_j).astype(BF16))
        rq = lax.broadcasted_iota(jnp.int32, (rows, page), 0) % nq
        kc = lax.broadcasted_iota(jnp.int32, (rows, page), 1)
        cn_col = jnp.sum(jnp.where(kc == rq, cn_row, 0.0), axis=-1, keepdims=True)
        cn_s[...] = cn_col
        suf_s[...] = jnp.zeros_like(suf_s)
        m_s[...] = jnp.full_like(m_s, -jnp.inf)
        l_s[...] = jnp.zeros_like(l_s)
        acc_s[...] = jnp.zeros_like(acc_s)
        s = _dot_nt(qrep[...], kn_ref[0].astype(BF16)) + cn_col - cn_row
        vn = vn_ref[0].astype(BF16)
        update([jnp.where(kc <= rq, s, -jnp.inf)], lambda u, p: _dot(p, vn))

    q_all = qrep[...]
    cn = cn_s[...]
    suf = suf_s[...]
    s_list = []
    for u in range(npp):
        kt = k_refs[u][0].reshape(WIDTH, page).astype(BF16)
        s_list.append(_dot(q_all, kt) + ((cn + suf) + rep_heads(rev_refs[u][0])))
        suf = suf + rep_heads(tot_refs[u][0])
    update(s_list, lambda u, p: _dot_nt(p, v_refs[u][0].reshape(WIDTH, page).astype(BF16)))
    suf_s[...] = suf

    @pl.when(step == pl.num_programs(1) - 1)
    def _():
        o_sel = jnp.where(row_head == lane_head, acc_s[...] / l_s[...], 0.0)
        out = o_sel[0:nq]
        for h in range(1, N_HEADS):
            out = out + o_sel[h * nq:(h + 1) * nq]
        o_ref[0] = out.astype(o_ref.dtype)


def _fox_sample(q, k_new, v_new, logf_new, cache_k, cache_v, cache_logf, page_table, pages_per_step):
    b, tn, _ = q.shape
    n_pool, page = cache_k.shape[:2]
    n_pages = page_table.shape[1]
    npp = pages_per_step
    assert n_pages % npp == 0 and tn <= page
    rows = N_HEADS * tn
    ck = cache_k.transpose(0, 2, 3, 1)
    cv = cache_v.transpose(0, 2, 3, 1)
    clf = cache_logf.transpose(0, 2, 1).reshape(n_pool * N_HEADS, page)
    sum_rows = 2048 if clf.shape[0] % 2048 == 0 else clf.shape[0]
    rev, tot = (a.reshape(n_pool, N_HEADS, page) for a in _page_sums(clf, sum_rows))
    pad_rows = lambda x: jnp.pad(x, ((0, 0), (0, page - tn), (0, 0)))
    lfn = jnp.pad(logf_new.transpose(0, 2, 1), ((0, 0), (0, 0), (0, page - tn)))

    def page_map(u, nd):
        return lambda bi, s, pt: (pt[bi, n_pages - 1 - (s * npp + u)],) + (0,) * nd

    tok_spec = lambda r: pl.BlockSpec((1, r, WIDTH), lambda bi, s, pt: (bi, 0, 0))
    in_specs = ([tok_spec(tn), tok_spec(page), tok_spec(page),
                 pl.BlockSpec((1, N_HEADS, page), lambda bi, s, pt: (bi, 0, 0))]
                + [pl.BlockSpec((1, N_HEADS, HEAD_DIM, page), page_map(u, 3)) for u in range(npp)] * 2
                + [pl.BlockSpec((1, N_HEADS, page), page_map(u, 2)) for u in range(npp)] * 2)
    grid_spec = pltpu.PrefetchScalarGridSpec(
        num_scalar_prefetch=1,
        grid=(b, n_pages // npp),
        in_specs=in_specs,
        out_specs=tok_spec(tn),
        scratch_shapes=[pltpu.VMEM((rows, WIDTH), BF16), pltpu.VMEM((rows, 1), F32), pltpu.VMEM((rows, 1), F32),
                        pltpu.VMEM((rows, WIDTH), F32), pltpu.VMEM((rows, page), F32), pltpu.VMEM((rows, 1), F32)])
    return pl.pallas_call(
        functools.partial(_fox_sample_kernel, pages_per_step=npp),
        grid_spec=grid_spec,
        out_shape=jax.ShapeDtypeStruct((b, tn, WIDTH), BF16),
        compiler_params=_cparams(("parallel", "arbitrary")),
        name="fox_sample",
    )(page_table, q, pad_rows(k_new), pad_rows(v_new), lfn, *([ck] * npp), *([cv] * npp), *([rev] * npp),
      *([tot] * npp))


def _merge_kernel(y_ref, bonus_ref, g_ref, yf_ref, gates_ref, x_ref, g1_ref, sh2_ref, sc2_ref,
                  lnw_ref, lnb_ref, seg_ref, wr_ref, wf_ref, wo_ref, n2_ref, wrt_ref,
                  x1_ref, h2_ref, lg_ref):
    seg = seg_ref[...]
    y = y_ref[0]
    mu = _dot_x3(y, seg) * (1.0 / HEAD_DIM)
    d = y - mu
    var = _dot_x3(d * d, seg) * (1.0 / HEAD_DIM)
    yn = d * lax.rsqrt(var + LNX_EPS) * lnw_ref[...] + lnb_ref[...]
    yr = ((yn + bonus_ref[0]) * g_ref[0]).astype(BF16)
    gates = gates_ref[0].astype(F32)
    d_model = x_ref.shape[2]
    merged = gates[:, :d_model] * _dot(yr, wr_ref[...]) + gates[:, d_model:] * _dot(yf_ref[0], wf_ref[...])
    x1 = x_ref[0] + g1_ref[0] * _dot(merged.astype(BF16), wo_ref[...])
    x1_ref[0] = x1
    h2 = _rmsnorm(x1, n2_ref[...]) * (1.0 + sc2_ref[0]) + sh2_ref[0]
    h2_ref[0] = h2
    hh, hl = _split2(h2)
    wrt = wrt_ref[...]
    lg_ref[0] = _dot(hh, wrt[0]) + (_dot(hh, wrt[1]) + _dot(hl, wrt[0]))


def _merge(y, bonus, g, yf, gates, x, g1, sh2, sc2, p, w_router_t, tm):
    b, t, d = x.shape
    per_tok = g1.shape[1] != 1
    mrows = tm if per_tok else 1
    mod_map = (lambda bi, i: (bi, i, 0)) if per_tok else (lambda bi, i: (bi, 0, 0))
    tok = lambda n: pl.BlockSpec((1, tm, n), lambda bi, i: (bi, i, 0))
    mod = pl.BlockSpec((1, mrows, d), mod_map)
    row = lambda a: a.reshape(1, -1)
    consts = [row(p["lnx_w"]), row(p["lnx_b"]), _seg_ones(WIDTH, HEAD_DIM), p["w_br_r"].astype(BF16),
              p["w_br_f"].astype(BF16), p["w_out"].astype(BF16), row(p["norm2_g"]), w_router_t]
    const_specs = [pl.BlockSpec(c.shape, (lambda bi, i: (0, 0)) if c.ndim == 2 else (lambda bi, i: (0, 0, 0)))
                   for c in consts]
    return pl.pallas_call(
        _merge_kernel,
        grid=(b, t // tm),
        in_specs=[tok(WIDTH), tok(WIDTH), tok(WIDTH), tok(WIDTH), tok(2 * d), tok(d), mod, mod, mod] + const_specs,
        out_specs=[tok(d), tok(d), tok(LANES)],
        out_shape=[jax.ShapeDtypeStruct((b, t, d), F32), jax.ShapeDtypeStruct((b, t, d), F32),
                   jax.ShapeDtypeStruct((b, t, LANES), F32)],
        compiler_params=_cparams(("parallel", "parallel")),
        name="merge",
    )(y, bonus, g, yf, gates, x, g1, sh2, sc2, *consts)


def _route_kernel(lg_ref, bias_ref, row_ref, rowt_ref, wt_ref, cnt_ref):
    lt = lg_ref[...].T
    tm = lt.shape[1]
    score = _sigmoid(lt[:N_EXPERTS])
    biased = score + bias_ref[...][:N_EXPERTS]
    slab = [biased[k * N_GROUPS:(k + 1) * N_GROUPS] for k in range(GROUP_SIZE)]
    neg = jnp.full((N_GROUPS, tm), -jnp.inf, F32)
    m1 = functools.reduce(jnp.maximum, slab)
    taken = jnp.zeros((N_GROUPS, tm), jnp.bool_)
    m2 = neg
    for k in range(GROUP_SIZE):
        is_first = (slab[k] == m1) & jnp.logical_not(taken)
        taken = taken | is_first
        m2 = jnp.maximum(m2, jnp.where(is_first, neg, slab[k]))
    gs = m1 + m2
    g_iota = lax.broadcasted_iota(jnp.int32, (N_GROUPS, tm), 0)
    cnt = jnp.zeros((N_GROUPS, tm), jnp.int32)
    for g2 in range(N_GROUPS):
        other = gs[g2:g2 + 1, :]
        beats = (other > gs) | ((g_iota > g2) & (other == gs))
        cnt = cnt + beats.astype(jnp.int32)
    g_sel = cnt < TOPK_GROUPS
    cand = [jnp.where(g_sel, slab[k], neg) for k in range(GROUP_SIZE)]
    rank = [jnp.zeros((N_GROUPS, tm), jnp.int32) for _ in range(GROUP_SIZE)]
    for k2 in range(GROUP_SIZE):
        for g2 in range(N_GROUPS):
            other = cand[k2][g2:g2 + 1, :]
            for k in range(GROUP_SIZE):
                first = (g_iota >= g2) if k2 < k else (g_iota > g2)
                beats = (other > cand[k]) | (first & (other == cand[k]))
                rank[k] = rank[k] + beats.astype(jnp.int32)
    sel = [rank[k] < TOP_K for k in range(GROUP_SIZE)]
    sc = [score[k * N_GROUPS:(k + 1) * N_GROUPS] for k in range(GROUP_SIZE)]
    picked = [jnp.where(sel[k], sc[k], 0.0) for k in range(GROUP_SIZE)]
    total = jnp.sum(functools.reduce(jnp.add, picked), axis=0, keepdims=True)
    gate = jnp.concatenate([pk / total * ROUTED_SCALE for pk in picked], axis=0)
    chosen = jnp.concatenate([s.astype(F32) for s in sel], axis=0)
    chosen_b = chosen.astype(BF16)
    ri = lax.broadcasted_iota(jnp.int32, (N_EXPERTS, N_EXPERTS), 0)
    ci = lax.broadcasted_iota(jnp.int32, (N_EXPERTS, N_EXPERTS), 1)
    lower = (ci < ri).astype(BF16)
    ordinal = _dot(lower, chosen_b)
    ta = lax.broadcasted_iota(jnp.int32, (tm, tm), 0)
    tc = lax.broadcasted_iota(jnp.int32, (tm, tm), 1)
    rank_tok = _dot(chosen_b, (ta < tc).astype(BF16))
    count = jnp.sum(chosen, axis=1, keepdims=True)
    runs = jnp.broadcast_to(jnp.floor((count + (RUN - 1)) * (1.0 / RUN)), (N_EXPERTS, LANES))
    cnt_ref[0] = runs
    start = _dot(lower, runs.astype(BF16))[:, 0:1] * float(RUN)
    tile_row = start + rank_tok
    r_rows, w_rows = [], []
    for n in range(TOP_K):
        hit = (chosen > 0.0) & (ordinal == float(n))
        r_rows.append(jnp.sum(jnp.where(hit, tile_row, 0.0), axis=0, keepdims=True))
        w_rows.append(jnp.sum(jnp.where(hit, gate, 0.0), axis=0, keepdims=True))
    zrow = jnp.zeros((1, tm), F32)
    pad8 = lambda rows: jnp.concatenate(rows + [zrow] * (8 - TOP_K), axis=0)
    zpad = jnp.zeros((LANES - 8, tm), F32)
    row_ref[...] = pad8(r_rows).astype(jnp.int32)
    rowt_ref[...] = jnp.concatenate([pad8(r_rows), zpad], axis=0).T.astype(jnp.int32)
    wt_ref[...] = jnp.concatenate([pad8(w_rows), zpad], axis=0).T


def _route(logits, e_bias_perm, tm):
    m = logits.shape[0]
    bias = jnp.pad(e_bias_perm, (0, LANES - N_EXPERTS)).reshape(LANES, 1)
    tok_spec = pl.BlockSpec((tm, LANES), lambda i: (i, 0))
    return pl.pallas_call(
        _route_kernel,
        grid=(m // tm,),
        in_specs=[tok_spec, pl.BlockSpec((LANES, 1), lambda i: (0, 0))],
        out_specs=[pl.BlockSpec((8, tm), lambda i: (0, i)), tok_spec, tok_spec,
                   pl.BlockSpec((1, N_EXPERTS, LANES), lambda i: (i, 0, 0))],
        out_shape=[jax.ShapeDtypeStruct((8, m), jnp.int32), jax.ShapeDtypeStruct((m, LANES), jnp.int32),
                   jax.ShapeDtypeStruct((m, LANES), F32), jax.ShapeDtypeStruct((m // tm, N_EXPERTS, LANES), F32)],
        compiler_params=_cparams(("parallel",)),
        name="route",
    )(logits, bias)


def _sorted_capacity(tm):
    rows = TOP_K * tm + N_EXPERTS * (RUN - 1)
    return -(-rows // SORT_BLOCK) * SORT_BLOCK


def _run_pieces(i, nrun_ref, src_ref, dst_ref, piece):
    def per_expert(e, total):
        j = i * N_EXPERTS + e
        n, s0, d0 = nrun_ref[j], src_ref[j], dst_ref[j]

        def one(c, carry):
            piece(s0 + c, d0 + c).start()
            return carry

        lax.fori_loop(0, n, one, 0)
        return total + n

    return lax.fori_loop(0, N_EXPERTS, per_expert, 0)


def _sort_kernel(nrun_ref, src_ref, dst_ref, nblk_ref, row_ref, h_ref, xs_in_ref, xs_ref, buf, pending, sem):
    del xs_in_ref
    i = pl.program_id(0)
    slot = i % 2
    tm = h_ref.shape[0]
    hb = h_ref[...].astype(BF16)
    rows = row_ref[...]
    riota = lax.broadcasted_iota(jnp.int32, (SORT_BLOCK, tm), 0)

    @pl.when(i == 0)
    def _():
        pending[0] = 0

    def block(b, carry):
        r0 = pl.multiple_of(b * SORT_BLOCK, SORT_BLOCK)
        hit = riota + r0 == rows[0:1, :]
        for n in range(1, TOP_K):
            hit = hit | (riota + r0 == rows[n:n + 1, :])
        buf[slot, pl.ds(r0, SORT_BLOCK), :] = _dot(hit.astype(BF16), hb).astype(BF16)
        return carry

    lax.fori_loop(0, nblk_ref[i], block, 0)

    def piece(s, d):
        return pltpu.make_async_copy(buf.at[slot, pl.ds(pl.multiple_of(s * RUN, RUN), RUN), :],
                                     xs_ref.at[pl.ds(pl.multiple_of(d * RUN, RUN), RUN), :], sem)

    def drain(c, carry):
        piece(0, 0).wait()
        return carry

    lax.fori_loop(0, pending[0], drain, 0)
    pending[0] = _run_pieces(i, nrun_ref, src_ref, dst_ref, piece)

    @pl.when(i == pl.num_programs(0) - 1)
    def _():
        lax.fori_loop(0, pending[0], drain, 0)


def _sort_rows(h2, row, tables, n_rows, tm):
    m, d = h2.shape
    grid_spec = pltpu.PrefetchScalarGridSpec(
        num_scalar_prefetch=4,
        grid=(m // tm,),
        in_specs=[pl.BlockSpec((8, tm), lambda i, *_: (0, i)), pl.BlockSpec((tm, d), lambda i, *_: (i, 0)),
                  pl.BlockSpec(memory_space=pl.ANY)],
        out_specs=pl.BlockSpec(memory_space=pl.ANY),
        scratch_shapes=[pltpu.VMEM((2, _sorted_capacity(tm), d), BF16), pltpu.SMEM((1,), jnp.int32),
                        pltpu.SemaphoreType.DMA(())])
    return pl.pallas_call(
        _sort_kernel,
        grid_spec=grid_spec,
        out_shape=jax.ShapeDtypeStruct((n_rows, d), BF16),
        input_output_aliases={6: 0},
        compiler_params=_cparams(("arbitrary",)),
        name="sort_rows",
    )(*tables, row, h2, jnp.zeros((n_rows, d), BF16))


def _expert_kernel(te_ref, nu_ref, x_ref, wg_ref, wu_ref, wd_ref, o_ref, wgb, wub, wdb):
    i = pl.program_id(0)
    changed = (i == 0) | (te_ref[i] != te_ref[jnp.maximum(i - 1, 0)])

    @pl.when(changed)
    def _():
        wgb[...] = wg_ref[0].astype(BF16)
        wub[...] = wu_ref[0].astype(BF16)
        wdb[...] = wd_ref[0].astype(BF16)

    @pl.when(i < nu_ref[0])
    def _():
        x = x_ref[...]
        act = _silu(_dot(x, wgb[...])) * _dot(x, wub[...])
        o_ref[...] = _dot(act.astype(BF16), wdb[...]).astype(o_ref.dtype)

    @pl.when(i >= nu_ref[0])
    def _():
        o_ref[...] = jnp.zeros_like(o_ref)


def _expert_tiles(xs, tile_expert, n_used, wg, wu, wd, te):
    n_rows, d = xs.shape
    ff = wg.shape[2]
    last = lambda i, nu: jnp.minimum(i, nu[0] - 1)
    grid_spec = pltpu.PrefetchScalarGridSpec(
        num_scalar_prefetch=2,
        grid=(n_rows // te,),
        in_specs=[pl.BlockSpec((te, d), lambda i, tx, nu: (last(i, nu), 0)),
                  pl.BlockSpec((1, d, ff), lambda i, tx, nu: (tx[i], 0, 0)),
                  pl.BlockSpec((1, d, ff), lambda i, tx, nu: (tx[i], 0, 0)),
                  pl.BlockSpec((1, ff, d), lambda i, tx, nu: (tx[i], 0, 0))],
        out_specs=pl.BlockSpec((te, d), lambda i, tx, nu: (i, 0)),
        scratch_shapes=[pltpu.VMEM((d, ff), BF16), pltpu.VMEM((d, ff), BF16), pltpu.VMEM((ff, d), BF16)])
    return pl.pallas_call(
        _expert_kernel,
        grid_spec=grid_spec,
        out_shape=jax.ShapeDtypeStruct((n_rows, d), BF16),
        compiler_params=_cparams(("arbitrary",)),
        name="expert_tiles",
    )(tile_expert, n_used, xs, wg, wu, wd)


def _combine_kernel(nrun_ref, src_ref, dst_ref, nblk_ref, rowt_ref, wt_ref, h_ref, x1_ref, g2_ref, nf_ref,
                    wsg_ref, wsu_ref, wsd_ref, os_ref, y_ref, buf, rowb, wb, sem):
    i = pl.program_id(0) * pl.num_programs(1) + pl.program_id(1)
    tm = h_ref.shape[1]
    nblk = nblk_ref[i]
    tail = pl.multiple_of((nblk - 1) * SORT_BLOCK, SORT_BLOCK)
    buf[pl.ds(tail, SORT_BLOCK), :] = jnp.zeros((SORT_BLOCK, buf.shape[1]), BF16)

    def piece(s, d):
        return pltpu.make_async_copy(os_ref.at[pl.ds(pl.multiple_of(d * RUN, RUN), RUN), :],
                                     buf.at[pl.ds(pl.multiple_of(s * RUN, RUN), RUN), :], sem)

    started = _run_pieces(i, nrun_ref, src_ref, dst_ref, piece)
    h = h_ref[0].astype(BF16)
    shared = _dot((_silu(_dot(h, wsg_ref[...])) * _dot(h, wsu_ref[...])).astype(BF16), wsd_ref[...])

    def drain(c, carry):
        piece(0, 0).wait()
        return carry

    w = wt_ref[0]
    rt = rowt_ref[0]
    for n in range(TOP_K):
        rowb[n] = jnp.broadcast_to(rt[:, n:n + 1], (tm, SORT_BLOCK))
        wb[n] = jnp.broadcast_to(w[:, n:n + 1], (tm, SORT_BLOCK))
    lax.fori_loop(0, started, drain, 0)
    ciota = lax.broadcasted_iota(jnp.int32, (tm, SORT_BLOCK), 1)

    def block(b, acc):
        r0 = pl.multiple_of(b * SORT_BLOCK, SORT_BLOCK)
        c = jnp.where(ciota == rowb[0] - r0, wb[0], 0.0)
        for n in range(1, TOP_K):
            c = c + jnp.where(ciota == rowb[n] - r0, wb[n], 0.0)
        return acc + _dot(c.astype(BF16), buf[pl.ds(r0, SORT_BLOCK), :])

    routed = lax.fori_loop(0, nblk, block, jnp.zeros((tm, buf.shape[1]), F32))
    x2 = x1_ref[0] + g2_ref[0] * (routed + shared)
    y_ref[0] = _rmsnorm(x2, nf_ref[...])


def _combine(os, rowt, wt, tables, h2, x1, g2, normf_g, wsg, wsu, wsd, tm):
    b, t, d = x1.shape
    nt = t // tm
    per_tok = g2.shape[1] != 1
    mrows = tm if per_tok else 1
    mod_map = (lambda bi, i, *_: (bi, i, 0)) if per_tok else (lambda bi, i, *_: (bi, 0, 0))
    tok = lambda n: pl.BlockSpec((1, tm, n), lambda bi, i, *_: (bi, i, 0))
    const = lambda a: pl.BlockSpec(a.shape, lambda bi, i, *_: (0, 0))
    grid_spec = pltpu.PrefetchScalarGridSpec(
        num_scalar_prefetch=4,
        grid=(b, nt),
        in_specs=[tok(LANES), tok(LANES), tok(d), tok(d), pl.BlockSpec((1, mrows, d), mod_map),
                  pl.BlockSpec((1, d), lambda bi, i, *_: (0, 0)), const(wsg), const(wsu), const(wsd),
                  pl.BlockSpec(memory_space=pl.ANY)],
        out_specs=tok(d),
        scratch_shapes=[pltpu.VMEM((_sorted_capacity(tm), d), BF16), pltpu.VMEM((TOP_K, tm, SORT_BLOCK), jnp.int32),
                        pltpu.VMEM((TOP_K, tm, SORT_BLOCK), F32), pltpu.SemaphoreType.DMA(())])
    return pl.pallas_call(
        _combine_kernel,
        grid_spec=grid_spec,
        out_shape=jax.ShapeDtypeStruct((b, t, d), F32),
        compiler_params=_cparams(("arbitrary", "arbitrary")),
        name="combine",
    )(*tables, rowt.reshape(b, t, LANES), wt.reshape(b, t, LANES), h2, x1, g2, normf_g.reshape(1, d), wsg, wsu, wsd, os)


def _moe(h2, logits, x1, g2, p, w, cfg):
    bx, tx, d = x1.shape
    m = bx * tx
    te, tm = cfg["te"], cfg["tm_route"]
    row, rowt, wt, cnt = _route(logits.reshape(m, LANES), w["e_bias_perm"], tm)
    runs = cnt[:, :, 0].astype(jnp.int32)
    src = jnp.cumsum(runs, axis=1) - runs
    nblk = (jnp.sum(runs, axis=1) * RUN + SORT_BLOCK - 1) // SORT_BLOCK
    per_expert = jnp.sum(runs, axis=0)
    tiles_e = (per_expert * RUN + te - 1) // te
    ends = jnp.cumsum(tiles_e)
    dst = ((ends - tiles_e) * (te // RUN))[None, :] + jnp.cumsum(runs, axis=0) - runs
    n_tiles = (m * TOP_K + (m // tm) * N_EXPERTS * (RUN - 1)) // te + N_EXPERTS
    n_used = ends[-1:].astype(jnp.int32)
    tile_ids = jnp.minimum(jnp.arange(n_tiles, dtype=jnp.int32), n_used[0] - 1)
    tile_row = jnp.minimum(jnp.sum(ends[None, :] <= tile_ids[:, None], axis=1), N_EXPERTS - 1)
    tile_expert = ((tile_row % N_GROUPS) * GROUP_SIZE + tile_row // N_GROUPS).astype(jnp.int32)
    flat = lambda a: a.reshape(-1).astype(jnp.int32)
    tables = (flat(runs), flat(src), flat(dst), flat(nblk))
    xs = _sort_rows(h2.reshape(m, d), row, tables, n_tiles * te, tm)
    os = _expert_tiles(xs, tile_expert, n_used, p["w_exp_gate"], p["w_exp_up"], p["w_exp_down"], te)
    return _combine(os, rowt, wt, tables, h2, x1, g2, p["normf_g"], w["sh_gate"], w["sh_up"], w["sh_down"], tm)


def _layer(x, mod, shift_prev, wkv0, attend, p, w, cfg):
    b, t, d = x.shape
    bx, tx = cfg["rows"]
    tm = cfg["tm"]
    xr = x.reshape(bx, tx, d)
    if bx == b:
        part = lambda i: mod[:, i:i + 1, :]
    else:
        part = lambda i: jnp.repeat(mod[:, i, :], t, axis=0).reshape(bx, tx, d)
    sh1, sc1, g1, sh2, sc2, g2 = (part(i) for i in range(6))
    h_rows = 1 if bx == b else tx
    proj = functools.partial(_in_proj, xr, sh1, sc1, p["norm1_g"], tm=cfg["tm_in"], h_rows=h_rows)
    pr, h_keep = proj(w["in_r"], None, tn=cfg["tn_r"], epilogue="none", out_dtype=F32)
    qkv, _ = proj(w["in_qkv"], None, tn=WIDTH, epilogue="none", out_dtype=F32, split=True)
    logf_pad, _ = proj(w["in_f"], w["b_f_pad"], tn=LANES, epilogue="log_sigmoid", out_dtype=F32)
    gates, _ = proj(w["in_g"], None, tn=1024, epilogue="sigmoid", out_dtype=BF16)
    shift_new = h_keep[:, 0, :] if bx == b else h_keep.reshape(b, t, d)[:, -1, :]
    logf = logf_pad.reshape(b, t, LANES)[:, :, :N_HEADS]
    qkv = qkv.reshape(3, b, t, WIDTH)

    prev = _dense(shift_prev, w["in_r_f32"], jnp.zeros((R_COLS,), F32), act=False, tn=R_COLS // 2)
    r, dec, k2, v, kk, kka, g, bonus = _rwkv_prep(pr.reshape(b, t, R_COLS), prev, p, cfg["tt"])
    y_scan, wkv_new = _wkv_scan(r, dec, k2, v, kk, kka, wkv0, cfg["tc"])

    y_f = attend(qkv, logf)

    rs = lambda a: a.reshape(bx, tx, a.shape[-1])
    x1, h2, logits = _merge(rs(y_scan), rs(bonus), rs(g), rs(y_f), gates, xr, g1, sh2, sc2, p, w["router_t"], tm)
    y = _moe(h2, logits, x1, g2, p, w, cfg)
    k_out = qkv[1].reshape(b, t, N_HEADS, HEAD_DIM)
    v_out = qkv[2].reshape(b, t, N_HEADS, HEAD_DIM)
    return y.reshape(b, t, d), k_out, v_out, logf, wkv_new, shift_new


def kernel(x_prompt, x_sample, c_prompt, c_sample, cache_k, cache_v, cache_logf, page_table, state_wkv, state_shift, w_ada, b_ada, norm1_g, w_in, mu_shift, w0, w_w2, a0, w_a2, w_g2, k_k, k_a, r_k, lnx_w, lnx_b, b_f, w_br_r, w_br_f, w_out, norm2_g, w_router, e_bias, w_exp_gate, w_exp_up, w_exp_down, w_sh_gate, w_sh_up, w_sh_down, normf_g):
    p = dict(norm1_g=norm1_g, mu_shift=mu_shift, w0=w0, w_w2=w_w2, a0=a0, w_a2=w_a2, w_g2=w_g2, k_k=k_k, k_a=k_a,
             r_k=r_k, lnx_w=lnx_w, lnx_b=lnx_b, w_br_r=w_br_r, w_br_f=w_br_f, w_out=w_out, norm2_g=norm2_g,
             normf_g=normf_g, w_exp_gate=w_exp_gate, w_exp_up=w_exp_up, w_exp_down=w_exp_down)
    bp, tp, d = x_prompt.shape
    bs, ts, _ = x_sample.shape
    off_f = R_COLS + 3 * WIDTH
    off_g = off_f + N_HEADS
    perm = lambda a: a.reshape(a.shape[:-1] + (N_GROUPS, GROUP_SIZE)).swapaxes(-1, -2).reshape(a.shape)
    router = jnp.pad(perm(w_router), ((0, 0), (0, LANES - N_EXPERTS)))
    r_hi = router.astype(BF16)
    w = dict(
        in_r=w_in[:, :R_COLS].astype(BF16), in_r_f32=w_in[:, :R_COLS],
        in_qkv=w_in[:, R_COLS:off_f].astype(BF16),
        in_f=jnp.pad(w_in[:, off_f:off_g], ((0, 0), (0, LANES - N_HEADS))).astype(BF16),
        in_g=w_in[:, off_g:].astype(BF16),
        b_f_pad=jnp.pad(b_f, (0, LANES - N_HEADS)).reshape(1, LANES),
        router_t=jnp.stack([r_hi, (router - r_hi.astype(F32)).astype(BF16)]),
        e_bias_perm=perm(e_bias),
        sh_gate=w_sh_gate.astype(BF16), sh_up=w_sh_up.astype(BF16), sh_down=w_sh_down.astype(BF16),
    )
    mod = _dense(jnp.concatenate([c_prompt, c_sample], axis=0), w_ada, b_ada, act=True).reshape(bp + bs, 6, d)

    def attend_prompt(qkv, logf):
        lf = jnp.pad(logf, ((0, 0), (0, 0), (0, LANES - N_HEADS)))
        return _fox_prompt(qkv, _cumsum_time(lf, 256), min(512, tp))

    def attend_sample(qkv, logf):
        return _fox_sample(qkv[0], qkv[1], qkv[2], logf,
                           cache_k, cache_v, cache_logf, page_table, min(16, page_table.shape[1]))

    cfg_p = dict(rows=(bp, tp), tm=min(512, tp), tm_in=min(1024, tp), tn_r=R_COLS // 2, tt=min(512, tp), tc=32,
                 tm_route=min(512, tp), te=512)
    cfg_s = dict(rows=(1, bs * ts), tm=bs * ts, tm_in=bs * ts, tn_r=R_COLS // 2, tt=ts, tc=ts, tm_route=bs * ts,
                 te=256)
    yp, kp, vp, lfp, wkvp, shp = _layer(x_prompt, mod[:bp], jnp.zeros((bp, d), F32),
                                        jnp.zeros((bp, N_HEADS, HEAD_DIM, HEAD_DIM), F32), attend_prompt, p, w, cfg_p)
    ys, ks, vs, lfs, wkvs, shs = _layer(x_sample, mod[bp:], state_shift, state_wkv, attend_sample, p, w, cfg_s)
    return (yp, ys, kp, vp, lfp, wkvp, shp, ks, vs, lfs, wkvs, shs)
```
